```python
import math
import jax, jax.numpy as jnp
from jax import lax
import numpy as np

D_MODEL = 1024
BATCH = 8
SEQ = 2048
DEPTH = 2
DEC_BATCH = 128
DEC_SEQ = 8
PAST_LEN = 16384
PAGE_SIZE = 128

D_MIX = D_MODEL
D_A = D_MIX // 2
N_A = 64
H_A = D_A // N_A
D_B = D_MIX - D_A
H_B = 4
DK_B = D_B // H_B
R_W = 64
R_A = 64
R_G = 128
RWKV_COLS = 3 * D_A + R_W + R_A + R_G
MLSTM_COLS = 4 * D_B + 2 * H_B
IN_COLS = RWKV_COLS + MLSTM_COLS
RWKV_SPLITS = [D_A, 2 * D_A, 3 * D_A, 3 * D_A + R_W, 3 * D_A + R_W + R_A]
MLSTM_SPLITS = [2 * D_B, 3 * D_B, 4 * D_B, 4 * D_B + H_B]
CONV_W = 4
CHUNK = 64
D_FF = 2816
ALPHA = (2.0 * DEPTH) ** 0.25
BETA = (8.0 * DEPTH) ** -0.25
LN_EPS = 1e-5
GN_EPS = 64e-5
MH_EPS = 1e-6

kernel_name = 'hymba_rwkv7_mlstm_macaron_deepnorm_step'


def _layer_norm(x, g, b):
    xf = x.astype(jnp.float32)
    mu = xf.mean(-1, keepdims=True)
    var = jnp.mean(jnp.square(xf - mu), -1, keepdims=True)
    return ((xf - mu) * lax.rsqrt(var + LN_EPS)).astype(x.dtype) * g + b


def _swiglu(x, wg, wu, wd):
    return (jax.nn.silu(x @ wg) * (x @ wu)) @ wd


def _rwkv7(p, shift_prev, wkv0, mu_shift, w0, w2, a0, a2, g2, k_k, k_a, r_k, gn_g, gn_b):
    B, T, _ = p.shape
    dt = p.dtype
    f32 = jnp.float32
    p_prev = jnp.concatenate([shift_prev[:, None, :].astype(dt), p[:, :-1]], axis=1)
    pm = p + (p_prev - p) * mu_shift
    r, k, v, xw, xa, xg = jnp.split(pm, RWKV_SPLITS, axis=-1)
    w_raw = -jax.nn.softplus(-(w0 + jnp.tanh(xw) @ w2)) - 0.5
    decay = jnp.exp(-jnp.exp(w_raw.astype(f32)))
    a = jax.nn.sigmoid(a0 + xa @ a2)
    g = jax.nn.sigmoid(xg) @ g2
    heads = lambda t: t.reshape(B, T, H_A, N_A)
    kk = heads(k * k_k).astype(f32)
    kk = kk * lax.rsqrt(jnp.maximum(jnp.sum(kk * kk, -1, keepdims=True), 1e-24))
    k = k * (1 + (a - 1) * k_a)
    rh, kh, vh, ah = heads(r), heads(k), heads(v), heads(a)
    tm = lambda t: jnp.moveaxis(t.astype(f32), 1, 0)

    def step(S, inp):
        r_t, w_t, k_t, v_t, kk_t, a_t = inp
        sa = jnp.einsum('bhvk,bhk->bhv', S, -kk_t)
        S = (S * w_t[:, :, None, :] + sa[..., None] * (kk_t * a_t)[:, :, None, :]
             + v_t[..., None] * k_t[:, :, None, :])
        return S, jnp.einsum('bhvk,bhk->bhv', S, r_t)

    S_T, ys = lax.scan(step, wkv0.astype(f32), (tm(rh), tm(heads(decay)), tm(kh), tm(vh), tm(kk), tm(ah)))
    y = jnp.moveaxis(ys, 0, 1)
    mu = y.mean(-1, keepdims=True)
    var = jnp.mean(jnp.square(y - mu), -1, keepdims=True)
    y = ((y - mu) * lax.rsqrt(var + GN_EPS)).reshape(B, T, D_A).astype(dt) * gn_g + gn_b
    bonus = (jnp.sum(rh * kh * r_k, -1, keepdims=True) * vh).reshape(B, T, D_A)
    out = (y + bonus) * g
    return out, p[:, -1], S_T.astype(wkv0.dtype)


def _mlstm(p, conv_prev, C0, n0, m0, conv_w, conv_b, i_bias, f_bias, mh_g):
    B, T, _ = p.shape
    dt = p.dtype
    f32 = jnp.float32
    qk_pre, v, o_pre, i_pre, f_pre = jnp.split(p, MLSTM_SPLITS, axis=-1)
    ext = jnp.concatenate([conv_prev.astype(dt), qk_pre], axis=1)
    conv = conv_b + sum(ext[:, j:j + T] * conv_w[j] for j in range(CONV_W))
    q, k = jnp.split(jax.nn.silu(conv), 2, axis=-1)
    k = k * DK_B ** -0.5
    log_i = (i_pre + i_bias).astype(f32)
    log_f = jax.nn.log_sigmoid((f_pre + f_bias).astype(f32))
    L = math.gcd(T, CHUNK)
    NC = T // L
    ch4 = lambda t: t.astype(f32).reshape(B, NC, L, H_B, DK_B).transpose(1, 0, 3, 2, 4)
    ch3 = lambda t: t.reshape(B, NC, L, H_B).transpose(1, 0, 3, 2)
    causal = jnp.tril(jnp.ones((L, L), dtype=bool))

    def chunk_step(carry, inp):
        C, n, m = carry
        qc, kc, vc, lic, lfc = inp
        b = jnp.cumsum(lfc, axis=-1)
        dmat = jnp.where(causal, b[..., :, None] - b[..., None, :] + lic[..., None, :], -jnp.inf)
        inter = b + m[..., None]
        m_t = jnp.maximum(inter, dmat.max(-1))
        g_inter = jnp.exp(inter - m_t)
        s = jnp.einsum('bhtd,bhsd->bhts', qc, kc) * jnp.exp(dmat - m_t[..., None])
        num = g_inter[..., None] * jnp.einsum('bhtd,bhde->bhte', qc, C) + jnp.einsum('bhts,bhse->bhte', s, vc)
        den = g_inter * jnp.einsum('bhtd,bhd->bht', qc, n) + s.sum(-1)
        h = num / jnp.maximum(jnp.abs(den), jnp.exp(-m_t))[..., None]
        bL = b[..., -1]
        last = bL[..., None] - b + lic
        m_new = jnp.maximum(bL + m, last.max(-1))
        ws = jnp.exp(last - m_new[..., None])
        dec = jnp.exp(bL + m - m_new)
        C = dec[..., None, None] * C + jnp.einsum('bhs,bhsd,bhse->bhde', ws, kc, vc)
        n = dec[..., None] * n + jnp.einsum('bhs,bhsd->bhd', ws, kc)
        return (C, n, m_new), h

    (C_T, n_T, m_T), hs = lax.scan(
        chunk_step, (C0.astype(f32), n0.astype(f32), m0.astype(f32)),
        (ch4(q), ch4(k), ch4(v), ch3(log_i), ch3(log_f)))
    h = hs.transpose(1, 0, 3, 2, 4).reshape(B, T, H_B, DK_B)
    mu = h.mean(-1, keepdims=True)
    var = jnp.mean(jnp.square(h - mu), -1, keepdims=True)
    h = ((h - mu) * lax.rsqrt(var + MH_EPS)).reshape(B, T, D_B).astype(dt) * mh_g
    out = jax.nn.sigmoid(o_pre) * h
    return out, ext[:, T:], C_T.astype(C0.dtype), n_T.astype(n0.dtype), m_T.astype(m0.dtype)


def _layer(x, st, w):
    x = _layer_norm(ALPHA * x + 0.5 * _swiglu(x, w['ffn1_wg'], w['ffn1_wu'], w['ffn1_wd']), w['ln1_g'], w['ln1_b'])
    p = x @ w['w_in']
    ya, shift, wkv = _rwkv7(p[..., :RWKV_COLS], st[0], st[1], w['mu_shift'], w['w0'], w['w2'], w['a0'],
                            w['a2'], w['g2'], w['k_k'], w['k_a'], w['r_k'], w['gn_g'], w['gn_b'])
    yb, conv, C, n, m = _mlstm(p[..., RWKV_COLS:], st[2], st[3], st[4], st[5], w['conv_w'], w['conv_b'],
                               w['i_bias'], w['f_bias'], w['mh_g'])
    mix = jnp.concatenate([ya, yb], axis=-1) @ w['w_out']
    x = _layer_norm(ALPHA * x + mix, w['ln2_g'], w['ln2_b'])
    x = _layer_norm(ALPHA * x + 0.5 * _swiglu(x, w['ffn2_wg'], w['ffn2_wu'], w['ffn2_wd']), w['ln3_g'], w['ln3_b'])
    return x, (shift, wkv, conv, C, n, m)


def _trunk(x, states, weights):
    new = [[] for _ in range(len(states))]
    for l in range(DEPTH):
        w = {name: arr[l] for name, arr in weights.items()}
        x, st_new = _layer(x, [s[l] for s in states], w)
        for j, s in enumerate(st_new):
            new[j].append(s)
    return x, [jnp.stack(s) for s in new]


def setup_inputs(seed: int = 0) -> dict:
    key = jax.random.key(seed)
    ks = iter(jax.random.split(key, 48))
    nrm = lambda shape, s: jax.random.normal(next(ks), shape, jnp.float32) * s
    Lr = DEPTH
    speed = -7.0 + 5.0 * jnp.linspace(0.0, 1.0, D_A) ** 0.9 + 0.5
    return {
        'x_prompt': nrm((BATCH, SEQ, D_MODEL), 1.0),
        'x_sample': nrm((DEC_BATCH, DEC_SEQ, D_MODEL), 1.0),
        'state_shift': nrm((Lr, DEC_BATCH, RWKV_COLS), 1.0),
        'state_wkv': nrm((Lr, DEC_BATCH, H_A, N_A, N_A), 0.3),
        'state_conv': nrm((Lr, DEC_BATCH, CONV_W - 1, 2 * D_B), 1.0),
        'state_C': nrm((Lr, DEC_BATCH, H_B, DK_B, DK_B), 0.1),
        'state_n': nrm((Lr, DEC_BATCH, H_B, DK_B), 0.3),
        'state_m': nrm((Lr, DEC_BATCH, H_B), 1.0),
        'ffn1_wg': nrm((Lr, D_MODEL, D_FF), D_MODEL ** -0.5),
        'ffn1_wu': nrm((Lr, D_MODEL, D_FF), D_MODEL ** -0.5),
        'ffn1_wd': nrm((Lr, D_FF, D_MODEL), BETA * D_FF ** -0.5),
        'ln1_g': 1.0 + nrm((Lr, D_MODEL), 0.02),
        'ln1_b': nrm((Lr, D_MODEL), 0.02),
        'w_in': nrm((Lr, D_MODEL, IN_COLS), D_MODEL ** -0.5),
        'mu_shift': jax.random.uniform(next(ks), (Lr, RWKV_COLS), jnp.float32),
        'w0': speed[None, :] + nrm((Lr, D_A), 0.1),
        'w2': nrm((Lr, R_W, D_A), 0.1 * R_W ** -0.5),
        'a0': nrm((Lr, D_A), 0.1),
        'a2': nrm((Lr, R_A, D_A), 0.1 * R_A ** -0.5),
        'g2': nrm((Lr, R_G, D_A), R_G ** -0.5),
        'k_k': 0.85 + nrm((Lr, D_A), 0.05),
        'k_a': 1.0 + nrm((Lr, D_A), 0.05),
        'r_k': nrm((Lr, H_A, N_A), 0.1),
        'gn_g': 1.0 + nrm((Lr, D_A), 0.02),
        'gn_b': nrm((Lr, D_A), 0.02),
        'conv_w': nrm((Lr, CONV_W, 2 * D_B), CONV_W ** -0.5),
        'conv_b': nrm((Lr, 2 * D_B), 0.02),
        'i_bias': nrm((Lr, H_B), 0.1),
        'f_bias': jnp.linspace(3.0, 6.0, H_B)[None, :] + nrm((Lr, H_B), 0.1),
        'mh_g': 1.0 + nrm((Lr, D_B), 0.02),
        'w_out': nrm((Lr, D_MIX, D_MODEL), BETA * D_MIX ** -0.5),
        'ln2_g': 1.0 + nrm((Lr, D_MODEL), 0.02),
        'ln2_b': nrm((Lr, D_MODEL), 0.02),
        'ffn2_wg': nrm((Lr, D_MODEL, D_FF), D_MODEL ** -0.5),
        'ffn2_wu': nrm((Lr, D_MODEL, D_FF), D_MODEL ** -0.5),
        'ffn2_wd': nrm((Lr, D_FF, D_MODEL), BETA * D_FF ** -0.5),
        'ln3_g': 1.0 + nrm((Lr, D_MODEL), 0.02),
        'ln3_b': nrm((Lr, D_MODEL), 0.02),
    }


def reference(x_prompt, x_sample, state_shift, state_wkv, state_conv, state_C, state_n, state_m,
              ffn1_wg, ffn1_wu, ffn1_wd, ln1_g, ln1_b, w_in, mu_shift, w0, w2, a0, a2, g2, k_k, k_a, r_k,
              gn_g, gn_b, conv_w, conv_b, i_bias, f_bias, mh_g, w_out, ln2_g, ln2_b,
              ffn2_wg, ffn2_wu, ffn2_wd, ln3_g, ln3_b):
    weights = dict(ffn1_wg=ffn1_wg, ffn1_wu=ffn1_wu, ffn1_wd=ffn1_wd, ln1_g=ln1_g, ln1_b=ln1_b, w_in=w_in,
                   mu_shift=mu_shift, w0=w0, w2=w2, a0=a0, a2=a2, g2=g2, k_k=k_k, k_a=k_a, r_k=r_k,
                   gn_g=gn_g, gn_b=gn_b, conv_w=conv_w, conv_b=conv_b, i_bias=i_bias, f_bias=f_bias,
                   mh_g=mh_g, w_out=w_out, ln2_g=ln2_g, ln2_b=ln2_b, ffn2_wg=ffn2_wg, ffn2_wu=ffn2_wu,
                   ffn2_wd=ffn2_wd, ln3_g=ln3_g, ln3_b=ln3_b)
    B = x_prompt.shape[0]
    dt = x_prompt.dtype
    init = [jnp.zeros((DEPTH, B, RWKV_COLS), dt),
            jnp.zeros((DEPTH, B, H_A, N_A, N_A), dt),
            jnp.zeros((DEPTH, B, CONV_W - 1, 2 * D_B), dt),
            jnp.zeros((DEPTH, B, H_B, DK_B, DK_B), dt),
            jnp.zeros((DEPTH, B, H_B, DK_B), dt),
            jnp.zeros((DEPTH, B, H_B), dt)]
    y_prompt, (p_shift, p_wkv, p_conv, p_C, p_n, p_m) = _trunk(x_prompt, init, weights)
    y_sample, (s_shift, s_wkv, s_conv, s_C, s_n, s_m) = _trunk(
        x_sample, [state_shift, state_wkv, state_conv, state_C, state_n, state_m], weights)
    return (y_prompt, y_sample, p_shift, p_wkv, p_conv, p_C, p_n, p_m,
            s_shift, s_wkv, s_conv, s_C, s_n, s_m)
```

```python
import functools
import math

import jax
import jax.numpy as jnp
from jax import lax
from jax.experimental import pallas as pl
from jax.experimental.pallas import tpu as pltpu

D_MODEL = 1024
DEPTH = 2
D_A = 512
N_A = 64
H_A = 8
D_B = 512
H_B = 4
DK_B = 128
R_W = 64
R_A = 64
R_G = 128
RWKV_COLS = 3 * D_A + R_W + R_A + R_G
CONV_W = 4
CHUNK = 64
D_FF = 2816
ALPHA = (2.0 * DEPTH) ** 0.25
LN_EPS = 1e-5
GN_EPS = 64e-5
MH_EPS = 1e-6

GATE_COLS = 256
ML_COLS = 4 * D_B
P_COLS = RWKV_COLS + GATE_COLS + ML_COLS
VMEM_LIMIT = 56 * 1024 * 1024

F32 = jnp.float32
BF16 = jnp.bfloat16


def _bdot(a, b):
    return jnp.dot(a.astype(BF16), b.astype(BF16), preferred_element_type=F32)


def _dot_exact_rhs(x, m_bf16):
    hi = x.astype(BF16)
    lo = (x - hi.astype(F32)).astype(BF16)
    return (jnp.dot(hi, m_bf16, preferred_element_type=F32)
            + jnp.dot(lo, m_bf16, preferred_element_type=F32))


def _layer_norm_rows(y, g, b):
    mu = jnp.mean(y, axis=-1, keepdims=True)
    d = y - mu
    var = jnp.mean(d * d, axis=-1, keepdims=True)
    return d * lax.rsqrt(var + LN_EPS) * g + b


def _softplus(x):
    return jnp.maximum(x, 0.0) + jnp.log1p(jnp.exp(-jnp.abs(x)))


def _sigmoid(x):
    return 1.0 / (1.0 + jnp.exp(-x))


def _ffn_ln_kernel(x_ref, wg_ref, wu_ref, wd_ref, g_ref, b_ref, o_ref, xb_scr, acc_scr):
    j = pl.program_id(1)

    @pl.when(j == 0)
    def _():
        xb_scr[...] = x_ref[...].astype(BF16)
        acc_scr[...] = jnp.zeros_like(acc_scr)

    xb = xb_scr[...]
    hg = jnp.dot(xb, wg_ref[...], preferred_element_type=F32)
    hu = jnp.dot(xb, wu_ref[...], preferred_element_type=F32)
    h = (hg * _sigmoid(hg)) * hu
    acc_scr[...] += jnp.dot(h.astype(BF16), wd_ref[...], preferred_element_type=F32)

    @pl.when(j == pl.num_programs(1) - 1)
    def _():
        y = ALPHA * x_ref[...] + 0.5 * acc_scr[...]
        o_ref[...] = _layer_norm_rows(y, g_ref[...], b_ref[...])


def _ffn_ln(x, wg, wu, wd, g, b, tm, tf=256):
    n = x.shape[0]
    return pl.pallas_call(
        _ffn_ln_kernel,
        grid=(n // tm, D_FF // tf),
        in_specs=[
            pl.BlockSpec((tm, D_MODEL), lambda i, j: (i, 0)),
            pl.BlockSpec((D_MODEL, tf), lambda i, j: (0, j)),
            pl.BlockSpec((D_MODEL, tf), lambda i, j: (0, j)),
            pl.BlockSpec((tf, D_MODEL), lambda i, j: (j, 0)),
            pl.BlockSpec((1, D_MODEL), lambda i, j: (0, 0)),
            pl.BlockSpec((1, D_MODEL), lambda i, j: (0, 0)),
        ],
        out_specs=pl.BlockSpec((tm, D_MODEL), lambda i, j: (i, 0)),
        out_shape=jax.ShapeDtypeStruct((n, D_MODEL), F32),
        scratch_shapes=[pltpu.VMEM((tm, D_MODEL), BF16), pltpu.VMEM((tm, D_MODEL), F32)],
        compiler_params=pltpu.CompilerParams(
            dimension_semantics=("parallel", "arbitrary"), vmem_limit_bytes=VMEM_LIMIT),
        name="ffn_ln",
    )(x, wg, wu, wd, g, b)


def _in_proj_kernel(x_ref, w_ref, o_ref, xb_scr):
    @pl.when(pl.program_id(1) == 0)
    def _():
        xb_scr[...] = x_ref[...].astype(BF16)

    o_ref[...] = jnp.dot(xb_scr[...], w_ref[...], preferred_element_type=F32)


def _in_proj(x, w, tm, tn=512):
    n = x.shape[0]
    return pl.pallas_call(
        _in_proj_kernel,
        grid=(n // tm, P_COLS // tn),
        in_specs=[
            pl.BlockSpec((tm, D_MODEL), lambda i, j: (i, 0)),
            pl.BlockSpec((D_MODEL, tn), lambda i, j: (0, j)),
        ],
        out_specs=pl.BlockSpec((tm, tn), lambda i, j: (i, j)),
        out_shape=jax.ShapeDtypeStruct((n, P_COLS), F32),
        scratch_shapes=[pltpu.VMEM((tm, D_MODEL), BF16)],
        compiler_params=pltpu.CompilerParams(
            dimension_semantics=("parallel", "arbitrary"), vmem_limit_bytes=VMEM_LIMIT),
        name="in_proj",
    )(x, w)


def _out_proj_ln_kernel(x_ref, ya_ref, yb_ref, wa_ref, wb_ref, g_ref, b_ref, o_ref):
    mix = (jnp.dot(ya_ref[...].astype(BF16), wa_ref[...], preferred_element_type=F32)
           + jnp.dot(yb_ref[...].astype(BF16), wb_ref[...], preferred_element_type=F32))
    o_ref[...] = _layer_norm_rows(ALPHA * x_ref[...] + mix, g_ref[...], b_ref[...])


def _out_proj_ln(x, ya, yb, wa, wb, g, b, tm):
    n = x.shape[0]
    return pl.pallas_call(
        _out_proj_ln_kernel,
        grid=(n // tm,),
        in_specs=[
            pl.BlockSpec((tm, D_MODEL), lambda i: (i, 0)),
            pl.BlockSpec((tm, D_A), lambda i: (i, 0)),
            pl.BlockSpec((tm, D_B), lambda i: (i, 0)),
            pl.BlockSpec((D_A, D_MODEL), lambda i: (0, 0)),
            pl.BlockSpec((D_B, D_MODEL), lambda i: (0, 0)),
            pl.BlockSpec((1, D_MODEL), lambda i: (0, 0)),
            pl.BlockSpec((1, D_MODEL), lambda i: (0, 0)),
        ],
        out_specs=pl.BlockSpec((tm, D_MODEL), lambda i: (i, 0)),
        out_shape=jax.ShapeDtypeStruct((n, D_MODEL), F32),
        compiler_params=pltpu.CompilerParams(
            dimension_semantics=("parallel",), vmem_limit_bytes=VMEM_LIMIT),
        name="out_proj_ln",
    )(x, ya, yb, wa, wb, g, b)


def _rwkv_kernel(p_ref, sp_ref, s0_ref, mu_ref, w0_ref, w2a_ref, a0_ref, g2_ref, kk_ref, ka_ref, rk_ref,
                 gng_ref, gnb_ref, seg_ref,
                 ya_ref, so_ref, st_ref,
                 s_scr, carry_scr, nk_scr, rp_scr, b_scr, k_scr, w_scr, v_scr, vsw_scr, bonus_scr, g_scr,
                 y0_scr, y1_scr, *, nb, tb):
    j = pl.program_id(1)
    n_hp = H_A // 2

    @pl.when(j == 0)
    def _():
        s_scr[...] = s0_ref[...]
        carry_scr[...] = sp_ref[...]

    seg = seg_ref[...]
    lane128 = lax.broadcasted_iota(jnp.int32, (tb, 128), 1)
    rowid = lax.broadcasted_iota(jnp.int32, (tb, RWKV_COLS), 0)
    rowid_a = lax.broadcasted_iota(jnp.int32, (tb, D_A), 0)

    r_last = []
    for b in range(nb):
        p = p_ref[b]
        prev = jnp.where(rowid == 0, carry_scr[b], pltpu.roll(p, 1, axis=0))
        carry_scr[b] = p[tb - 1:tb, :]
        pm = p + (prev - p) * mu_ref[...]
        r = pm[:, 0:D_A]
        k = pm[:, D_A:2 * D_A]
        v = pm[:, 2 * D_A:3 * D_A]
        z = pm[:, 3 * D_A:3 * D_A + 128]
        xg = pm[:, 3 * D_A + 128:RWKV_COLS]
        zt = jnp.where(lane128 < R_W, jnp.tanh(z), z)
        lr = _bdot(zt, w2a_ref[...])
        w_raw = -_softplus(-(w0_ref[...] + lr[:, :D_A])) - 0.5
        w = jnp.exp(-jnp.exp(w_raw))
        a = _sigmoid(a0_ref[...] + lr[:, D_A:])
        g_scr[b] = _bdot(_sigmoid(xg), g2_ref[...])
        kk = k * kk_ref[...]
        ss = _dot_exact_rhs(kk * kk, seg)
        kk = kk * lax.rsqrt(jnp.maximum(ss, 1e-24))
        kmod = k * (1.0 + (a - 1.0) * ka_ref[...])
        bonus_scr[b] = _dot_exact_rhs(r * kmod * rk_ref[...], seg) * v
        nk_scr[b] = -kk
        rp_scr[b] = jnp.where(rowid_a == 0, 0.0, pltpu.roll(r, 1, axis=0))
        b_scr[b] = kk * a
        k_scr[b] = kmod
        w_scr[b] = w
        v_scr[b] = v
        for hp in range(n_hp):
            vsw_scr[b, :, hp * 128:(hp + 1) * 128] = pltpu.roll(v[:, hp * 128:(hp + 1) * 128], 64, axis=1)
        r_last.append(r[tb - 1:tb, :])

    row8 = lax.broadcasted_iota(jnp.int32, (8, 128), 0)
    lane8 = lax.broadcasted_iota(jnp.int32, (8, 128), 1)
    m_a = (((row8 == 0) & (lane8 < 64)) | ((row8 == 1) & (lane8 >= 64))).astype(F32)
    m_b = (((row8 == 2) & (lane8 < 64)) | ((row8 == 3) & (lane8 >= 64))).astype(F32)
    row64 = lax.broadcasted_iota(jnp.int32, (8, 64), 0)
    r01 = (row64 < 2).astype(F32)
    e2 = (row64 == 2).astype(F32)
    e3 = (row64 == 3).astype(F32)
    chains = [(b, hp) for b in range(nb) for hp in range(n_hp)]
    nt_dims = (((1,), (1,)), ((), ()))
    tn_dims = (((0,), (0,)), ((), ()))

    def group(t8, carry):
        ts = pl.ds(pl.multiple_of(t8 * 8, 8), 8)
        tiles = {}
        for (b, hp) in chains:
            sl = pl.ds(hp * 128, 128)
            tiles[(b, hp)] = tuple(ref[b, ts, sl] for ref in
                                   (nk_scr, rp_scr, b_scr, k_scr, w_scr, v_scr, vsw_scr))
        y0 = {c: [] for c in chains}
        y1 = {c: [] for c in chains}
        for i in range(8):
            reds = {}
            for c in chains:
                nk8, rp8 = tiles[c][0], tiles[c][1]
                a_mat = nk8[i:i + 1] * m_a + rp8[i:i + 1] * m_b
                reds[c] = lax.dot_general(a_mat.astype(BF16), s_scr[c[0], c[1]].astype(BF16), nt_dims,
                                          preferred_element_type=F32)
            xs = {}
            for c in chains:
                vv8, vs8 = tiles[c][5], tiles[c][6]
                x_mat = reds[c] * r01 + vv8[i:i + 1, :64] * e2 + vs8[i:i + 1, :64] * e3
                xs[c] = x_mat.astype(BF16)
            for c in chains:
                bb8, kk8, ww8 = tiles[c][2], tiles[c][3], tiles[c][4]
                y_mat = bb8[i:i + 1] * m_a + kk8[i:i + 1] * m_b
                d_s = lax.dot_general(xs[c], y_mat.astype(BF16), tn_dims, preferred_element_type=F32)
                s_scr[c[0], c[1]] = s_scr[c[0], c[1]] * ww8[i:i + 1] + d_s
                y0[c].append(reds[c][2:3, :])
                y1[c].append(reds[c][3:4, :])
        for c in chains:
            y0_scr[c[0], c[1], ts, :] = jnp.concatenate(y0[c], axis=0)
            y1_scr[c[0], c[1], ts, :] = jnp.concatenate(y1[c], axis=0)
        return carry

    lax.fori_loop(0, tb // 8, group, 0)

    for (b, hp) in chains:
        rl = r_last[b][:, hp * 128:(hp + 1) * 128]
        a_mat = rl * m_b
        red = lax.dot_general(a_mat.astype(BF16), s_scr[b, hp].astype(BF16), nt_dims, preferred_element_type=F32)
        y0_scr[b, hp, pl.ds(tb, 8), :] = jnp.broadcast_to(red[2:3, :], (8, 64))
        y1_scr[b, hp, pl.ds(tb, 8), :] = jnp.broadcast_to(red[3:4, :], (8, 64))

    for b in range(nb):
        pieces = []
        for hp in range(n_hp):
            pieces.append(y0_scr[b, hp, pl.ds(1, tb), :])
            pieces.append(y1_scr[b, hp, pl.ds(1, tb), :])
        y = jnp.concatenate(pieces, axis=1)
        mu = _dot_exact_rhs(y, seg) * (1.0 / N_A)
        d = y - mu
        var = _dot_exact_rhs(d * d, seg) * (1.0 / N_A)
        yn = d * lax.rsqrt(var + GN_EPS) * gng_ref[...] + gnb_ref[...]
        ya_ref[b] = (yn + bonus_scr[b]) * g_scr[b]

    @pl.when(j == pl.num_programs(1) - 1)
    def _():
        st_ref[...] = s_scr[...]
        so_ref[...] = carry_scr[...]


def _rwkv(p3, shift_prev, s0, wts, nb, tb):
    bsz, t, _ = p3.shape
    n_hp = H_A // 2
    full = lambda shape: pl.BlockSpec(shape, lambda i, j: (0,) * len(shape))
    blk = lambda: pltpu.VMEM((nb, tb, D_A), F32)
    kern = functools.partial(_rwkv_kernel, nb=nb, tb=tb)
    return pl.pallas_call(
        kern,
        grid=(bsz // nb, t // tb),
        in_specs=[
            pl.BlockSpec((nb, tb, RWKV_COLS), lambda i, j: (i, j, 0)),
            pl.BlockSpec((nb, 1, RWKV_COLS), lambda i, j: (i, 0, 0)),
            pl.BlockSpec((nb, n_hp, N_A, 128), lambda i, j: (i, 0, 0, 0)),
            full((1, RWKV_COLS)), full((1, D_A)), full((128, 2 * D_A)), full((1, D_A)), full((R_G, D_A)),
            full((1, D_A)), full((1, D_A)), full((1, D_A)), full((1, D_A)), full((1, D_A)), full((D_A, D_A)),
        ],
        out_specs=[
            pl.BlockSpec((nb, tb, D_A), lambda i, j: (i, j, 0)),
            pl.BlockSpec((nb, 1, RWKV_COLS), lambda i, j: (i, 0, 0)),
            pl.BlockSpec((nb, n_hp, N_A, 128), lambda i, j: (i, 0, 0, 0)),
        ],
        out_shape=[
            jax.ShapeDtypeStruct((bsz, t, D_A), F32),
            jax.ShapeDtypeStruct((bsz, 1, RWKV_COLS), F32),
            jax.ShapeDtypeStruct((bsz, n_hp, N_A, 128), F32),
        ],
        scratch_shapes=[
            pltpu.VMEM((nb, n_hp, N_A, 128), F32),
            pltpu.VMEM((nb, 1, RWKV_COLS), F32),
            blk(), blk(), blk(), blk(), blk(), blk(), blk(), blk(), blk(),
            pltpu.VMEM((nb, n_hp, tb + 8, N_A), F32),
            pltpu.VMEM((nb, n_hp, tb + 8, N_A), F32),
        ],
        compiler_params=pltpu.CompilerParams(
            dimension_semantics=("parallel", "arbitrary"), vmem_limit_bytes=VMEM_LIMIT),
        name="rwkv7",
    )(p3, shift_prev, s0, *wts)


def _mlstm_kernel(pm_ref, pg_ref, cp_ref, c0_ref, n0_ref, m0_ref, cw_ref, cb_ref, gb_ref, mhg_ref,
                  yb_ref, co_ref, ct_ref, nt_ref, mt_ref,
                  x_scr, c_scr, n_scr, m_scr, *, nb, tb, lc):
    j = pl.program_id(1)
    hi = lax.Precision.HIGHEST

    @pl.when(j == 0)
    def _():
        c_scr[...] = c0_ref[...]
        n_scr[...] = n0_ref[...]
        m_scr[...] = jnp.broadcast_to(m0_ref[...], m_scr.shape)
        x_scr[:, 5:8, :] = cp_ref[...]

    rr = lax.broadcasted_iota(jnp.int32, (lc, lc), 0)
    cc = lax.broadcasted_iota(jnp.int32, (lc, lc), 1)
    causal = rr >= cc
    tri = causal.astype(F32)
    lane_g = lax.broadcasted_iota(jnp.int32, (tb, 128), 1)
    tn_dims = (((0,), (0,)), ((), ()))
    nt_dims = (((1,), (1,)), ((), ()))

    for b in range(nb):
        x = pm_ref[b, :, 0:2 * D_B]
        x_scr[b, pl.ds(8, tb), :] = x
        conv = cb_ref[...] + x * cw_ref[3:4, :]
        for s in range(1, CONV_W):
            conv = conv + x_scr[b, pl.ds(8 - s, tb), :] * cw_ref[3 - s:4 - s, :]
        x_scr[b, 5:8, :] = x_scr[b, pl.ds(8 + tb - 3, 3), :]
        sc = conv * _sigmoid(conv)
        q_all = sc[:, :D_B]
        k_all = sc[:, D_B:] * (DK_B ** -0.5)
        v_all = pm_ref[b, :, 2 * D_B:3 * D_B]
        o_all = pm_ref[b, :, 3 * D_B:4 * D_B]
        gp = pg_ref[b, :, 0:128] + gb_ref[...]
        gates = jnp.where(lane_g < H_B, gp, jnp.minimum(gp, 0.0) - jnp.log1p(jnp.exp(-jnp.abs(gp))))

        for c in range(tb // lc):
            rs = slice(c * lc, (c + 1) * lc)
            gc = gates[rs]
            csum_col = jnp.dot(tri, gc, precision=hi, preferred_element_type=F32)
            gct = gc.T
            csum_row = lax.dot_general(gct, tri, nt_dims, precision=hi, preferred_element_type=F32)
            for h in range(H_B):
                hs = slice(h * DK_B, (h + 1) * DK_B)
                qc, kc, vc = q_all[rs, hs], k_all[rs, hs], v_all[rs, hs]
                li_col = gc[:, h:h + 1]
                b_col = csum_col[:, H_B + h:H_B + h + 1]
                li_row = gct[h:h + 1, :]
                b_row = csum_row[H_B + h:H_B + h + 1, :]
                m_prev = m_scr[b, h]
                m_p = m_prev[:, 0:1]
                c_prev = c_scr[b, h]
                n_prev = n_scr[b, h]

                dmat = jnp.where(causal, b_col - b_row + li_row, -jnp.inf)
                inter = b_col + m_p
                m_t = jnp.maximum(inter, jnp.max(dmat, axis=-1, keepdims=True))
                g_inter = jnp.exp(inter - m_t)
                s = lax.dot_general(qc.astype(BF16), kc.astype(BF16), nt_dims,
                                    preferred_element_type=F32) * jnp.exp(dmat - m_t)
                num = g_inter * _bdot(qc, c_prev) + _bdot(s, vc)
                den = g_inter * jnp.sum(qc * n_prev, axis=-1, keepdims=True) + jnp.sum(s, axis=-1, keepdims=True)
                hh = num / jnp.maximum(jnp.abs(den), jnp.exp(-m_t))

                b_l = b_col[lc - 1:lc, :]
                last_row = b_l - b_row + li_row
                m_new = jnp.maximum(b_l + m_p, jnp.max(last_row, axis=-1, keepdims=True))
                ws_col = jnp.exp(b_l - b_col + li_col - m_new)
                dec = jnp.exp(b_l + m_p - m_new)
                kw = kc * ws_col
                c_scr[b, h] = dec * c_prev + lax.dot_general(kw.astype(BF16), vc.astype(BF16), tn_dims,
                                                             preferred_element_type=F32)
                n_scr[b, h] = dec * n_prev + jnp.sum(kw, axis=0, keepdims=True)
                m_scr[b, h] = jnp.broadcast_to(m_new, (1, 128))

                mu = jnp.mean(hh, axis=-1, keepdims=True)
                d = hh - mu
                var = jnp.mean(d * d, axis=-1, keepdims=True)
                hn = d * lax.rsqrt(var + MH_EPS) * mhg_ref[:, hs]
                yb_ref[b, rs, hs] = _sigmoid(o_all[rs, hs]) * hn

    @pl.when(j == pl.num_programs(1) - 1)
    def _():
        ct_ref[...] = c_scr[...]
        nt_ref[...] = n_scr[...]
        mt_ref[...] = m_scr[:, :, :, 0:1]
        co_ref[...] = x_scr[:, 5:8, :]


def _mlstm(p3, conv_prev, c0, n0, m0, wts, nb, tb, lc):
    bsz, t, _ = p3.shape
    full = lambda shape: pl.BlockSpec(shape, lambda i, j: (0,) * len(shape))
    kern = functools.partial(_mlstm_kernel, nb=nb, tb=tb, lc=lc)
    gate_blk = RWKV_COLS // GATE_COLS
    return pl.pallas_call(
        kern,
        grid=(bsz // nb, t // tb),
        in_specs=[
            pl.BlockSpec((nb, tb, ML_COLS), lambda i, j: (i, j, 1)),
            pl.BlockSpec((nb, tb, GATE_COLS), lambda i, j: (i, j, gate_blk)),
            pl.BlockSpec((nb, CONV_W - 1, 2 * D_B), lambda i, j: (i, 0, 0)),
            pl.BlockSpec((nb, H_B, DK_B, DK_B), lambda i, j: (i, 0, 0, 0)),
            pl.BlockSpec((nb, H_B, 1, DK_B), lambda i, j: (i, 0, 0, 0)),
            pl.BlockSpec((nb, H_B, 1, 1), lambda i, j: (i, 0, 0, 0)),
            full((CONV_W, 2 * D_B)), full((1, 2 * D_B)), full((1, 128)), full((1, D_B)),
        ],
        out_specs=[
            pl.BlockSpec((nb, tb, D_B), lambda i, j: (i, j, 0)),
            pl.BlockSpec((nb, CONV_W - 1, 2 * D_B), lambda i, j: (i, 0, 0)),
            pl.BlockSpec((nb, H_B, DK_B, DK_B), lambda i, j: (i, 0, 0, 0)),
            pl.BlockSpec((nb, H_B, 1, DK_B), lambda i, j: (i, 0, 0, 0)),
            pl.BlockSpec((nb, H_B, 1, 1), lambda i, j: (i, 0, 0, 0)),
        ],
        out_shape=[
            jax.ShapeDtypeStruct((bsz, t, D_B), F32),
            jax.ShapeDtypeStruct((bsz, CONV_W - 1, 2 * D_B), F32),
            jax.ShapeDtypeStruct((bsz, H_B, DK_B, DK_B), F32),
            jax.ShapeDtypeStruct((bsz, H_B, 1, DK_B), F32),
            jax.ShapeDtypeStruct((bsz, H_B, 1, 1), F32),
        ],
        scratch_shapes=[
            pltpu.VMEM((nb, tb + 8, 2 * D_B), F32),
            pltpu.VMEM((nb, H_B, DK_B, DK_B), F32),
            pltpu.VMEM((nb, H_B, 1, DK_B), F32),
            pltpu.VMEM((nb, H_B, 1, 128), F32),
        ],
        compiler_params=pltpu.CompilerParams(
            dimension_semantics=("parallel", "arbitrary"), vmem_limit_bytes=VMEM_LIMIT),
        name="mlstm",
    )(p3, p3, conv_prev, c0, n0, m0, *wts)


def _pack_wkv(s):
    bsz = s.shape[0]
    return s.reshape(bsz, H_A // 2, 2, N_A, N_A).transpose(0, 1, 3, 2, 4).reshape(bsz, H_A // 2, N_A, 2 * N_A)


def _unpack_wkv(s):
    bsz = s.shape[0]
    return s.reshape(bsz, H_A // 2, N_A, 2, N_A).transpose(0, 1, 3, 2, 4).reshape(bsz, H_A, N_A, N_A)


def _layer_weights(l, w):
    bf = lambda a: a.astype(BF16)
    row = lambda a: a.reshape(1, -1)
    w_in = w['w_in'][l]
    w_gates = jnp.pad(w_in[:, RWKV_COLS + ML_COLS:], ((0, 0), (0, GATE_COLS - 2 * H_B)))
    w_cat = jnp.concatenate([w_in[:, :RWKV_COLS], w_gates, w_in[:, RWKV_COLS:RWKV_COLS + ML_COLS]], axis=1)
    zero = jnp.zeros((R_W, D_A), F32)
    w2a = jnp.concatenate([jnp.concatenate([w['w2'][l], zero], axis=1),
                           jnp.concatenate([zero, w['a2'][l]], axis=1)], axis=0)
    ids = jnp.arange(D_A) // N_A
    seg = (ids[:, None] == ids[None, :]).astype(BF16)
    gate_bias = jnp.pad(jnp.concatenate([w['i_bias'][l], w['f_bias'][l]]), (0, 128 - 2 * H_B)).reshape(1, 128)
    return dict(
        ffn1=(bf(w['ffn1_wg'][l]), bf(w['ffn1_wu'][l]), bf(w['ffn1_wd'][l]), row(w['ln1_g'][l]), row(w['ln1_b'][l])),
        ffn2=(bf(w['ffn2_wg'][l]), bf(w['ffn2_wu'][l]), bf(w['ffn2_wd'][l]), row(w['ln3_g'][l]), row(w['ln3_b'][l])),
        w_cat=bf(w_cat),
        rwkv=(row(w['mu_shift'][l]), row(w['w0'][l]), bf(w2a), row(w['a0'][l]), bf(w['g2'][l]), row(w['k_k'][l]),
              row(w['k_a'][l]), row(w['r_k'][l]), row(w['gn_g'][l]), row(w['gn_b'][l]), seg),
        mlstm=(w['conv_w'][l], row(w['conv_b'][l]), gate_bias, row(w['mh_g'][l])),
        out=(bf(w['w_out'][l][:D_A]), bf(w['w_out'][l][D_A:]), row(w['ln2_g'][l]), row(w['ln2_b'][l])),
    )


def _trunk(x, states, lw, nb_r, tb_r, nb_m, tb_m):
    bsz, t, _ = x.shape
    n = bsz * t
    tm = min(1024, n // 2)
    lc = math.gcd(t, CHUNK)
    xf = x.reshape(n, D_MODEL)
    new = [[] for _ in range(6)]
    for l in range(DEPTH):
        wl = lw[l]
        x1 = _ffn_ln(xf, *wl['ffn1'], tm=tm)
        p3 = _in_proj(x1, wl['w_cat'], tm=tm).reshape(bsz, t, P_COLS)
        ya, shift, wkv = _rwkv(p3, states[0][l].reshape(bsz, 1, RWKV_COLS), _pack_wkv(states[1][l]),
                               wl['rwkv'], nb_r, tb_r)
        yb, conv, c_t, n_t, m_t = _mlstm(p3, states[2][l], states[3][l],
                                         states[4][l].reshape(bsz, H_B, 1, DK_B),
                                         states[5][l].reshape(bsz, H_B, 1, 1), wl['mlstm'], nb_m, tb_m, lc)
        x2 = _out_proj_ln(x1, ya.reshape(n, D_A), yb.reshape(n, D_B), *wl['out'], tm=tm)
        xf = _ffn_ln(x2, *wl['ffn2'], tm=tm)
        for idx, s in enumerate((shift.reshape(bsz, RWKV_COLS), _unpack_wkv(wkv), conv, c_t,
                                 n_t.reshape(bsz, H_B, DK_B), m_t.reshape(bsz, H_B))):
            new[idx].append(s)
    return xf.reshape(bsz, t, D_MODEL), [jnp.stack(s) for s in new]


def kernel(x_prompt, x_sample, state_shift, state_wkv, state_conv, state_C, state_n, state_m,
           ffn1_wg, ffn1_wu, ffn1_wd, ln1_g, ln1_b, w_in, mu_shift, w0, w2, a0, a2, g2, k_k, k_a, r_k,
           gn_g, gn_b, conv_w, conv_b, i_bias, f_bias, mh_g, w_out, ln2_g, ln2_b,
           ffn2_wg, ffn2_wu, ffn2_wd, ln3_g, ln3_b):
    w = dict(ffn1_wg=ffn1_wg, ffn1_wu=ffn1_wu, ffn1_wd=ffn1_wd, ln1_g=ln1_g, ln1_b=ln1_b, w_in=w_in,
             mu_shift=mu_shift, w0=w0, w2=w2, a0=a0, a2=a2, g2=g2, k_k=k_k, k_a=k_a, r_k=r_k,
             gn_g=gn_g, gn_b=gn_b, conv_w=conv_w, conv_b=conv_b, i_bias=i_bias, f_bias=f_bias,
             mh_g=mh_g, w_out=w_out, ln2_g=ln2_g, ln2_b=ln2_b, ffn2_wg=ffn2_wg, ffn2_wu=ffn2_wu,
             ffn2_wd=ffn2_wd, ln3_g=ln3_g, ln3_b=ln3_b)
    lw = [_layer_weights(l, w) for l in range(DEPTH)]
    bp = x_prompt.shape[0]
    dt = x_prompt.dtype
    init = [jnp.zeros((DEPTH, bp, RWKV_COLS), dt),
            jnp.zeros((DEPTH, bp, H_A, N_A, N_A), dt),
            jnp.zeros((DEPTH, bp, CONV_W - 1, 2 * D_B), dt),
            jnp.zeros((DEPTH, bp, H_B, DK_B, DK_B), dt),
            jnp.zeros((DEPTH, bp, H_B, DK_B), dt),
            jnp.zeros((DEPTH, bp, H_B), dt)]
    y_p, ps = _trunk(x_prompt, init, lw, nb_r=8, tb_r=64, nb_m=1, tb_m=256)
    y_s, ss = _trunk(x_sample, [state_shift, state_wkv, state_conv, state_C, state_n, state_m], lw,
                     nb_r=8, tb_r=8, nb_m=8, tb_m=8)
    return (y_p, y_s, *ps, *ss)
```

```python
import functools
import math

import jax
import jax.numpy as jnp
from jax import lax
from jax.experimental import pallas as pl
from jax.experimental.pallas import tpu as pltpu

D_MODEL = 1024
DEPTH = 2
D_A = 512
N_A = 64
H_A = 8
D_B = 512
H_B = 4
DK_B = 128
R_W = 64
R_A = 64
R_G = 128
RWKV_COLS = 3 * D_A + R_W + R_A + R_G
CONV_W = 4
CHUNK = 64
D_FF = 2816
ALPHA = (2.0 * DEPTH) ** 0.25
LN_EPS = 1e-5
GN_EPS = 64e-5
MH_EPS = 1e-6

GATE_COLS = 256
ML_COLS = 4 * D_B
P_COLS = RWKV_COLS + GATE_COLS + ML_COLS
VMEM_LIMIT = 56 * 1024 * 1024

F32 = jnp.float32
BF16 = jnp.bfloat16


def _bdot(a, b):
    return jnp.dot(a.astype(BF16), b.astype(BF16), preferred_element_type=F32)


def _dot_exact_rhs(x, m_bf16):
    hi = x.astype(BF16)
    lo = (x - hi.astype(F32)).astype(BF16)
    return (jnp.dot(hi, m_bf16, preferred_element_type=F32)
            + jnp.dot(lo, m_bf16, preferred_element_type=F32))


def _layer_norm_rows(y, g, b):
    mu = jnp.mean(y, axis=-1, keepdims=True)
    d = y - mu
    var = jnp.mean(d * d, axis=-1, keepdims=True)
    return d * lax.rsqrt(var + LN_EPS) * g + b


def _softplus(x):
    return jnp.maximum(x, 0.0) + jnp.log1p(jnp.exp(-jnp.abs(x)))


def _sigmoid(x):
    return 1.0 / (1.0 + jnp.exp(-x))


def _ffn_ln_kernel(x_ref, wg_ref, wu_ref, wd_ref, g_ref, b_ref, o_ref, xb_scr, acc_scr):
    j = pl.program_id(1)

    @pl.when(j == 0)
    def _():
        xb_scr[...] = x_ref[...].astype(BF16)
        acc_scr[...] = jnp.zeros_like(acc_scr)

    xb = xb_scr[...]
    hg = jnp.dot(xb, wg_ref[...], preferred_element_type=F32)
    hu = jnp.dot(xb, wu_ref[...], preferred_element_type=F32)
    h = (hg * _sigmoid(hg)) * hu
    acc_scr[...] += jnp.dot(h.astype(BF16), wd_ref[...], preferred_element_type=F32)

    @pl.when(j == pl.num_programs(1) - 1)
    def _():
        y = ALPHA * x_ref[...] + 0.5 * acc_scr[...]
        o_ref[...] = _layer_norm_rows(y, g_ref[...], b_ref[...])


def _ffn_ln(x, wg, wu, wd, g, b, l, tm, tf=256):
    n = x.shape[0]
    return pl.pallas_call(
        _ffn_ln_kernel,
        grid=(n // tm, D_FF // tf),
        in_specs=[
            pl.BlockSpec((tm, D_MODEL), lambda i, j: (i, 0)),
            pl.BlockSpec((None, D_MODEL, tf), lambda i, j: (l, 0, j)),
            pl.BlockSpec((None, D_MODEL, tf), lambda i, j: (l, 0, j)),
            pl.BlockSpec((None, tf, D_MODEL), lambda i, j: (l, j, 0)),
            pl.BlockSpec((1, D_MODEL), lambda i, j: (0, 0)),
            pl.BlockSpec((1, D_MODEL), lambda i, j: (0, 0)),
        ],
        out_specs=pl.BlockSpec((tm, D_MODEL), lambda i, j: (i, 0)),
        out_shape=jax.ShapeDtypeStruct((n, D_MODEL), F32),
        scratch_shapes=[pltpu.VMEM((tm, D_MODEL), BF16), pltpu.VMEM((tm, D_MODEL), F32)],
        compiler_params=pltpu.CompilerParams(
            dimension_semantics=("parallel", "arbitrary"), vmem_limit_bytes=VMEM_LIMIT),
        name="ffn_ln",
    )(x, wg, wu, wd, g, b)


def _in_proj_kernel(x_ref, w_ref, o_ref, xb_scr):
    @pl.when(pl.program_id(1) == 0)
    def _():
        xb_scr[...] = x_ref[...].astype(BF16)

    o_ref[...] = jnp.dot(xb_scr[...], w_ref[...], preferred_element_type=F32)


def _in_proj(x, w, tm, tn=512):
    n = x.shape[0]
    return pl.pallas_call(
        _in_proj_kernel,
        grid=(n // tm, P_COLS // tn),
        in_specs=[
            pl.BlockSpec((tm, D_MODEL), lambda i, j: (i, 0)),
            pl.BlockSpec((D_MODEL, tn), lambda i, j: (0, j)),
        ],
        out_specs=pl.BlockSpec((tm, tn), lambda i, j: (i, j)),
        out_shape=jax.ShapeDtypeStruct((n, P_COLS), F32),
        scratch_shapes=[pltpu.VMEM((tm, D_MODEL), BF16)],
        compiler_params=pltpu.CompilerParams(
            dimension_semantics=("parallel", "arbitrary"), vmem_limit_bytes=VMEM_LIMIT),
        name="in_proj",
    )(x, w)


def _out_proj_ln_kernel(x_ref, ya_ref, yb_ref, wa_ref, wb_ref, g_ref, b_ref, o_ref):
    mix = (jnp.dot(ya_ref[...].astype(BF16), wa_ref[...], preferred_element_type=F32)
           + jnp.dot(yb_ref[...].astype(BF16), wb_ref[...], preferred_element_type=F32))
    o_ref[...] = _layer_norm_rows(ALPHA * x_ref[...] + mix, g_ref[...], b_ref[...])


def _out_proj_ln(x, ya, yb, wa, wb, g, b, tm):
    n = x.shape[0]
    return pl.pallas_call(
        _out_proj_ln_kernel,
        grid=(n // tm,),
        in_specs=[
            pl.BlockSpec((tm, D_MODEL), lambda i: (i, 0)),
            pl.BlockSpec((tm, D_A), lambda i: (i, 0)),
            pl.BlockSpec((tm, D_B), lambda i: (i, 0)),
            pl.BlockSpec((D_A, D_MODEL), lambda i: (0, 0)),
            pl.BlockSpec((D_B, D_MODEL), lambda i: (0, 0)),
            pl.BlockSpec((1, D_MODEL), lambda i: (0, 0)),
            pl.BlockSpec((1, D_MODEL), lambda i: (0, 0)),
        ],
        out_specs=pl.BlockSpec((tm, D_MODEL), lambda i: (i, 0)),
        out_shape=jax.ShapeDtypeStruct((n, D_MODEL), F32),
        compiler_params=pltpu.CompilerParams(
            dimension_semantics=("parallel",), vmem_limit_bytes=VMEM_LIMIT),
        name="out_proj_ln",
    )(x, ya, yb, wa, wb, g, b)


def _rwkv_kernel(*refs, nb, tb, n_alias):
    (p_ref, sp_ref, s0_ref, mu_ref, w0_ref, w2a_ref, a0_ref, g2_ref, kk_ref, ka_ref, rk_ref,
     gng_ref, gnb_ref, seg_ref) = refs[:14]
    (ya_ref, so_ref, st_ref,
     s_scr, carry_scr, nk_scr, rp_scr, b_scr, k_scr, w_scr, v_scr, vsw_scr, bonus_scr, g_scr,
     y0_scr, y1_scr) = refs[14 + n_alias:]
    j = pl.program_id(1)
    n_hp = H_A // 2

    @pl.when(j == 0)
    def _():
        for b in range(nb):
            for hp in range(n_hp):
                s_scr[b, hp] = jnp.concatenate([s0_ref[b, 2 * hp], s0_ref[b, 2 * hp + 1]], axis=-1)
        carry_scr[...] = sp_ref[...]

    seg = seg_ref[...]
    rows = nb * tb
    lane128 = lax.broadcasted_iota(jnp.int32, (rows, 128), 1)
    rowid = lax.broadcasted_iota(jnp.int32, (tb, RWKV_COLS), 0)
    rowid_a = lax.broadcasted_iota(jnp.int32, (tb, D_A), 0)

    pms, rps, r_last = [], [], []
    for b in range(nb):
        p = p_ref[b]
        prev = jnp.where(rowid == 0, carry_scr[b], pltpu.roll(p, 1, axis=0))
        carry_scr[b] = p[tb - 1:tb, :]
        pm_b = p + (prev - p) * mu_ref[...]
        r_b = pm_b[:, 0:D_A]
        pms.append(pm_b)
        rps.append(jnp.where(rowid_a == 0, 0.0, pltpu.roll(r_b, 1, axis=0)))
        r_last.append(r_b[tb - 1:tb, :])
    pm = jnp.concatenate(pms, axis=0)
    r = pm[:, 0:D_A]
    k = pm[:, D_A:2 * D_A]
    v = pm[:, 2 * D_A:3 * D_A]
    z = pm[:, 3 * D_A:3 * D_A + 128]
    xg = pm[:, 3 * D_A + 128:RWKV_COLS]
    zt = jnp.where(lane128 < R_W, jnp.tanh(z), z)
    lr = _bdot(zt, w2a_ref[...])
    g_scr[...] = _bdot(_sigmoid(xg), g2_ref[...]).reshape(nb, tb, D_A)
    kk = k * kk_ref[...]
    ss = _dot_exact_rhs(kk * kk, seg)
    w_raw = -_softplus(-(w0_ref[...] + lr[:, :D_A])) - 0.5
    w = jnp.exp(-jnp.exp(w_raw))
    a = _sigmoid(a0_ref[...] + lr[:, D_A:])
    kk = kk * lax.rsqrt(jnp.maximum(ss, 1e-24))
    kmod = k * (1.0 + (a - 1.0) * ka_ref[...])
    bonus_scr[...] = (_dot_exact_rhs(r * kmod * rk_ref[...], seg) * v).reshape(nb, tb, D_A)
    nk_scr[...] = (-kk).reshape(nb, tb, D_A)
    rp_scr[...] = jnp.concatenate(rps, axis=0).reshape(nb, tb, D_A)
    b_scr[...] = (kk * a).reshape(nb, tb, D_A)
    k_scr[...] = kmod.reshape(nb, tb, D_A)
    w_scr[...] = w.reshape(nb, tb, D_A)
    v_scr[...] = v.reshape(nb, tb, D_A)
    for hp in range(n_hp):
        vsw_scr[:, :, hp * 128:(hp + 1) * 128] = pltpu.roll(
            v[:, hp * 128:(hp + 1) * 128], 64, axis=1).reshape(nb, tb, 128)

    row8 = lax.broadcasted_iota(jnp.int32, (8, 128), 0)
    lane8 = lax.broadcasted_iota(jnp.int32, (8, 128), 1)
    m_a = (((row8 == 0) & (lane8 < 64)) | ((row8 == 1) & (lane8 >= 64))).astype(F32)
    m_b = (((row8 == 2) & (lane8 < 64)) | ((row8 == 3) & (lane8 >= 64))).astype(F32)
    row64 = lax.broadcasted_iota(jnp.int32, (8, 64), 0)
    r01 = (row64 < 2).astype(F32)
    e2 = (row64 == 2).astype(F32)
    e3 = (row64 == 3).astype(F32)
    chains = [(b, hp) for b in range(nb) for hp in range(n_hp)]
    nt_dims = (((1,), (1,)), ((), ()))
    tn_dims = (((0,), (0,)), ((), ()))

    def group(t8, carry):
        ts = pl.ds(pl.multiple_of(t8 * 8, 8), 8)
        tiles = {}
        for (b, hp) in chains:
            sl = pl.ds(hp * 128, 128)
            tiles[(b, hp)] = tuple(ref[b, ts, sl] for ref in
                                   (nk_scr, rp_scr, b_scr, k_scr, w_scr, v_scr, vsw_scr))
        y0 = {c: [] for c in chains}
        y1 = {c: [] for c in chains}
        for i in range(8):
            reds = {}
            for c in chains:
                nk8, rp8 = tiles[c][0], tiles[c][1]
                a_mat = nk8[i:i + 1] * m_a + rp8[i:i + 1] * m_b
                reds[c] = lax.dot_general(a_mat.astype(BF16), s_scr[c[0], c[1]].astype(BF16), nt_dims,
                                          preferred_element_type=F32)
            xs = {}
            for c in chains:
                vv8, vs8 = tiles[c][5], tiles[c][6]
                x_mat = reds[c] * r01 + vv8[i:i + 1, :64] * e2 + vs8[i:i + 1, :64] * e3
                xs[c] = x_mat.astype(BF16)
            for c in chains:
                bb8, kk8, ww8 = tiles[c][2], tiles[c][3], tiles[c][4]
                y_mat = bb8[i:i + 1] * m_a + kk8[i:i + 1] * m_b
                d_s = lax.dot_general(xs[c], y_mat.astype(BF16), tn_dims, preferred_element_type=F32)
                s_scr[c[0], c[1]] = s_scr[c[0], c[1]] * ww8[i:i + 1] + d_s
                y0[c].append(reds[c][2:3, :])
                y1[c].append(reds[c][3:4, :])
        for c in chains:
            y0_scr[c[0], c[1], ts, :] = jnp.concatenate(y0[c], axis=0)
            y1_scr[c[0], c[1], ts, :] = jnp.concatenate(y1[c], axis=0)
        return carry

    lax.fori_loop(0, tb // 8, group, 0)

    for (b, hp) in chains:
        rl = r_last[b][:, hp * 128:(hp + 1) * 128]
        a_mat = rl * m_b
        red = lax.dot_general(a_mat.astype(BF16), s_scr[b, hp].astype(BF16), nt_dims, preferred_element_type=F32)
        y0_scr[b, hp, pl.ds(tb, 8), :] = jnp.broadcast_to(red[2:3, :], (8, 64))
        y1_scr[b, hp, pl.ds(tb, 8), :] = jnp.broadcast_to(red[3:4, :], (8, 64))

    ys = []
    for b in range(nb):
        pieces = []
        for hp in range(n_hp):
            pieces.append(y0_scr[b, hp, pl.ds(1, tb), :])
            pieces.append(y1_scr[b, hp, pl.ds(1, tb), :])
        ys.append(jnp.concatenate(pieces, axis=1))
    y = jnp.concatenate(ys, axis=0)
    mu = _dot_exact_rhs(y, seg) * (1.0 / N_A)
    d = y - mu
    var = _dot_exact_rhs(d * d, seg) * (1.0 / N_A)
    yn = d * lax.rsqrt(var + GN_EPS) * gng_ref[...] + gnb_ref[...]
    ya_ref[...] = (yn.reshape(nb, tb, D_A) + bonus_scr[...]) * g_scr[...]

    @pl.when(j == pl.num_programs(1) - 1)
    def _():
        for b in range(nb):
            for hp in range(n_hp):
                s_pair = s_scr[b, hp]
                st_ref[b, 2 * hp] = s_pair[:, :N_A]
                st_ref[b, 2 * hp + 1] = s_pair[:, N_A:]
        so_ref[...] = carry_scr[...]


def _rwkv(p3, shift_prev, wkv_all, l, wkv_out_prev, wts, nb, tb):
    bsz, t, _ = p3.shape
    n_hp = H_A // 2
    full = lambda shape: pl.BlockSpec(shape, lambda i, j: (0,) * len(shape))
    blk = lambda: pltpu.VMEM((nb, tb, D_A), F32)
    n_alias = 0 if wkv_out_prev is None else 1
    kern = functools.partial(_rwkv_kernel, nb=nb, tb=tb, n_alias=n_alias)
    state_spec = pl.BlockSpec((None, nb, H_A, N_A, N_A), lambda i, j: (l, i, 0, 0, 0))
    in_specs = [
        pl.BlockSpec((nb, tb, RWKV_COLS), lambda i, j: (i, j, 0)),
        pl.BlockSpec((nb, 1, RWKV_COLS), lambda i, j: (i, 0, 0)),
        state_spec,
        full((1, RWKV_COLS)), full((1, D_A)), full((128, 2 * D_A)), full((1, D_A)), full((R_G, D_A)),
        full((1, D_A)), full((1, D_A)), full((1, D_A)), full((1, D_A)), full((1, D_A)), full((D_A, D_A)),
    ]
    args = [p3, shift_prev, wkv_all, *wts]
    aliases = {}
    if n_alias:
        in_specs.append(pl.BlockSpec(memory_space=pl.ANY))
        args.append(wkv_out_prev)
        aliases = {len(args) - 1: 2}
    return pl.pallas_call(
        kern,
        grid=(bsz // nb, t // tb),
        in_specs=in_specs,
        out_specs=[
            pl.BlockSpec((nb, tb, D_A), lambda i, j: (i, j, 0)),
            pl.BlockSpec((nb, 1, RWKV_COLS), lambda i, j: (i, 0, 0)),
            state_spec,
        ],
        out_shape=[
            jax.ShapeDtypeStruct((bsz, t, D_A), F32),
            jax.ShapeDtypeStruct((bsz, 1, RWKV_COLS), F32),
            jax.ShapeDtypeStruct((DEPTH, bsz, H_A, N_A, N_A), F32),
        ],
        scratch_shapes=[
            pltpu.VMEM((nb, n_hp, N_A, 128), F32),
            pltpu.VMEM((nb, 1, RWKV_COLS), F32),
            blk(), blk(), blk(), blk(), blk(), blk(), blk(), blk(), blk(),
            pltpu.VMEM((nb, n_hp, tb + 8, N_A), F32),
            pltpu.VMEM((nb, n_hp, tb + 8, N_A), F32),
        ],
        input_output_aliases=aliases,
        compiler_params=pltpu.CompilerParams(
            dimension_semantics=("parallel", "arbitrary"), vmem_limit_bytes=VMEM_LIMIT),
        name="rwkv7",
    )(*args)


def _mlstm_kernel(*refs, nb, tb, lc, n_alias):
    (pm_ref, pg_ref, cp_ref, c0_ref, n0_ref, m0_ref, cw_ref, cb_ref, gb_ref, mhg_ref) = refs[:10]
    (yb_ref, co_ref, ct_ref, nt_ref, mt_ref, x_scr, c_scr, n_scr, m_scr) = refs[10 + n_alias:]
    j = pl.program_id(1)
    hi = lax.Precision.HIGHEST
    nch = tb // lc
    lc_shift = lc.bit_length() - 1

    @pl.when(j == 0)
    def _():
        c_scr[...] = c0_ref[...]
        n_scr[...] = n0_ref[...]
        m_scr[...] = jnp.broadcast_to(m0_ref[...], m_scr.shape)
        x_scr[:, 5:8, :] = cp_ref[...]

    rr = lax.broadcasted_iota(jnp.int32, (lc, lc), 0)
    cc = lax.broadcasted_iota(jnp.int32, (lc, lc), 1)
    causal = rr >= cc
    rb = lax.broadcasted_iota(jnp.int32, (tb, tb), 0)
    cb = lax.broadcasted_iota(jnp.int32, (tb, tb), 1)
    tri_blk = ((rb >= cb) & (lax.shift_right_logical(rb, lc_shift) == lax.shift_right_logical(cb, lc_shift))
               ).astype(F32)
    lane_g = lax.broadcasted_iota(jnp.int32, (tb, 128), 1)
    tn_dims = (((0,), (0,)), ((), ()))
    nt_dims = (((1,), (1,)), ((), ()))

    units = [(b, c, h) for b in range(nb) for c in range(nch) for h in range(H_B)]
    q_u, k_u, v_u, o_u = {}, {}, {}, {}
    li_col, li_row, b_col, b_row, b_l = {}, {}, {}, {}, {}

    for b in range(nb):
        x = pm_ref[b, :, 0:2 * D_B]
        x_scr[b, pl.ds(8, tb), :] = x
        conv = cb_ref[...] + x * cw_ref[3:4, :]
        for s in range(1, CONV_W):
            conv = conv + x_scr[b, pl.ds(8 - s, tb), :] * cw_ref[3 - s:4 - s, :]
        x_scr[b, 5:8, :] = x_scr[b, pl.ds(8 + tb - 3, 3), :]
        sc = conv * _sigmoid(conv)
        q_all = sc[:, :D_B]
        k_all = sc[:, D_B:] * (DK_B ** -0.5)
        v_all = pm_ref[b, :, 2 * D_B:3 * D_B]
        o_all = pm_ref[b, :, 3 * D_B:4 * D_B]
        gp = pg_ref[b, :, 0:128] + gb_ref[...]
        gates = jnp.where(lane_g < H_B, gp, jnp.minimum(gp, 0.0) - jnp.log1p(jnp.exp(-jnp.abs(gp))))
        csum_col = jnp.dot(tri_blk, gates, precision=hi, preferred_element_type=F32)
        gt = gates.T
        csum_row = lax.dot_general(gt, tri_blk, nt_dims, precision=hi, preferred_element_type=F32)
        for c in range(nch):
            rs = slice(c * lc, (c + 1) * lc)
            for h in range(H_B):
                u = (b, c, h)
                hs = slice(h * DK_B, (h + 1) * DK_B)
                q_u[u], k_u[u], v_u[u], o_u[u] = q_all[rs, hs], k_all[rs, hs], v_all[rs, hs], o_all[rs, hs]
                li_col[u] = gates[rs, h:h + 1]
                li_row[u] = gt[h:h + 1, rs]
                b_col[u] = csum_col[rs, H_B + h:H_B + h + 1]
                b_row[u] = csum_row[H_B + h:H_B + h + 1, rs]
                b_l[u] = csum_col[(c + 1) * lc - 1:(c + 1) * lc, H_B + h:H_B + h + 1]

    rep = lambda col: jnp.broadcast_to(col, (lc, DK_B))
    bc = {u: rep(b_col[u]) for u in units}
    lic = {u: rep(li_col[u]) for u in units}

    last_max = {u: jnp.max(b_l[u] - b_row[u] + li_row[u], axis=-1, keepdims=True) for u in units}
    m_prev, m_new = {}, {}
    for b in range(nb):
        for h in range(H_B):
            m_p = m_scr[b, h][:, 0:1]
            for c in range(nch):
                u = (b, c, h)
                m_prev[u] = m_p
                m_p = jnp.maximum(b_l[u] + m_p, last_max[u])
                m_new[u] = m_p
            m_scr[b, h] = jnp.broadcast_to(m_p, (1, 128))

    dmat = {u: jnp.where(causal, bc[u][:, :lc] - b_row[u] + li_row[u], -jnp.inf) for u in units}
    row_max = {u: rep(jnp.max(dmat[u], axis=-1, keepdims=True)) for u in units}
    m_t, g_inter, e_mat = {}, {}, {}
    for u in units:
        inter = bc[u] + m_prev[u]
        m_t[u] = jnp.maximum(inter, row_max[u])
        g_inter[u] = jnp.exp(inter - m_t[u])
        e_mat[u] = jnp.exp(dmat[u] - m_t[u][:, :lc])

    qk = {u: lax.dot_general(q_u[u].astype(BF16), k_u[u].astype(BF16), nt_dims, preferred_element_type=F32)
          for u in units}
    s_mat = {u: qk[u] * e_mat[u] for u in units}
    sv = {u: _bdot(s_mat[u], v_u[u]) for u in units}
    s_sum = {u: rep(jnp.sum(s_mat[u], axis=-1, keepdims=True)) for u in units}
    kw = {u: k_u[u] * jnp.exp(b_l[u] - bc[u] + lic[u] - m_new[u]) for u in units}
    kwv = {u: lax.dot_general(kw[u].astype(BF16), v_u[u].astype(BF16), tn_dims, preferred_element_type=F32)
           for u in units}

    q_c, qn_prod = {}, {}
    c_cur = {(b, h): c_scr[b, h] for b in range(nb) for h in range(H_B)}
    n_cur = {(b, h): n_scr[b, h] for b in range(nb) for h in range(H_B)}
    for c in range(nch):
        for b in range(nb):
            for h in range(H_B):
                u = (b, c, h)
                q_c[u] = _bdot(q_u[u], c_cur[(b, h)])
                qn_prod[u] = q_u[u] * n_cur[(b, h)]
                dec = jnp.exp(b_l[u] + m_prev[u] - m_new[u])
                c_cur[(b, h)] = dec * c_cur[(b, h)] + kwv[u]
                n_cur[(b, h)] = dec * n_cur[(b, h)] + jnp.sum(kw[u], axis=0, keepdims=True)
    for b in range(nb):
        for h in range(H_B):
            c_scr[b, h] = c_cur[(b, h)]
            n_scr[b, h] = n_cur[(b, h)]
    q_n = {u: rep(jnp.sum(qn_prod[u], axis=-1, keepdims=True)) for u in units}

    hh = {}
    for u in units:
        num = g_inter[u] * q_c[u] + sv[u]
        den = g_inter[u] * q_n[u] + s_sum[u]
        hh[u] = num / jnp.maximum(jnp.abs(den), jnp.exp(-m_t[u]))
    mu = {u: rep(jnp.sum(hh[u], axis=-1, keepdims=True)) * (1.0 / DK_B) for u in units}
    dev = {u: hh[u] - mu[u] for u in units}
    var = {u: rep(jnp.sum(dev[u] * dev[u], axis=-1, keepdims=True)) * (1.0 / DK_B) for u in units}
    for u in units:
        b, c, h = u
        hs = slice(h * DK_B, (h + 1) * DK_B)
        hn = dev[u] * lax.rsqrt(var[u] + MH_EPS) * mhg_ref[:, hs]
        yb_ref[b, c * lc:(c + 1) * lc, hs] = _sigmoid(o_u[u]) * hn

    @pl.when(j == pl.num_programs(1) - 1)
    def _():
        ct_ref[...] = c_scr[...]
        nt_ref[...] = n_scr[...]
        mt_ref[...] = m_scr[:, :, :, 0:1]
        co_ref[...] = x_scr[:, 5:8, :]


def _mlstm(p3, conv_prev, c_all, l, c_out_prev, n0, m0, wts, nb, tb, lc):
    bsz, t, _ = p3.shape
    full = lambda shape: pl.BlockSpec(shape, lambda i, j: (0,) * len(shape))
    n_alias = 0 if c_out_prev is None else 1
    kern = functools.partial(_mlstm_kernel, nb=nb, tb=tb, lc=lc, n_alias=n_alias)
    gate_blk = RWKV_COLS // GATE_COLS
    c_spec = pl.BlockSpec((None, nb, H_B, DK_B, DK_B), lambda i, j: (l, i, 0, 0, 0))
    in_specs = [
        pl.BlockSpec((nb, tb, ML_COLS), lambda i, j: (i, j, 1)),
        pl.BlockSpec((nb, tb, GATE_COLS), lambda i, j: (i, j, gate_blk)),
        pl.BlockSpec((nb, CONV_W - 1, 2 * D_B), lambda i, j: (i, 0, 0)),
        c_spec,
        pl.BlockSpec((nb, H_B, 1, DK_B), lambda i, j: (i, 0, 0, 0)),
        pl.BlockSpec((nb, H_B, 1, 1), lambda i, j: (i, 0, 0, 0)),
        full((CONV_W, 2 * D_B)), full((1, 2 * D_B)), full((1, 128)), full((1, D_B)),
    ]
    args = [p3, p3, conv_prev, c_all, n0, m0, *wts]
    aliases = {}
    if n_alias:
        in_specs.append(pl.BlockSpec(memory_space=pl.ANY))
        args.append(c_out_prev)
        aliases = {len(args) - 1: 2}
    return pl.pallas_call(
        kern,
        grid=(bsz // nb, t // tb),
        in_specs=in_specs,
        out_specs=[
            pl.BlockSpec((nb, tb, D_B), lambda i, j: (i, j, 0)),
            pl.BlockSpec((nb, CONV_W - 1, 2 * D_B), lambda i, j: (i, 0, 0)),
            c_spec,
            pl.BlockSpec((nb, H_B, 1, DK_B), lambda i, j: (i, 0, 0, 0)),
            pl.BlockSpec((nb, H_B, 1, 1), lambda i, j: (i, 0, 0, 0)),
        ],
        out_shape=[
            jax.ShapeDtypeStruct((bsz, t, D_B), F32),
            jax.ShapeDtypeStruct((bsz, CONV_W - 1, 2 * D_B), F32),
            jax.ShapeDtypeStruct((DEPTH, bsz, H_B, DK_B, DK_B), F32),
            jax.ShapeDtypeStruct((bsz, H_B, 1, DK_B), F32),
            jax.ShapeDtypeStruct((bsz, H_B, 1, 1), F32),
        ],
        scratch_shapes=[
            pltpu.VMEM((nb, tb + 8, 2 * D_B), F32),
            pltpu.VMEM((nb, H_B, DK_B, DK_B), F32),
            pltpu.VMEM((nb, H_B, 1, DK_B), F32),
            pltpu.VMEM((nb, H_B, 1, 128), F32),
        ],
        input_output_aliases=aliases,
        compiler_params=pltpu.CompilerParams(
            dimension_semantics=("parallel", "arbitrary"), vmem_limit_bytes=VMEM_LIMIT),
        name="mlstm",
    )(*args)


def _layer_weights(l, w):
    bf = lambda a: a.astype(BF16)
    row = lambda a: a.reshape(1, -1)
    w_in = w['w_in'][l]
    w_gates = jnp.pad(w_in[:, RWKV_COLS + ML_COLS:], ((0, 0), (0, GATE_COLS - 2 * H_B)))
    w_cat = jnp.concatenate([w_in[:, :RWKV_COLS], w_gates, w_in[:, RWKV_COLS:RWKV_COLS + ML_COLS]], axis=1)
    zero = jnp.zeros((R_W, D_A), F32)
    w2a = jnp.concatenate([jnp.concatenate([w['w2'][l], zero], axis=1),
                           jnp.concatenate([zero, w['a2'][l]], axis=1)], axis=0)
    ids = jnp.arange(D_A) // N_A
    seg = (ids[:, None] == ids[None, :]).astype(BF16)
    gate_bias = jnp.pad(jnp.concatenate([w['i_bias'][l], w['f_bias'][l]]), (0, 128 - 2 * H_B)).reshape(1, 128)
    return dict(
        ln1=(row(w['ln1_g'][l]), row(w['ln1_b'][l])),
        ln3=(row(w['ln3_g'][l]), row(w['ln3_b'][l])),
        w_cat=bf(w_cat),
        rwkv=(row(w['mu_shift'][l]), row(w['w0'][l]), bf(w2a), row(w['a0'][l]), bf(w['g2'][l]), row(w['k_k'][l]),
              row(w['k_a'][l]), row(w['r_k'][l]), row(w['gn_g'][l]), row(w['gn_b'][l]), seg),
        mlstm=(w['conv_w'][l], row(w['conv_b'][l]), gate_bias, row(w['mh_g'][l])),
        out=(bf(w['w_out'][l][:D_A]), bf(w['w_out'][l][D_A:]), row(w['ln2_g'][l]), row(w['ln2_b'][l])),
    )


def _trunk(x, states, lw, ffn_w, nb_r, tb_r, nb_m, tb_m):
    bsz, t, _ = x.shape
    n = bsz * t
    tm = min(1024, n // 2)
    lc = math.gcd(t, CHUNK)
    xf = x.reshape(n, D_MODEL)
    st_shift, st_wkv, st_conv, st_c, st_n, st_m = states
    new = [[] for _ in range(4)]
    wkv_out, c_out = None, None
    for l in range(DEPTH):
        wl = lw[l]
        x1 = _ffn_ln(xf, *ffn_w[0], *wl['ln1'], l=l, tm=tm)
        p3 = _in_proj(x1, wl['w_cat'], tm=tm).reshape(bsz, t, P_COLS)
        ya, shift, wkv_out = _rwkv(p3, st_shift[l].reshape(bsz, 1, RWKV_COLS), st_wkv, l, wkv_out,
                                   wl['rwkv'], nb_r, tb_r)
        yb, conv, c_out, n_t, m_t = _mlstm(p3, st_conv[l], st_c, l, c_out,
                                           st_n[l].reshape(bsz, H_B, 1, DK_B), st_m[l].reshape(bsz, H_B, 1, 1),
                                           wl['mlstm'], nb_m, tb_m, lc)
        x2 = _out_proj_ln(x1, ya.reshape(n, D_A), yb.reshape(n, D_B), *wl['out'], tm=tm)
        xf = _ffn_ln(x2, *ffn_w[1], *wl['ln3'], l=l, tm=tm)
        for idx, s in enumerate((shift.reshape(bsz, RWKV_COLS), conv, n_t.reshape(bsz, H_B, DK_B),
                                 m_t.reshape(bsz, H_B))):
            new[idx].append(s)
    shift_o, conv_o, n_o, m_o = [jnp.stack(s) for s in new]
    return xf.reshape(bsz, t, D_MODEL), [shift_o, wkv_out, conv_o, c_out, n_o, m_o]


def kernel(x_prompt, x_sample, state_shift, state_wkv, state_conv, state_C, state_n, state_m,
           ffn1_wg, ffn1_wu, ffn1_wd, ln1_g, ln1_b, w_in, mu_shift, w0, w2, a0, a2, g2, k_k, k_a, r_k,
           gn_g, gn_b, conv_w, conv_b, i_bias, f_bias, mh_g, w_out, ln2_g, ln2_b,
           ffn2_wg, ffn2_wu, ffn2_wd, ln3_g, ln3_b):
    w = dict(ffn1_wg=ffn1_wg, ffn1_wu=ffn1_wu, ffn1_wd=ffn1_wd, ln1_g=ln1_g, ln1_b=ln1_b, w_in=w_in,
             mu_shift=mu_shift, w0=w0, w2=w2, a0=a0, a2=a2, g2=g2, k_k=k_k, k_a=k_a, r_k=r_k,
             gn_g=gn_g, gn_b=gn_b, conv_w=conv_w, conv_b=conv_b, i_bias=i_bias, f_bias=f_bias,
             mh_g=mh_g, w_out=w_out, ln2_g=ln2_g, ln2_b=ln2_b, ffn2_wg=ffn2_wg, ffn2_wu=ffn2_wu,
             ffn2_wd=ffn2_wd, ln3_g=ln3_g, ln3_b=ln3_b)
    lw = [_layer_weights(l, w) for l in range(DEPTH)]
    ffn_w = ((ffn1_wg.astype(BF16), ffn1_wu.astype(BF16), ffn1_wd.astype(BF16)),
             (ffn2_wg.astype(BF16), ffn2_wu.astype(BF16), ffn2_wd.astype(BF16)))
    bp = x_prompt.shape[0]
    dt = x_prompt.dtype
    init = [jnp.zeros((DEPTH, bp, RWKV_COLS), dt),
            jnp.zeros((DEPTH, bp, H_A, N_A, N_A), dt),
            jnp.zeros((DEPTH, bp, CONV_W - 1, 2 * D_B), dt),
            jnp.zeros((DEPTH, bp, H_B, DK_B, DK_B), dt),
            jnp.zeros((DEPTH, bp, H_B, DK_B), dt),
            jnp.zeros((DEPTH, bp, H_B), dt)]
    y_p, ps = _trunk(x_prompt, init, lw, ffn_w, nb_r=8, tb_r=64, nb_m=1, tb_m=256)
    y_s, ss = _trunk(x_sample, [state_shift, state_wkv, state_conv, state_C, state_n, state_m], lw, ffn_w,
                     nb_r=8, tb_r=8, nb_m=8, tb_m=8)
    return (y_p, y_s, *ps, *ss)
```

```python
import functools
import math

import jax
import jax.numpy as jnp
from jax import lax
from jax.experimental import pallas as pl
from jax.experimental.pallas import tpu as pltpu

D_MODEL = 1024
DEPTH = 2
D_A = 512
N_A = 64
H_A = 8
D_B = 512
H_B = 4
DK_B = 128
R_W = 64
R_A = 64
R_G = 128
RWKV_COLS = 3 * D_A + R_W + R_A + R_G
CONV_W = 4
CHUNK = 64
D_FF = 2816
ALPHA = (2.0 * DEPTH) ** 0.25
LN_EPS = 1e-5
GN_EPS = 64e-5
MH_EPS = 1e-6

GATE_COLS = 256
ML_COLS = 4 * D_B
P_COLS = RWKV_COLS + GATE_COLS + ML_COLS
VMEM_LIMIT = 56 * 1024 * 1024

F32 = jnp.float32
BF16 = jnp.bfloat16


def _bdot(a, b):
    return jnp.dot(a.astype(BF16), b.astype(BF16), preferred_element_type=F32)


def _dot_exact_rhs(x, m_bf16):
    hi = x.astype(BF16)
    lo = (x - hi.astype(F32)).astype(BF16)
    return (jnp.dot(hi, m_bf16, preferred_element_type=F32)
            + jnp.dot(lo, m_bf16, preferred_element_type=F32))


def _layer_norm_rows(y, g, b):
    mu = jnp.mean(y, axis=-1, keepdims=True)
    d = y - mu
    var = jnp.mean(d * d, axis=-1, keepdims=True)
    return d * lax.rsqrt(var + LN_EPS) * g + b


def _softplus(x):
    return jnp.maximum(x, 0.0) + jnp.log1p(jnp.exp(-jnp.abs(x)))


def _sigmoid(x):
    return 1.0 / (1.0 + jnp.exp(-x))


def _ffn_ln_kernel(x_ref, wg_ref, wu_ref, wd_ref, g_ref, b_ref, o_ref, xb_scr, acc_scr):
    j = pl.program_id(1)

    @pl.when(j == 0)
    def _():
        xb_scr[...] = x_ref[...].astype(BF16)
        acc_scr[...] = jnp.zeros_like(acc_scr)

    xb = xb_scr[...]
    hg = jnp.dot(xb, wg_ref[...], preferred_element_type=F32)
    hu = jnp.dot(xb, wu_ref[...], preferred_element_type=F32)
    h = (hg * _sigmoid(hg)) * hu
    acc_scr[...] += jnp.dot(h.astype(BF16), wd_ref[...], preferred_element_type=F32)

    @pl.when(j == pl.num_programs(1) - 1)
    def _():
        y = ALPHA * x_ref[...] + 0.5 * acc_scr[...]
        o_ref[...] = _layer_norm_rows(y, g_ref[...], b_ref[...])


def _ffn_ln(x, wg, wu, wd, g, b, l, tm, tf=256):
    n = x.shape[0]
    return pl.pallas_call(
        _ffn_ln_kernel,
        grid=(n // tm, D_FF // tf),
        in_specs=[
            pl.BlockSpec((tm, D_MODEL), lambda i, j: (i, 0)),
            pl.BlockSpec((None, D_MODEL, tf), lambda i, j: (l, 0, j)),
            pl.BlockSpec((None, D_MODEL, tf), lambda i, j: (l, 0, j)),
            pl.BlockSpec((None, tf, D_MODEL), lambda i, j: (l, j, 0)),
            pl.BlockSpec((1, D_MODEL), lambda i, j: (0, 0)),
            pl.BlockSpec((1, D_MODEL), lambda i, j: (0, 0)),
        ],
        out_specs=pl.BlockSpec((tm, D_MODEL), lambda i, j: (i, 0)),
        out_shape=jax.ShapeDtypeStruct((n, D_MODEL), F32),
        scratch_shapes=[pltpu.VMEM((tm, D_MODEL), BF16), pltpu.VMEM((tm, D_MODEL), F32)],
        compiler_params=pltpu.CompilerParams(
            dimension_semantics=("parallel", "arbitrary"), vmem_limit_bytes=VMEM_LIMIT),
        name="ffn_ln",
    )(x, wg, wu, wd, g, b)


def _in_proj_kernel(x_ref, w_ref, o_ref, xb_scr):
    @pl.when(pl.program_id(1) == 0)
    def _():
        xb_scr[...] = x_ref[...].astype(BF16)

    o_ref[...] = jnp.dot(xb_scr[...], w_ref[...], preferred_element_type=F32)


def _in_proj(x, w, tm, tn=512):
    n = x.shape[0]
    return pl.pallas_call(
        _in_proj_kernel,
        grid=(n // tm, P_COLS // tn),
        in_specs=[
            pl.BlockSpec((tm, D_MODEL), lambda i, j: (i, 0)),
            pl.BlockSpec((D_MODEL, tn), lambda i, j: (0, j)),
        ],
        out_specs=pl.BlockSpec((tm, tn), lambda i, j: (i, j)),
        out_shape=jax.ShapeDtypeStruct((n, P_COLS), F32),
        scratch_shapes=[pltpu.VMEM((tm, D_MODEL), BF16)],
        compiler_params=pltpu.CompilerParams(
            dimension_semantics=("parallel", "arbitrary"), vmem_limit_bytes=VMEM_LIMIT),
        name="in_proj",
    )(x, w)


def _out_proj_ln_kernel(x_ref, ya_ref, yb_ref, wa_ref, wb_ref, g_ref, b_ref, o_ref):
    mix = (jnp.dot(ya_ref[...].astype(BF16), wa_ref[...], preferred_element_type=F32)
           + jnp.dot(yb_ref[...].astype(BF16), wb_ref[...], preferred_element_type=F32))
    o_ref[...] = _layer_norm_rows(ALPHA * x_ref[...] + mix, g_ref[...], b_ref[...])


def _out_proj_ln(x, ya, yb, wa, wb, g, b, tm):
    n = x.shape[0]
    return pl.pallas_call(
        _out_proj_ln_kernel,
        grid=(n // tm,),
        in_specs=[
            pl.BlockSpec((tm, D_MODEL), lambda i: (i, 0)),
            pl.BlockSpec((tm, D_A), lambda i: (i, 0)),
            pl.BlockSpec((tm, D_B), lambda i: (i, 0)),
            pl.BlockSpec((D_A, D_MODEL), lambda i: (0, 0)),
            pl.BlockSpec((D_B, D_MODEL), lambda i: (0, 0)),
            pl.BlockSpec((1, D_MODEL), lambda i: (0, 0)),
            pl.BlockSpec((1, D_MODEL), lambda i: (0, 0)),
        ],
        out_specs=pl.BlockSpec((tm, D_MODEL), lambda i: (i, 0)),
        out_shape=jax.ShapeDtypeStruct((n, D_MODEL), F32),
        compiler_params=pltpu.CompilerParams(
            dimension_semantics=("parallel",), vmem_limit_bytes=VMEM_LIMIT),
        name="out_proj_ln",
    )(x, ya, yb, wa, wb, g, b)


def _rwkv_kernel(*refs, nb, tb, n_alias):
    (p_ref, sp_ref, s0_ref, mu_ref, w0_ref, w2a_ref, a0_ref, g2_ref, kk_ref, ka_ref, rk_ref,
     gng_ref, gnb_ref, seg_ref) = refs[:14]
    (ya_ref, so_ref, st_ref,
     s_scr, carry_scr, nk_scr, p2_scr, q1_scr, rp_scr, v_scr, vsw_scr, ya1_scr, b_scr, yc1_scr, k_scr, w12_scr,
     cbr_scr, ckr_scr, bonus_scr, g_scr, sa_scr, q_scr, yo_scr) = refs[14 + n_alias:]
    j = pl.program_id(1)
    n_hp = H_A // 2

    @pl.when(j == 0)
    def _():
        for b in range(nb):
            for hp in range(n_hp):
                s_scr[b, hp] = jnp.concatenate([s0_ref[b, 2 * hp], s0_ref[b, 2 * hp + 1]], axis=-1)
        carry_scr[...] = sp_ref[...]
        sa_scr[...] = jnp.zeros_like(sa_scr)
        q_scr[...] = jnp.zeros_like(q_scr)
        yo_scr[...] = jnp.zeros_like(yo_scr)

    seg = seg_ref[...]
    rows = nb * tb
    lane128 = lax.broadcasted_iota(jnp.int32, (rows, 128), 1)
    rowid = lax.broadcasted_iota(jnp.int32, (tb, RWKV_COLS), 0)
    rowid_a = lax.broadcasted_iota(jnp.int32, (tb, D_A), 0)

    pms, rps, r_last = [], [], []
    for b in range(nb):
        p = p_ref[b]
        prev = jnp.where(rowid == 0, carry_scr[b], pltpu.roll(p, 1, axis=0))
        carry_scr[b] = p[tb - 1:tb, :]
        pm_b = p + (prev - p) * mu_ref[...]
        r_b = pm_b[:, 0:D_A]
        pms.append(pm_b)
        rps.append(jnp.where(rowid_a == 0, 0.0, pltpu.roll(r_b, 1, axis=0)))
        r_last.append(r_b[tb - 1:tb, :])
    pm = jnp.concatenate(pms, axis=0)
    r = pm[:, 0:D_A]
    k = pm[:, D_A:2 * D_A]
    v = pm[:, 2 * D_A:3 * D_A]
    z = pm[:, 3 * D_A:3 * D_A + 128]
    xg = pm[:, 3 * D_A + 128:RWKV_COLS]
    zt = jnp.where(lane128 < R_W, jnp.tanh(z), z)
    lr = _bdot(zt, w2a_ref[...])
    g_scr[...] = _bdot(_sigmoid(xg), g2_ref[...]).reshape(nb, tb, D_A)
    kk = k * kk_ref[...]
    ss = _dot_exact_rhs(kk * kk, seg)
    w_raw = -_softplus(-(w0_ref[...] + lr[:, :D_A])) - 0.5
    w = jnp.exp(-jnp.exp(w_raw))
    a = _sigmoid(a0_ref[...] + lr[:, D_A:])
    kk = kk * lax.rsqrt(jnp.maximum(ss, 1e-24))
    kmod = k * (1.0 + (a - 1.0) * ka_ref[...])
    bonus_scr[...] = (_dot_exact_rhs(r * kmod * rk_ref[...], seg) * v).reshape(nb, tb, D_A)
    bb = kk * a
    nxt = lambda x: pltpu.roll(x, rows - 1, axis=0)
    kk_n, w_n, bb_n = nxt(kk), nxt(w), nxt(bb)
    cc = _bdot(jnp.concatenate([bb * kk_n, kmod * kk_n, bb * r, kmod * r], axis=0), seg)
    cbk, ckk = cc[0:rows], cc[rows:2 * rows]
    cbr_scr[...] = cc[2 * rows:3 * rows].reshape(nb, tb, D_A)
    ckr_scr[...] = cc[3 * rows:4 * rows].reshape(nb, tb, D_A)
    nk_scr[...] = (-kk).reshape(nb, tb, D_A)
    p2_scr[...] = (-(w * kk_n)).reshape(nb, tb, D_A)
    q1_scr[...] = (w * r).reshape(nb, tb, D_A)
    rp_scr[...] = jnp.concatenate(rps, axis=0).reshape(nb, tb, D_A)
    ya1_scr[...] = (bb * w_n - cbk * bb_n).reshape(nb, tb, D_A)
    b_scr[...] = bb.reshape(nb, tb, D_A)
    yc1_scr[...] = (kmod * w_n - ckk * bb_n).reshape(nb, tb, D_A)
    k_scr[...] = kmod.reshape(nb, tb, D_A)
    w12_scr[...] = (w * w_n).reshape(nb, tb, D_A)
    v_scr[...] = v.reshape(nb, tb, D_A)
    for hp in range(n_hp):
        vsw_scr[:, :, hp * 128:(hp + 1) * 128] = pltpu.roll(
            v[:, hp * 128:(hp + 1) * 128], 64, axis=1).reshape(nb, tb, 128)

    row8 = lax.broadcasted_iota(jnp.int32, (8, 128), 0)
    lane8 = lax.broadcasted_iota(jnp.int32, (8, 128), 1)
    mk = [(((row8 == 2 * q) & (lane8 < 64)) | ((row8 == 2 * q + 1) & (lane8 >= 64))).astype(F32) for q in range(4)]
    row64 = lax.broadcasted_iota(jnp.int32, (8, 64), 0)
    keep03 = (row64 < 4).astype(F32)
    e_row = [(row64 == q).astype(F32) for q in range(8)]
    chains = [(b, hp) for b in range(nb) for hp in range(n_hp)]
    nt_dims = (((1,), (1,)), ((), ()))
    tn_dims = (((0,), (0,)), ((), ()))
    grp = min(16, tb)
    n_pair = grp // 2

    def natural_rows(rows_h0, rows_h1):
        pad = [jnp.zeros((8 - n_pair, N_A), F32)] if n_pair < 8 else []
        return jnp.concatenate([jnp.concatenate(rows_h0 + pad, axis=0),
                                jnp.concatenate(rows_h1 + pad, axis=0)], axis=1)

    def group(tg, carry):
        ts = pl.ds(pl.multiple_of(tg * grp, grp), grp)
        tiles = {}
        for (b, hp) in chains:
            sl = pl.ds(hp * 128, 128)
            tiles[(b, hp)] = tuple(ref[b, ts, sl] for ref in
                                   (nk_scr, p2_scr, q1_scr, rp_scr, v_scr, vsw_scr,
                                    ya1_scr, b_scr, yc1_scr, k_scr, w12_scr))
        out = {c: [[] for _ in range(6)] for c in chains}
        for pi in range(n_pair):
            i = 2 * pi
            reds = {}
            for c in chains:
                nk_t, p2_t, q1_t, rp_t = tiles[c][0:4]
                a_mat = (nk_t[i:i + 1] * mk[0] + p2_t[i:i + 1] * mk[1]
                         + q1_t[i:i + 1] * mk[2] + rp_t[i:i + 1] * mk[3])
                reds[c] = lax.dot_general(a_mat.astype(BF16), s_scr[c[0], c[1]].astype(BF16), nt_dims,
                                          preferred_element_type=F32)
            xs = {}
            for c in chains:
                v_t, vs_t = tiles[c][4], tiles[c][5]
                x_mat = (reds[c] * keep03 + v_t[i:i + 1, :64] * e_row[4] + vs_t[i:i + 1, :64] * e_row[5]
                         + v_t[i + 1:i + 2, :64] * e_row[6] + vs_t[i + 1:i + 2, :64] * e_row[7])
                xs[c] = x_mat.astype(BF16)
            for c in chains:
                ya_t, b_t, yc_t, k_t, w12_t = tiles[c][6:11]
                y_mat = (ya_t[i:i + 1] * mk[0] + b_t[i + 1:i + 2] * mk[1]
                         + yc_t[i:i + 1] * mk[2] + k_t[i + 1:i + 2] * mk[3])
                d_s = lax.dot_general(xs[c], y_mat.astype(BF16), tn_dims, preferred_element_type=F32)
                s_scr[c[0], c[1]] = s_scr[c[0], c[1]] * w12_t[i:i + 1] + d_s
                for slot, row in enumerate((0, 1, 4, 5, 6, 7)):
                    out[c][slot].append(reds[c][row:row + 1, :])
        even0 = pl.multiple_of(tg * grp, 8)
        for c in chains:
            o = out[c]
            sa_scr[c[0], c[1], pl.ds(even0, 8, stride=2), :] = natural_rows(o[0], o[1])
            q_scr[c[0], c[1], pl.ds(even0, 8, stride=2), :] = natural_rows(o[2], o[3])
            yo_scr[c[0], c[1], pl.ds(even0 + 8, 8, stride=2), :] = natural_rows(o[4], o[5])
        return carry

    lax.fori_loop(0, tb // grp, group, 0)

    for (b, hp) in chains:
        rl = r_last[b][:, hp * 128:(hp + 1) * 128]
        a_mat = rl * mk[3]
        red = lax.dot_general(a_mat.astype(BF16), s_scr[b, hp].astype(BF16), nt_dims, preferred_element_type=F32)
        y_last = jnp.concatenate([red[6:7, :], red[7:8, :]], axis=1)
        yo_scr[b, hp, pl.ds(tb + 8, 8), :] = jnp.broadcast_to(y_last, (8, 128))

    def slab(scr, first):
        return jnp.concatenate(
            [jnp.concatenate([scr[b, hp, pl.ds(first, tb), :] for hp in range(n_hp)], axis=1) for b in range(nb)],
            axis=0)

    y_even = (slab(q_scr, 0) + slab(sa_scr, 0) * cbr_scr[...].reshape(rows, D_A)
              + v_scr[...].reshape(rows, D_A) * ckr_scr[...].reshape(rows, D_A))
    parity = lax.broadcasted_iota(jnp.int32, (rows, D_A), 0) & 1
    y = jnp.where(parity == 0, y_even, slab(yo_scr, 9))
    mu = _dot_exact_rhs(y, seg) * (1.0 / N_A)
    d = y - mu
    var = _dot_exact_rhs(d * d, seg) * (1.0 / N_A)
    yn = d * lax.rsqrt(var + GN_EPS) * gng_ref[...] + gnb_ref[...]
    ya_ref[...] = (yn.reshape(nb, tb, D_A) + bonus_scr[...]) * g_scr[...]

    @pl.when(j == pl.num_programs(1) - 1)
    def _():
        for b in range(nb):
            for hp in range(n_hp):
                s_pair = s_scr[b, hp]
                st_ref[b, 2 * hp] = s_pair[:, :N_A]
                st_ref[b, 2 * hp + 1] = s_pair[:, N_A:]
        so_ref[...] = carry_scr[...]


def _rwkv(p3, shift_prev, wkv_all, l, wkv_out_prev, wts, nb, tb):
    bsz, t, _ = p3.shape
    n_hp = H_A // 2
    full = lambda shape: pl.BlockSpec(shape, lambda i, j: (0,) * len(shape))
    blk = lambda: pltpu.VMEM((nb, tb, D_A), F32)
    n_alias = 0 if wkv_out_prev is None else 1
    kern = functools.partial(_rwkv_kernel, nb=nb, tb=tb, n_alias=n_alias)
    state_spec = pl.BlockSpec((None, nb, H_A, N_A, N_A), lambda i, j: (l, i, 0, 0, 0))
    in_specs = [
        pl.BlockSpec((nb, tb, RWKV_COLS), lambda i, j: (i, j, 0)),
        pl.BlockSpec((nb, 1, RWKV_COLS), lambda i, j: (i, 0, 0)),
        state_spec,
        full((1, RWKV_COLS)), full((1, D_A)), full((128, 2 * D_A)), full((1, D_A)), full((R_G, D_A)),
        full((1, D_A)), full((1, D_A)), full((1, D_A)), full((1, D_A)), full((1, D_A)), full((D_A, D_A)),
    ]
    args = [p3, shift_prev, wkv_all, *wts]
    aliases = {}
    if n_alias:
        in_specs.append(pl.BlockSpec(memory_space=pl.ANY))
        args.append(wkv_out_prev)
        aliases = {len(args) - 1: 2}
    return pl.pallas_call(
        kern,
        grid=(bsz // nb, t // tb),
        in_specs=in_specs,
        out_specs=[
            pl.BlockSpec((nb, tb, D_A), lambda i, j: (i, j, 0)),
            pl.BlockSpec((nb, 1, RWKV_COLS), lambda i, j: (i, 0, 0)),
            state_spec,
        ],
        out_shape=[
            jax.ShapeDtypeStruct((bsz, t, D_A), F32),
            jax.ShapeDtypeStruct((bsz, 1, RWKV_COLS), F32),
            jax.ShapeDtypeStruct((DEPTH, bsz, H_A, N_A, N_A), F32),
        ],
        scratch_shapes=[
            pltpu.VMEM((nb, n_hp, N_A, 128), F32),
            pltpu.VMEM((nb, 1, RWKV_COLS), F32),
            *[blk() for _ in range(15)],
            *[pltpu.VMEM((nb, n_hp, tb + 24, 2 * N_A), F32) for _ in range(3)],
        ],
        input_output_aliases=aliases,
        compiler_params=pltpu.CompilerParams(
            dimension_semantics=("parallel", "arbitrary"), vmem_limit_bytes=VMEM_LIMIT),
        name="rwkv7",
    )(*args)


def _mlstm_kernel(*refs, nb, tb, lc, n_alias):
    (pm_ref, pg_ref, cp_ref, c0_ref, n0_ref, m0_ref, cw_ref, cb_ref, gb_ref, mhg_ref) = refs[:10]
    (yb_ref, co_ref, ct_ref, nt_ref, mt_ref, x_scr, c_scr, n_scr, m_scr) = refs[10 + n_alias:]
    j = pl.program_id(1)
    hi = lax.Precision.HIGHEST
    nch = tb // lc
    lc_shift = lc.bit_length() - 1

    @pl.when(j == 0)
    def _():
        c_scr[...] = c0_ref[...]
        n_scr[...] = n0_ref[...]
        m_scr[...] = jnp.broadcast_to(m0_ref[...], m_scr.shape)
        x_scr[:, 5:8, :] = cp_ref[...]

    rr = lax.broadcasted_iota(jnp.int32, (lc, lc), 0)
    cc = lax.broadcasted_iota(jnp.int32, (lc, lc), 1)
    causal = rr >= cc
    rb = lax.broadcasted_iota(jnp.int32, (tb, tb), 0)
    cb = lax.broadcasted_iota(jnp.int32, (tb, tb), 1)
    tri_blk = ((rb >= cb) & (lax.shift_right_logical(rb, lc_shift) == lax.shift_right_logical(cb, lc_shift))
               ).astype(F32)
    lane_g = lax.broadcasted_iota(jnp.int32, (tb, 128), 1)
    tn_dims = (((0,), (0,)), ((), ()))
    nt_dims = (((1,), (1,)), ((), ()))

    units = [(b, c, h) for b in range(nb) for c in range(nch) for h in range(H_B)]
    q_u, k_u, v_u, o_u = {}, {}, {}, {}
    li_col, li_row, b_col, b_row, b_l = {}, {}, {}, {}, {}

    for b in range(nb):
        x = pm_ref[b, :, 0:2 * D_B]
        x_scr[b, pl.ds(8, tb), :] = x
        conv = cb_ref[...] + x * cw_ref[3:4, :]
        for s in range(1, CONV_W):
            conv = conv + x_scr[b, pl.ds(8 - s, tb), :] * cw_ref[3 - s:4 - s, :]
        x_scr[b, 5:8, :] = x_scr[b, pl.ds(8 + tb - 3, 3), :]
        sc = conv * _sigmoid(conv)
        q_all = sc[:, :D_B]
        k_all = sc[:, D_B:] * (DK_B ** -0.5)
        v_all = pm_ref[b, :, 2 * D_B:3 * D_B]
        o_all = pm_ref[b, :, 3 * D_B:4 * D_B]
        gp = pg_ref[b, :, 0:128] + gb_ref[...]
        gates = jnp.where(lane_g < H_B, gp, jnp.minimum(gp, 0.0) - jnp.log1p(jnp.exp(-jnp.abs(gp))))
        csum_col = jnp.dot(tri_blk, gates, precision=hi, preferred_element_type=F32)
        gt = gates.T
        csum_row = lax.dot_general(gt, tri_blk, nt_dims, precision=hi, preferred_element_type=F32)
        for c in range(nch):
            rs = slice(c * lc, (c + 1) * lc)
            for h in range(H_B):
                u = (b, c, h)
                hs = slice(h * DK_B, (h + 1) * DK_B)
                q_u[u], k_u[u], v_u[u], o_u[u] = q_all[rs, hs], k_all[rs, hs], v_all[rs, hs], o_all[rs, hs]
                li_col[u] = gates[rs, h:h + 1]
                li_row[u] = gt[h:h + 1, rs]
                b_col[u] = csum_col[rs, H_B + h:H_B + h + 1]
                b_row[u] = csum_row[H_B + h:H_B + h + 1, rs]
                b_l[u] = csum_col[(c + 1) * lc - 1:(c + 1) * lc, H_B + h:H_B + h + 1]

    rep = lambda col: jnp.broadcast_to(col, (lc, DK_B))
    bc = {u: rep(b_col[u]) for u in units}
    lic = {u: rep(li_col[u]) for u in units}

    last_max = {u: jnp.max(b_l[u] - b_row[u] + li_row[u], axis=-1, keepdims=True) for u in units}
    m_prev, m_new = {}, {}
    for b in range(nb):
        for h in range(H_B):
            m_p = m_scr[b, h][:, 0:1]
            for c in range(nch):
                u = (b, c, h)
                m_prev[u] = m_p
                m_p = jnp.maximum(b_l[u] + m_p, last_max[u])
                m_new[u] = m_p
            m_scr[b, h] = jnp.broadcast_to(m_p, (1, 128))

    dmat = {u: jnp.where(causal, bc[u][:, :lc] - b_row[u] + li_row[u], -jnp.inf) for u in units}
    row_max = {u: rep(jnp.max(dmat[u], axis=-1, keepdims=True)) for u in units}
    m_t, g_inter, e_mat = {}, {}, {}
    for u in units:
        inter = bc[u] + m_prev[u]
        m_t[u] = jnp.maximum(inter, row_max[u])
        g_inter[u] = jnp.exp(inter - m_t[u])
        e_mat[u] = jnp.exp(dmat[u] - m_t[u][:, :lc])

    qk = {u: lax.dot_general(q_u[u].astype(BF16), k_u[u].astype(BF16), nt_dims, preferred_element_type=F32)
          for u in units}
    s_mat = {u: qk[u] * e_mat[u] for u in units}
    sv = {u: _bdot(s_mat[u], v_u[u]) for u in units}
    s_sum = {u: rep(jnp.sum(s_mat[u], axis=-1, keepdims=True)) for u in units}
    kw = {u: k_u[u] * jnp.exp(b_l[u] - bc[u] + lic[u] - m_new[u]) for u in units}
    kwv = {u: lax.dot_general(kw[u].astype(BF16), v_u[u].astype(BF16), tn_dims, preferred_element_type=F32)
           for u in units}

    q_c, qn_prod = {}, {}
    c_cur = {(b, h): c_scr[b, h] for b in range(nb) for h in range(H_B)}
    n_cur = {(b, h): n_scr[b, h] for b in range(nb) for h in range(H_B)}
    for c in range(nch):
        for b in range(nb):
            for h in range(H_B):
                u = (b, c, h)
                q_c[u] = _bdot(q_u[u], c_cur[(b, h)])
                qn_prod[u] = q_u[u] * n_cur[(b, h)]
                dec = jnp.exp(b_l[u] + m_prev[u] - m_new[u])
                c_cur[(b, h)] = dec * c_cur[(b, h)] + kwv[u]
                n_cur[(b, h)] = dec * n_cur[(b, h)] + jnp.sum(kw[u], axis=0, keepdims=True)
    for b in range(nb):
        for h in range(H_B):
            c_scr[b, h] = c_cur[(b, h)]
            n_scr[b, h] = n_cur[(b, h)]
    q_n = {u: rep(jnp.sum(qn_prod[u], axis=-1, keepdims=True)) for u in units}

    hh = {}
    for u in units:
        num = g_inter[u] * q_c[u] + sv[u]
        den = g_inter[u] * q_n[u] + s_sum[u]
        hh[u] = num / jnp.maximum(jnp.abs(den), jnp.exp(-m_t[u]))
    mu = {u: rep(jnp.sum(hh[u], axis=-1, keepdims=True)) * (1.0 / DK_B) for u in units}
    dev = {u: hh[u] - mu[u] for u in units}
    var = {u: rep(jnp.sum(dev[u] * dev[u], axis=-1, keepdims=True)) * (1.0 / DK_B) for u in units}
    for u in units:
        b, c, h = u
        hs = slice(h * DK_B, (h + 1) * DK_B)
        hn = dev[u] * lax.rsqrt(var[u] + MH_EPS) * mhg_ref[:, hs]
        yb_ref[b, c * lc:(c + 1) * lc, hs] = _sigmoid(o_u[u]) * hn

    @pl.when(j == pl.num_programs(1) - 1)
    def _():
        ct_ref[...] = c_scr[...]
        nt_ref[...] = n_scr[...]
        mt_ref[...] = m_scr[:, :, :, 0:1]
        co_ref[...] = x_scr[:, 5:8, :]


def _mlstm(p3, conv_prev, c_all, l, c_out_prev, n0, m0, wts, nb, tb, lc):
    bsz, t, _ = p3.shape
    full = lambda shape: pl.BlockSpec(shape, lambda i, j: (0,) * len(shape))
    n_alias = 0 if c_out_prev is None else 1
    kern = functools.partial(_mlstm_kernel, nb=nb, tb=tb, lc=lc, n_alias=n_alias)
    gate_blk = RWKV_COLS // GATE_COLS
    c_spec = pl.BlockSpec((None, nb, H_B, DK_B, DK_B), lambda i, j: (l, i, 0, 0, 0))
    in_specs = [
        pl.BlockSpec((nb, tb, ML_COLS), lambda i, j: (i, j, 1)),
        pl.BlockSpec((nb, tb, GATE_COLS), lambda i, j: (i, j, gate_blk)),
        pl.BlockSpec((nb, CONV_W - 1, 2 * D_B), lambda i, j: (i, 0, 0)),
        c_spec,
        pl.BlockSpec((nb, H_B, 1, DK_B), lambda i, j: (i, 0, 0, 0)),
        pl.BlockSpec((nb, H_B, 1, 1), lambda i, j: (i, 0, 0, 0)),
        full((CONV_W, 2 * D_B)), full((1, 2 * D_B)), full((1, 128)), full((1, D_B)),
    ]
    args = [p3, p3, conv_prev, c_all, n0, m0, *wts]
    aliases = {}
    if n_alias:
        in_specs.append(pl.BlockSpec(memory_space=pl.ANY))
        args.append(c_out_prev)
        aliases = {len(args) - 1: 2}
    return pl.pallas_call(
        kern,
        grid=(bsz // nb, t // tb),
        in_specs=in_specs,
        out_specs=[
            pl.BlockSpec((nb, tb, D_B), lambda i, j: (i, j, 0)),
            pl.BlockSpec((nb, CONV_W - 1, 2 * D_B), lambda i, j: (i, 0, 0)),
            c_spec,
            pl.BlockSpec((nb, H_B, 1, DK_B), lambda i, j: (i, 0, 0, 0)),
            pl.BlockSpec((nb, H_B, 1, 1), lambda i, j: (i, 0, 0, 0)),
        ],
        out_shape=[
            jax.ShapeDtypeStruct((bsz, t, D_B), F32),
            jax.ShapeDtypeStruct((bsz, CONV_W - 1, 2 * D_B), F32),
            jax.ShapeDtypeStruct((DEPTH, bsz, H_B, DK_B, DK_B), F32),
            jax.ShapeDtypeStruct((bsz, H_B, 1, DK_B), F32),
            jax.ShapeDtypeStruct((bsz, H_B, 1, 1), F32),
        ],
        scratch_shapes=[
            pltpu.VMEM((nb, tb + 8, 2 * D_B), F32),
            pltpu.VMEM((nb, H_B, DK_B, DK_B), F32),
            pltpu.VMEM((nb, H_B, 1, DK_B), F32),
            pltpu.VMEM((nb, H_B, 1, 128), F32),
        ],
        input_output_aliases=aliases,
        compiler_params=pltpu.CompilerParams(
            dimension_semantics=("parallel", "arbitrary"), vmem_limit_bytes=VMEM_LIMIT),
        name="mlstm",
    )(*args)


def _layer_weights(l, w):
    bf = lambda a: a.astype(BF16)
    row = lambda a: a.reshape(1, -1)
    w_in = w['w_in'][l]
    w_gates = jnp.pad(w_in[:, RWKV_COLS + ML_COLS:], ((0, 0), (0, GATE_COLS - 2 * H_B)))
    w_cat = jnp.concatenate([w_in[:, :RWKV_COLS], w_gates, w_in[:, RWKV_COLS:RWKV_COLS + ML_COLS]], axis=1)
    zero = jnp.zeros((R_W, D_A), F32)
    w2a = jnp.concatenate([jnp.concatenate([w['w2'][l], zero], axis=1),
                           jnp.concatenate([zero, w['a2'][l]], axis=1)], axis=0)
    ids = jnp.arange(D_A) // N_A
    seg = (ids[:, None] == ids[None, :]).astype(BF16)
    gate_bias = jnp.pad(jnp.concatenate([w['i_bias'][l], w['f_bias'][l]]), (0, 128 - 2 * H_B)).reshape(1, 128)
    return dict(
        ln1=(row(w['ln1_g'][l]), row(w['ln1_b'][l])),
        ln3=(row(w['ln3_g'][l]), row(w['ln3_b'][l])),
        w_cat=bf(w_cat),
        rwkv=(row(w['mu_shift'][l]), row(w['w0'][l]), bf(w2a), row(w['a0'][l]), bf(w['g2'][l]), row(w['k_k'][l]),
              row(w['k_a'][l]), row(w['r_k'][l]), row(w['gn_g'][l]), row(w['gn_b'][l]), seg),
        mlstm=(w['conv_w'][l], row(w['conv_b'][l]), gate_bias, row(w['mh_g'][l])),
        out=(bf(w['w_out'][l][:D_A]), bf(w['w_out'][l][D_A:]), row(w['ln2_g'][l]), row(w['ln2_b'][l])),
    )


def _trunk(x, states, lw, ffn_w, nb_r, tb_r, nb_m, tb_m):
    bsz, t, _ = x.shape
    n = bsz * t
    tm = min(1024, n // 2)
    lc = math.gcd(t, CHUNK)
    xf = x.reshape(n, D_MODEL)
    st_shift, st_wkv, st_conv, st_c, st_n, st_m = states
    new = [[] for _ in range(4)]
    wkv_out, c_out = None, None
    for l in range(DEPTH):
        wl = lw[l]
        x1 = _ffn_ln(xf, *ffn_w[0], *wl['ln1'], l=l, tm=tm)
        p3 = _in_proj(x1, wl['w_cat'], tm=tm).reshape(bsz, t, P_COLS)
        ya, shift, wkv_out = _rwkv(p3, st_shift[l].reshape(bsz, 1, RWKV_COLS), st_wkv, l, wkv_out,
                                   wl['rwkv'], nb_r, tb_r)
        yb, conv, c_out, n_t, m_t = _mlstm(p3, st_conv[l], st_c, l, c_out,
                                           st_n[l].reshape(bsz, H_B, 1, DK_B), st_m[l].reshape(bsz, H_B, 1, 1),
                                           wl['mlstm'], nb_m, tb_m, lc)
        x2 = _out_proj_ln(x1, ya.reshape(n, D_A), yb.reshape(n, D_B), *wl['out'], tm=tm)
        xf = _ffn_ln(x2, *ffn_w[1], *wl['ln3'], l=l, tm=tm)
        for idx, s in enumerate((shift.reshape(bsz, RWKV_COLS), conv, n_t.reshape(bsz, H_B, DK_B),
                                 m_t.reshape(bsz, H_B))):
            new[idx].append(s)
    shift_o, conv_o, n_o, m_o = [jnp.stack(s) for s in new]
    return xf.reshape(bsz, t, D_MODEL), [shift_o, wkv_out, conv_o, c_out, n_o, m_o]


def kernel(x_prompt, x_sample, state_shift, state_wkv, state_conv, state_C, state_n, state_m,
           ffn1_wg, ffn1_wu, ffn1_wd, ln1_g, ln1_b, w_in, mu_shift, w0, w2, a0, a2, g2, k_k, k_a, r_k,
           gn_g, gn_b, conv_w, conv_b, i_bias, f_bias, mh_g, w_out, ln2_g, ln2_b,
           ffn2_wg, ffn2_wu, ffn2_wd, ln3_g, ln3_b):
    w = dict(ffn1_wg=ffn1_wg, ffn1_wu=ffn1_wu, ffn1_wd=ffn1_wd, ln1_g=ln1_g, ln1_b=ln1_b, w_in=w_in,
             mu_shift=mu_shift, w0=w0, w2=w2, a0=a0, a2=a2, g2=g2, k_k=k_k, k_a=k_a, r_k=r_k,
             gn_g=gn_g, gn_b=gn_b, conv_w=conv_w, conv_b=conv_b, i_bias=i_bias, f_bias=f_bias,
             mh_g=mh_g, w_out=w_out, ln2_g=ln2_g, ln2_b=ln2_b, ffn2_wg=ffn2_wg, ffn2_wu=ffn2_wu,
             ffn2_wd=ffn2_wd, ln3_g=ln3_g, ln3_b=ln3_b)
    lw = [_layer_weights(l, w) for l in range(DEPTH)]
    ffn_w = ((ffn1_wg.astype(BF16), ffn1_wu.astype(BF16), ffn1_wd.astype(BF16)),
             (ffn2_wg.astype(BF16), ffn2_wu.astype(BF16), ffn2_wd.astype(BF16)))
    bp = x_prompt.shape[0]
    dt = x_prompt.dtype
    init = [jnp.zeros((DEPTH, bp, RWKV_COLS), dt),
            jnp.zeros((DEPTH, bp, H_A, N_A, N_A), dt),
            jnp.zeros((DEPTH, bp, CONV_W - 1, 2 * D_B), dt),
            jnp.zeros((DEPTH, bp, H_B, DK_B, DK_B), dt),
            jnp.zeros((DEPTH, bp, H_B, DK_B), dt),
            jnp.zeros((DEPTH, bp, H_B), dt)]
    y_p, ps = _trunk(x_prompt, init, lw, ffn_w, nb_r=8, tb_r=64, nb_m=1, tb_m=256)
    y_s, ss = _trunk(x_sample, [state_shift, state_wkv, state_conv, state_C, state_n, state_m], lw, ffn_w,
                     nb_r=8, tb_r=8, nb_m=8, tb_m=8)
    return (y_p, y_s, *ps, *ss)
```

```python
import functools
import math

import jax
import jax.numpy as jnp
from jax import lax
from jax.experimental import pallas as pl
from jax.experimental.pallas import tpu as pltpu

D_MODEL = 1024
DEPTH = 2
D_A = 512
N_A = 64
H_A = 8
D_B = 512
H_B = 4
DK_B = 128
R_W = 64
R_A = 64
R_G = 128
RWKV_COLS = 3 * D_A + R_W + R_A + R_G
CONV_W = 4
CHUNK = 64
D_FF = 2816
ALPHA = (2.0 * DEPTH) ** 0.25
LN_EPS = 1e-5
GN_EPS = 64e-5
MH_EPS = 1e-6

GATE_COLS = 256
ML_COLS = 4 * D_B
P_COLS = RWKV_COLS + GATE_COLS + ML_COLS
VMEM_LIMIT = 56 * 1024 * 1024

F32 = jnp.float32
BF16 = jnp.bfloat16


def _bdot(a, b):
    return jnp.dot(a.astype(BF16), b.astype(BF16), preferred_element_type=F32)


def _seg_dot(x_bf16, seg_half):
    h = seg_half.shape[0]
    return jnp.concatenate([jnp.dot(x_bf16[:, :h], seg_half, preferred_element_type=F32),
                            jnp.dot(x_bf16[:, h:], seg_half, preferred_element_type=F32)], axis=1)


def _seg_sum(x, seg_half):
    hi = x.astype(BF16)
    lo = (x - hi.astype(F32)).astype(BF16)
    return _seg_dot(hi, seg_half) + _seg_dot(lo, seg_half)


def _layer_norm_rows(y, g, b):
    mu = jnp.mean(y, axis=-1, keepdims=True)
    d = y - mu
    var = jnp.mean(d * d, axis=-1, keepdims=True)
    return d * lax.rsqrt(var + LN_EPS) * g + b


def _sigmoid(x):
    return 1.0 / (1.0 + jnp.exp(-x))


def _ffn_ln_kernel(*refs, mix):
    if mix:
        x_ref, ya_ref, yb_ref, wa_ref, wb_ref, g2_ref, b2_ref = refs[:7]
        wg_ref, wu_ref, wd_ref, g_ref, b_ref, o_ref, xb_scr, acc_scr, xin_scr = refs[7:]
    else:
        x_ref, wg_ref, wu_ref, wd_ref, g_ref, b_ref, o_ref, xb_scr, acc_scr = refs
    j = pl.program_id(1)

    @pl.when(j == 0)
    def _():
        if mix:
            proj = (jnp.dot(ya_ref[...].astype(BF16), wa_ref[...], preferred_element_type=F32)
                    + jnp.dot(yb_ref[...].astype(BF16), wb_ref[...], preferred_element_type=F32))
            xin = _layer_norm_rows(ALPHA * x_ref[...] + proj, g2_ref[...], b2_ref[...])
            xin_scr[...] = xin
        else:
            xin = x_ref[...]
        xb_scr[...] = xin.astype(BF16)
        acc_scr[...] = jnp.zeros_like(acc_scr)

    xb = xb_scr[...]
    hg = jnp.dot(xb, wg_ref[...], preferred_element_type=F32)
    hu = jnp.dot(xb, wu_ref[...], preferred_element_type=F32)
    h = (hg * _sigmoid(hg)) * hu
    acc_scr[...] += jnp.dot(h.astype(BF16), wd_ref[...], preferred_element_type=F32)

    @pl.when(j == pl.num_programs(1) - 1)
    def _():
        xin = xin_scr[...] if mix else x_ref[...]
        y = ALPHA * xin + 0.5 * acc_scr[...]
        o_ref[...] = _layer_norm_rows(y, g_ref[...], b_ref[...])


def _ffn_ln(x, wg, wu, wd, g, b, l, tm, tf=256, mix_in=None):
    n = x.shape[0]
    row_tile = lambda width: pl.BlockSpec((tm, width), lambda i, j: (i, 0))
    const = lambda shape: pl.BlockSpec(shape, lambda i, j: (0, 0))
    in_specs = [row_tile(D_MODEL)]
    args = [x]
    scratch = [pltpu.VMEM((tm, D_MODEL), BF16), pltpu.VMEM((tm, D_MODEL), F32)]
    if mix_in is not None:
        in_specs += [row_tile(D_A), row_tile(D_B), const((D_A, D_MODEL)), const((D_B, D_MODEL)),
                     const((1, D_MODEL)), const((1, D_MODEL))]
        args += list(mix_in)
        scratch.append(pltpu.VMEM((tm, D_MODEL), F32))
    in_specs += [
        pl.BlockSpec((None, D_MODEL, tf), lambda i, j: (l, 0, j)),
        pl.BlockSpec((None, D_MODEL, tf), lambda i, j: (l, 0, j)),
        pl.BlockSpec((None, tf, D_MODEL), lambda i, j: (l, j, 0)),
        const((1, D_MODEL)), const((1, D_MODEL)),
    ]
    args += [wg, wu, wd, g, b]
    return pl.pallas_call(
        functools.partial(_ffn_ln_kernel, mix=mix_in is not None),
        grid=(n // tm, D_FF // tf),
        in_specs=in_specs,
        out_specs=pl.BlockSpec((tm, D_MODEL), lambda i, j: (i, 0)),
        out_shape=jax.ShapeDtypeStruct((n, D_MODEL), F32),
        scratch_shapes=scratch,
        compiler_params=pltpu.CompilerParams(
            dimension_semantics=("parallel", "arbitrary"), vmem_limit_bytes=VMEM_LIMIT),
        name="ffn_ln",
    )(*args)


def _in_proj_kernel(x_ref, w_ref, o_ref):
    o_ref[...] = jnp.dot(x_ref[...].astype(BF16), w_ref[...], preferred_element_type=F32)


def _in_proj(x, w, tm):
    n = x.shape[0]
    return pl.pallas_call(
        _in_proj_kernel,
        grid=(n // tm,),
        in_specs=[
            pl.BlockSpec((tm, D_MODEL), lambda i: (i, 0)),
            pl.BlockSpec((D_MODEL, P_COLS), lambda i: (0, 0)),
        ],
        out_specs=pl.BlockSpec((tm, P_COLS), lambda i: (i, 0)),
        out_shape=jax.ShapeDtypeStruct((n, P_COLS), F32),
        compiler_params=pltpu.CompilerParams(
            dimension_semantics=("parallel",), vmem_limit_bytes=VMEM_LIMIT),
        name="in_proj",
    )(x, w)


def _rwkv_kernel(*refs, nb, tb, n_alias):
    (p_ref, sp_ref, s0_ref, mu_ref, w0_ref, w2a_ref, a0_ref, g2_ref, kk_ref, ka_ref, rk_ref,
     gng_ref, gnb_ref, seg_ref) = refs[:14]
    (ya_ref, so_ref, st_ref,
     s_scr, carry_scr, nk_scr, p2_scr, q1_scr, rp_scr, v_scr, vsw_scr, ya1_scr, b_scr, yc1_scr, k_scr, w12_scr,
     cbr_scr, ckr_scr, bonus_scr, g_scr, sa_scr, q_scr, yo_scr) = refs[14 + n_alias:]
    j = pl.program_id(1)
    n_hp = H_A // 2

    @pl.when(j == 0)
    def _():
        for b in range(nb):
            for hp in range(n_hp):
                s_scr[b, hp] = jnp.concatenate([s0_ref[b, 2 * hp], s0_ref[b, 2 * hp + 1]], axis=-1)
        carry_scr[...] = sp_ref[...]
        sa_scr[...] = jnp.zeros_like(sa_scr)
        q_scr[...] = jnp.zeros_like(q_scr)
        yo_scr[...] = jnp.zeros_like(yo_scr)

    seg = seg_ref[...]
    rows = nb * tb
    lane128 = lax.broadcasted_iota(jnp.int32, (rows, 128), 1)
    rowid = lax.broadcasted_iota(jnp.int32, (tb, RWKV_COLS), 0)
    rowid_a = lax.broadcasted_iota(jnp.int32, (tb, D_A), 0)

    pms, rps, r_last = [], [], []
    for b in range(nb):
        p = p_ref[b]
        prev = jnp.where(rowid == 0, carry_scr[b], pltpu.roll(p, 1, axis=0))
        carry_scr[b] = p[tb - 1:tb, :]
        pm_b = p + (prev - p) * mu_ref[...]
        r_b = pm_b[:, 0:D_A]
        pms.append(pm_b)
        rps.append(jnp.where(rowid_a == 0, 0.0, pltpu.roll(r_b, 1, axis=0)))
        r_last.append(r_b[tb - 1:tb, :])
    pm = jnp.concatenate(pms, axis=0)
    r = pm[:, 0:D_A]
    k = pm[:, D_A:2 * D_A]
    v = pm[:, 2 * D_A:3 * D_A]
    z = pm[:, 3 * D_A:3 * D_A + 128]
    xg = pm[:, 3 * D_A + 128:RWKV_COLS]
    zt = jnp.where(lane128 < R_W, jnp.tanh(z), z)
    lr = _bdot(zt, w2a_ref[...])
    g_scr[...] = _bdot(_sigmoid(xg), g2_ref[...]).reshape(nb, tb, D_A)
    kk = k * kk_ref[...]
    ss = _seg_sum(kk * kk, seg)
    w = jnp.exp(-math.exp(-0.5) * _sigmoid(w0_ref[...] + lr[:, :D_A]))
    a = _sigmoid(a0_ref[...] + lr[:, D_A:])
    kk = kk * lax.rsqrt(jnp.maximum(ss, 1e-24))
    kmod = k * (1.0 + (a - 1.0) * ka_ref[...])
    bonus_scr[...] = (_seg_sum(r * kmod * rk_ref[...], seg) * v).reshape(nb, tb, D_A)
    bb = kk * a
    nxt = lambda x: pltpu.roll(x, rows - 1, axis=0)
    kk_n, w_n, bb_n = nxt(kk), nxt(w), nxt(bb)
    cc = _seg_dot(jnp.concatenate([bb * kk_n, kmod * kk_n, bb * r, kmod * r], axis=0).astype(BF16), seg)
    cbk, ckk = cc[0:rows], cc[rows:2 * rows]
    cbr_scr[...] = cc[2 * rows:3 * rows].reshape(nb, tb, D_A)
    ckr_scr[...] = cc[3 * rows:4 * rows].reshape(nb, tb, D_A)
    def put(scr, arr):
        for hp in range(n_hp):
            scr[:, hp] = arr[:, hp * 128:(hp + 1) * 128].reshape(nb, tb, 128)

    put(nk_scr, -kk)
    put(p2_scr, -(w * kk_n))
    put(q1_scr, w * r)
    put(rp_scr, jnp.concatenate(rps, axis=0))
    put(ya1_scr, bb * w_n - cbk * bb_n)
    put(b_scr, bb)
    put(yc1_scr, kmod * w_n - ckk * bb_n)
    put(k_scr, kmod)
    put(w12_scr, w * w_n)
    put(v_scr, v)
    for hp in range(n_hp):
        vsw_scr[:, hp] = pltpu.roll(v[:, hp * 128:(hp + 1) * 128], 64, axis=1).reshape(nb, tb, 128)

    row8 = lax.broadcasted_iota(jnp.int32, (8, 128), 0)
    lane8 = lax.broadcasted_iota(jnp.int32, (8, 128), 1)
    pair8 = lax.shift_right_logical(row8, 1)
    half8 = ((((row8 & 1) == 0) & (lane8 < 64)) | (((row8 & 1) == 1) & (lane8 >= 64))).astype(F32)
    mk3 = ((row8 >= 6) & ((((row8 & 1) == 0) & (lane8 < 64)) | (((row8 & 1) == 1) & (lane8 >= 64)))).astype(F32)
    row64 = lax.broadcasted_iota(jnp.int32, (8, 64), 0)

    def pair_tile(q0, q1, q2, q3):
        return jnp.where(pair8 == 0, q0, jnp.where(pair8 == 1, q1, jnp.where(pair8 == 2, q2, q3))) * half8

    chains = [(b, hp) for b in range(nb) for hp in range(n_hp)]
    nt_dims = (((1,), (1,)), ((), ()))
    tn_dims = (((0,), (0,)), ((), ()))
    grp = min(16, tb)
    n_pair = grp // 2

    def natural_rows(rows_h0, rows_h1):
        pad = [jnp.zeros((8 - n_pair, N_A), F32)] if n_pair < 8 else []
        return jnp.concatenate([jnp.concatenate(rows_h0 + pad, axis=0),
                                jnp.concatenate(rows_h1 + pad, axis=0)], axis=1)

    def group(tg, carry):
        t0 = tg * grp

        def row(ref, c, t, n=8):
            return ref[c[0], c[1], pl.ds(t0 + t, n, stride=0), :]

        out = {c: [[] for _ in range(6)] for c in chains}
        for pi in range(n_pair):
            i = 2 * pi
            reds = {}
            for c in chains:
                a_mat = pair_tile(row(nk_scr, c, i), row(p2_scr, c, i), row(q1_scr, c, i), row(rp_scr, c, i))
                reds[c] = lax.dot_general(a_mat.astype(BF16), s_scr[c[0], c[1]].astype(BF16), nt_dims,
                                          preferred_element_type=F32)
            xs = {}
            for c in chains:
                x_mat = jnp.where(
                    row64 < 4, reds[c],
                    jnp.where(row64 == 4, row(v_scr, c, i)[:, :64],
                              jnp.where(row64 == 5, row(vsw_scr, c, i)[:, :64],
                                        jnp.where(row64 == 6, row(v_scr, c, i + 1)[:, :64],
                                                  row(vsw_scr, c, i + 1)[:, :64]))))
                xs[c] = x_mat.astype(BF16)
            for c in chains:
                y_mat = pair_tile(row(ya1_scr, c, i), row(b_scr, c, i + 1), row(yc1_scr, c, i), row(k_scr, c, i + 1))
                d_s = lax.dot_general(xs[c], y_mat.astype(BF16), tn_dims, preferred_element_type=F32)
                s_scr[c[0], c[1]] = s_scr[c[0], c[1]] * row(w12_scr, c, i, N_A) + d_s
                for slot, red_row in enumerate((0, 1, 4, 5, 6, 7)):
                    out[c][slot].append(reds[c][red_row:red_row + 1, :])
        even0 = pl.multiple_of(tg * grp, 8)
        for c in chains:
            o = out[c]
            sa_scr[c[0], c[1], pl.ds(even0, 8, stride=2), :] = natural_rows(o[0], o[1])
            q_scr[c[0], c[1], pl.ds(even0, 8, stride=2), :] = natural_rows(o[2], o[3])
            yo_scr[c[0], c[1], pl.ds(even0 + 8, 8, stride=2), :] = natural_rows(o[4], o[5])
        return carry

    lax.fori_loop(0, tb // grp, group, 0)

    for (b, hp) in chains:
        rl = r_last[b][:, hp * 128:(hp + 1) * 128]
        a_mat = rl * mk3
        red = lax.dot_general(a_mat.astype(BF16), s_scr[b, hp].astype(BF16), nt_dims, preferred_element_type=F32)
        y_last = jnp.concatenate([red[6:7, :], red[7:8, :]], axis=1)
        yo_scr[b, hp, pl.ds(tb + 8, 8), :] = jnp.broadcast_to(y_last, (8, 128))

    def slab(scr, first):
        return jnp.concatenate(
            [jnp.concatenate([scr[b, hp, pl.ds(first, tb), :] for hp in range(n_hp)], axis=1) for b in range(nb)],
            axis=0)

    y_even = (slab(q_scr, 0) + slab(sa_scr, 0) * cbr_scr[...].reshape(rows, D_A)
              + slab(v_scr, 0) * ckr_scr[...].reshape(rows, D_A))
    parity = lax.broadcasted_iota(jnp.int32, (rows, D_A), 0) & 1
    y = jnp.where(parity == 0, y_even, slab(yo_scr, 9))
    mu = _seg_sum(y, seg) * (1.0 / N_A)
    d = y - mu
    var = _seg_sum(d * d, seg) * (1.0 / N_A)
    yn = d * lax.rsqrt(var + GN_EPS) * gng_ref[...] + gnb_ref[...]
    ya_ref[...] = (yn.reshape(nb, tb, D_A) + bonus_scr[...]) * g_scr[...]

    @pl.when(j == pl.num_programs(1) - 1)
    def _():
        for b in range(nb):
            for hp in range(n_hp):
                s_pair = s_scr[b, hp]
                st_ref[b, 2 * hp] = s_pair[:, :N_A]
                st_ref[b, 2 * hp + 1] = s_pair[:, N_A:]
        so_ref[...] = carry_scr[...]


def _rwkv(p3, shift_prev, wkv_all, l, wkv_out_prev, wts, nb, tb):
    bsz, t, _ = p3.shape
    n_hp = H_A // 2
    full = lambda shape: pl.BlockSpec(shape, lambda i, j: (0,) * len(shape))
    blk = lambda: pltpu.VMEM((nb, tb, D_A), F32)
    n_alias = 0 if wkv_out_prev is None else 1
    kern = functools.partial(_rwkv_kernel, nb=nb, tb=tb, n_alias=n_alias)
    state_spec = pl.BlockSpec((None, nb, H_A, N_A, N_A), lambda i, j: (l, i, 0, 0, 0))
    in_specs = [
        pl.BlockSpec((nb, tb, RWKV_COLS), lambda i, j: (i, j, 0)),
        pl.BlockSpec((nb, 1, RWKV_COLS), lambda i, j: (i, 0, 0)),
        state_spec,
        full((1, RWKV_COLS)), full((1, D_A)), full((128, 2 * D_A)), full((1, D_A)), full((R_G, D_A)),
        full((1, D_A)), full((1, D_A)), full((1, D_A)), full((1, D_A)), full((1, D_A)), full((D_A // 2, D_A // 2)),
    ]
    args = [p3, shift_prev, wkv_all, *wts]
    aliases = {}
    if n_alias:
        in_specs.append(pl.BlockSpec(memory_space=pl.ANY))
        args.append(wkv_out_prev)
        aliases = {len(args) - 1: 2}
    return pl.pallas_call(
        kern,
        grid=(bsz // nb, t // tb),
        in_specs=in_specs,
        out_specs=[
            pl.BlockSpec((nb, tb, D_A), lambda i, j: (i, j, 0)),
            pl.BlockSpec((nb, 1, RWKV_COLS), lambda i, j: (i, 0, 0)),
            state_spec,
        ],
        out_shape=[
            jax.ShapeDtypeStruct((bsz, t, D_A), F32),
            jax.ShapeDtypeStruct((bsz, 1, RWKV_COLS), F32),
            jax.ShapeDtypeStruct((DEPTH, bsz, H_A, N_A, N_A), F32),
        ],
        scratch_shapes=[
            pltpu.VMEM((nb, n_hp, N_A, 128), F32),
            pltpu.VMEM((nb, 1, RWKV_COLS), F32),
            *[pltpu.VMEM((nb, n_hp, tb, 2 * N_A), F32) for _ in range(11)],
            *[blk() for _ in range(4)],
            *[pltpu.VMEM((nb, n_hp, tb + 24, 2 * N_A), F32) for _ in range(3)],
        ],
        input_output_aliases=aliases,
        compiler_params=pltpu.CompilerParams(
            dimension_semantics=("parallel", "arbitrary"), vmem_limit_bytes=VMEM_LIMIT),
        name="rwkv7",
    )(*args)


def _mlstm_kernel(*refs, nb, tb, lc, n_alias):
    (pm_ref, pg_ref, cp_ref, c0_ref, n0_ref, m0_ref, cw_ref, cb_ref, gb_ref, mhg_ref) = refs[:10]
    (yb_ref, co_ref, ct_ref, nt_ref, mt_ref, x_scr, c_scr, n_scr, m_scr) = refs[10 + n_alias:]
    j = pl.program_id(1)
    hi = lax.Precision.HIGHEST
    nch = tb // lc
    lc_shift = lc.bit_length() - 1

    @pl.when(j == 0)
    def _():
        c_scr[...] = c0_ref[...]
        n_scr[...] = n0_ref[...]
        m_scr[...] = jnp.broadcast_to(m0_ref[...], m_scr.shape)
        x_scr[:, 5:8, :] = cp_ref[...]

    rr = lax.broadcasted_iota(jnp.int32, (lc, lc), 0)
    cc = lax.broadcasted_iota(jnp.int32, (lc, lc), 1)
    causal = rr >= cc
    rb = lax.broadcasted_iota(jnp.int32, (tb, tb), 0)
    cb = lax.broadcasted_iota(jnp.int32, (tb, tb), 1)
    tri_blk = ((rb >= cb) & (lax.shift_right_logical(rb, lc_shift) == lax.shift_right_logical(cb, lc_shift))
               ).astype(F32)
    lane_g = lax.broadcasted_iota(jnp.int32, (tb, 128), 1)
    tn_dims = (((0,), (0,)), ((), ()))
    nt_dims = (((1,), (1,)), ((), ()))

    units = [(b, c, h) for b in range(nb) for c in range(nch) for h in range(H_B)]
    q_u, k_u, v_u, o_u = {}, {}, {}, {}
    li_col, li_row, b_col, b_row, b_l = {}, {}, {}, {}, {}

    for b in range(nb):
        x = pm_ref[b, :, 0:2 * D_B]
        x_scr[b, pl.ds(8, tb), :] = x
        conv = cb_ref[...] + x * cw_ref[3:4, :]
        for s in range(1, CONV_W):
            conv = conv + x_scr[b, pl.ds(8 - s, tb), :] * cw_ref[3 - s:4 - s, :]
        x_scr[b, 5:8, :] = x_scr[b, pl.ds(8 + tb - 3, 3), :]
        sc = conv * _sigmoid(conv)
        q_all = sc[:, :D_B]
        k_all = sc[:, D_B:] * (DK_B ** -0.5)
        v_all = pm_ref[b, :, 2 * D_B:3 * D_B]
        o_all = pm_ref[b, :, 3 * D_B:4 * D_B]
        gp = pg_ref[b, :, 0:128] + gb_ref[...]
        gates = jnp.where(lane_g < H_B, gp, jnp.minimum(gp, 0.0) - jnp.log1p(jnp.exp(-jnp.abs(gp))))
        csum_col = jnp.dot(tri_blk, gates, precision=hi, preferred_element_type=F32)
        gt = gates.T
        csum_row = lax.dot_general(gt, tri_blk, nt_dims, precision=hi, preferred_element_type=F32)
        for c in range(nch):
            rs = slice(c * lc, (c + 1) * lc)
            for h in range(H_B):
                u = (b, c, h)
                hs = slice(h * DK_B, (h + 1) * DK_B)
                q_u[u], k_u[u], v_u[u], o_u[u] = q_all[rs, hs], k_all[rs, hs], v_all[rs, hs], o_all[rs, hs]
                li_col[u] = gates[rs, h:h + 1]
                li_row[u] = gt[h:h + 1, rs]
                b_col[u] = csum_col[rs, H_B + h:H_B + h + 1]
                b_row[u] = csum_row[H_B + h:H_B + h + 1, rs]
                b_l[u] = csum_col[(c + 1) * lc - 1:(c + 1) * lc, H_B + h:H_B + h + 1]

    rep = lambda col: jnp.broadcast_to(col, (lc, DK_B))
    bc = {u: rep(b_col[u]) for u in units}
    lic = {u: rep(li_col[u]) for u in units}

    last_max = {u: jnp.max(b_l[u] - b_row[u] + li_row[u], axis=-1, keepdims=True) for u in units}
    m_prev, m_new = {}, {}
    for b in range(nb):
        for h in range(H_B):
            m_p = m_scr[b, h][:, 0:1]
            for c in range(nch):
                u = (b, c, h)
                m_prev[u] = m_p
                m_p = jnp.maximum(b_l[u] + m_p, last_max[u])
                m_new[u] = m_p
            m_scr[b, h] = jnp.broadcast_to(m_p, (1, 128))

    dmat = {u: jnp.where(causal, bc[u][:, :lc] - b_row[u] + li_row[u], -jnp.inf) for u in units}
    row_max = {u: rep(jnp.max(dmat[u], axis=-1, keepdims=True)) for u in units}
    m_t, g_inter, e_mat = {}, {}, {}
    for u in units:
        inter = bc[u] + m_prev[u]
        m_t[u] = jnp.maximum(inter, row_max[u])
        g_inter[u] = jnp.exp(inter - m_t[u])
        e_mat[u] = jnp.exp(dmat[u] - m_t[u][:, :lc])

    qk = {u: lax.dot_general(q_u[u].astype(BF16), k_u[u].astype(BF16), nt_dims, preferred_element_type=F32)
          for u in units}
    s_mat = {u: qk[u] * e_mat[u] for u in units}
    sv = {u: _bdot(s_mat[u], v_u[u]) for u in units}
    s_sum = {u: rep(jnp.sum(s_mat[u], axis=-1, keepdims=True)) for u in units}
    kw = {u: k_u[u] * jnp.exp(b_l[u] - bc[u] + lic[u] - m_new[u]) for u in units}
    kwv = {u: lax.dot_general(kw[u].astype(BF16), v_u[u].astype(BF16), tn_dims, preferred_element_type=F32)
           for u in units}

    q_c, qn_prod = {}, {}
    c_cur = {(b, h): c_scr[b, h] for b in range(nb) for h in range(H_B)}
    n_cur = {(b, h): n_scr[b, h] for b in range(nb) for h in range(H_B)}
    for c in range(nch):
        for b in range(nb):
            for h in range(H_B):
                u = (b, c, h)
                q_c[u] = _bdot(q_u[u], c_cur[(b, h)])
                qn_prod[u] = q_u[u] * n_cur[(b, h)]
                dec = jnp.exp(b_l[u] + m_prev[u] - m_new[u])
                c_cur[(b, h)] = dec * c_cur[(b, h)] + kwv[u]
                n_cur[(b, h)] = dec * n_cur[(b, h)] + jnp.sum(kw[u], axis=0, keepdims=True)
    for b in range(nb):
        for h in range(H_B):
            c_scr[b, h] = c_cur[(b, h)]
            n_scr[b, h] = n_cur[(b, h)]
    q_n = {u: rep(jnp.sum(qn_prod[u], axis=-1, keepdims=True)) for u in units}

    hh = {}
    for u in units:
        num = g_inter[u] * q_c[u] + sv[u]
        den = g_inter[u] * q_n[u] + s_sum[u]
        hh[u] = num / jnp.maximum(jnp.abs(den), jnp.exp(-m_t[u]))
    mu = {u: rep(jnp.sum(hh[u], axis=-1, keepdims=True)) * (1.0 / DK_B) for u in units}
    dev = {u: hh[u] - mu[u] for u in units}
    var = {u: rep(jnp.sum(dev[u] * dev[u], axis=-1, keepdims=True)) * (1.0 / DK_B) for u in units}
    for u in units:
        b, c, h = u
        hs = slice(h * DK_B, (h + 1) * DK_B)
        hn = dev[u] * lax.rsqrt(var[u] + MH_EPS) * mhg_ref[:, hs]
        yb_ref[b, c * lc:(c + 1) * lc, hs] = _sigmoid(o_u[u]) * hn

    @pl.when(j == pl.num_programs(1) - 1)
    def _():
        ct_ref[...] = c_scr[...]
        nt_ref[...] = n_scr[...]
        mt_ref[...] = m_scr[:, :, :, 0:1]
        co_ref[...] = x_scr[:, 5:8, :]


def _mlstm(p3, conv_prev, c_all, l, c_out_prev, n0, m0, wts, nb, tb, lc):
    bsz, t, _ = p3.shape
    full = lambda shape: pl.BlockSpec(shape, lambda i, j: (0,) * len(shape))
    n_alias = 0 if c_out_prev is None else 1
    kern = functools.partial(_mlstm_kernel, nb=nb, tb=tb, lc=lc, n_alias=n_alias)
    gate_blk = RWKV_COLS // GATE_COLS
    c_spec = pl.BlockSpec((None, nb, H_B, DK_B, DK_B), lambda i, j: (l, i, 0, 0, 0))
    in_specs = [
        pl.BlockSpec((nb, tb, ML_COLS), lambda i, j: (i, j, 1)),
        pl.BlockSpec((nb, tb, GATE_COLS), lambda i, j: (i, j, gate_blk)),
        pl.BlockSpec((nb, CONV_W - 1, 2 * D_B), lambda i, j: (i, 0, 0)),
        c_spec,
        pl.BlockSpec((nb, H_B, 1, DK_B), lambda i, j: (i, 0, 0, 0)),
        pl.BlockSpec((nb, H_B, 1, 1), lambda i, j: (i, 0, 0, 0)),
        full((CONV_W, 2 * D_B)), full((1, 2 * D_B)), full((1, 128)), full((1, D_B)),
    ]
    args = [p3, p3, conv_prev, c_all, n0, m0, *wts]
    aliases = {}
    if n_alias:
        in_specs.append(pl.BlockSpec(memory_space=pl.ANY))
        args.append(c_out_prev)
        aliases = {len(args) - 1: 2}
    return pl.pallas_call(
        kern,
        grid=(bsz // nb, t // tb),
        in_specs=in_specs,
        out_specs=[
            pl.BlockSpec((nb, tb, D_B), lambda i, j: (i, j, 0)),
            pl.BlockSpec((nb, CONV_W - 1, 2 * D_B), lambda i, j: (i, 0, 0)),
            c_spec,
            pl.BlockSpec((nb, H_B, 1, DK_B), lambda i, j: (i, 0, 0, 0)),
            pl.BlockSpec((nb, H_B, 1, 1), lambda i, j: (i, 0, 0, 0)),
        ],
        out_shape=[
            jax.ShapeDtypeStruct((bsz, t, D_B), F32),
            jax.ShapeDtypeStruct((bsz, CONV_W - 1, 2 * D_B), F32),
            jax.ShapeDtypeStruct((DEPTH, bsz, H_B, DK_B, DK_B), F32),
            jax.ShapeDtypeStruct((bsz, H_B, 1, DK_B), F32),
            jax.ShapeDtypeStruct((bsz, H_B, 1, 1), F32),
        ],
        scratch_shapes=[
            pltpu.VMEM((nb, tb + 8, 2 * D_B), F32),
            pltpu.VMEM((nb, H_B, DK_B, DK_B), F32),
            pltpu.VMEM((nb, H_B, 1, DK_B), F32),
            pltpu.VMEM((nb, H_B, 1, 128), F32),
        ],
        input_output_aliases=aliases,
        compiler_params=pltpu.CompilerParams(
            dimension_semantics=("parallel", "arbitrary"), vmem_limit_bytes=VMEM_LIMIT),
        name="mlstm",
    )(*args)


def _layer_weights(l, w):
    bf = lambda a: a.astype(BF16)
    row = lambda a: a.reshape(1, -1)
    w_in = w['w_in'][l]
    w_gates = jnp.pad(w_in[:, RWKV_COLS + ML_COLS:], ((0, 0), (0, GATE_COLS - 2 * H_B)))
    w_cat = jnp.concatenate([w_in[:, :RWKV_COLS], w_gates, w_in[:, RWKV_COLS:RWKV_COLS + ML_COLS]], axis=1)
    zero = jnp.zeros((R_W, D_A), F32)
    w2a = jnp.concatenate([jnp.concatenate([w['w2'][l], zero], axis=1),
                           jnp.concatenate([zero, w['a2'][l]], axis=1)], axis=0)
    ids = jnp.arange(D_A // 2) // N_A
    seg = (ids[:, None] == ids[None, :]).astype(BF16)
    gate_bias = jnp.pad(jnp.concatenate([w['i_bias'][l], w['f_bias'][l]]), (0, 128 - 2 * H_B)).reshape(1, 128)
    return dict(
        ln1=(row(w['ln1_g'][l]), row(w['ln1_b'][l])),
        ln3=(row(w['ln3_g'][l]), row(w['ln3_b'][l])),
        w_cat=bf(w_cat),
        rwkv=(row(w['mu_shift'][l]), row(w['w0'][l]), bf(w2a), row(w['a0'][l]), bf(w['g2'][l]), row(w['k_k'][l]),
              row(w['k_a'][l]), row(w['r_k'][l]), row(w['gn_g'][l]), row(w['gn_b'][l]), seg),
        mlstm=(w['conv_w'][l], row(w['conv_b'][l]), gate_bias, row(w['mh_g'][l])),
        out=(bf(w['w_out'][l][:D_A]), bf(w['w_out'][l][D_A:]), row(w['ln2_g'][l]), row(w['ln2_b'][l])),
    )


def _trunk(x, states, lw, ffn_w, nb_r, tb_r, nb_m, tb_m):
    bsz, t, _ = x.shape
    n = bsz * t
    tm = min(1024, n // 2)
    lc = math.gcd(t, CHUNK)
    xf = x.reshape(n, D_MODEL)
    st_shift, st_wkv, st_conv, st_c, st_n, st_m = states
    new = [[] for _ in range(4)]
    wkv_out = jnp.zeros((DEPTH, bsz, H_A, N_A, N_A), F32)
    c_out = jnp.zeros((DEPTH, bsz, H_B, DK_B, DK_B), F32)
    for l in range(DEPTH):
        wl = lw[l]
        x1 = _ffn_ln(xf, *ffn_w[0], *wl['ln1'], l=l, tm=tm)
        p3 = _in_proj(x1, wl['w_cat'], tm=min(tm, 512)).reshape(bsz, t, P_COLS)
        ya, shift, wkv_out = _rwkv(p3, st_shift[l].reshape(bsz, 1, RWKV_COLS), st_wkv, l, wkv_out,
                                   wl['rwkv'], nb_r, tb_r)
        yb, conv, c_out, n_t, m_t = _mlstm(p3, st_conv[l], st_c, l, c_out,
                                           st_n[l].reshape(bsz, H_B, 1, DK_B), st_m[l].reshape(bsz, H_B, 1, 1),
                                           wl['mlstm'], nb_m, tb_m, lc)
        xf = _ffn_ln(x1, *ffn_w[1], *wl['ln3'], l=l, tm=tm,
                     mix_in=(ya.reshape(n, D_A), yb.reshape(n, D_B), *wl['out']))
        for idx, s in enumerate((shift.reshape(bsz, RWKV_COLS), conv, n_t.reshape(bsz, H_B, DK_B),
                                 m_t.reshape(bsz, H_B))):
            new[idx].append(s)
    shift_o, conv_o, n_o, m_o = [jnp.stack(s) for s in new]
    return xf.reshape(bsz, t, D_MODEL), [shift_o, wkv_out, conv_o, c_out, n_o, m_o]


def kernel(x_prompt, x_sample, state_shift, state_wkv, state_conv, state_C, state_n, state_m,
           ffn1_wg, ffn1_wu, ffn1_wd, ln1_g, ln1_b, w_in, mu_shift, w0, w2, a0, a2, g2, k_k, k_a, r_k,
           gn_g, gn_b, conv_w, conv_b, i_bias, f_bias, mh_g, w_out, ln2_g, ln2_b,
           ffn2_wg, ffn2_wu, ffn2_wd, ln3_g, ln3_b):
    w = dict(ffn1_wg=ffn1_wg, ffn1_wu=ffn1_wu, ffn1_wd=ffn1_wd, ln1_g=ln1_g, ln1_b=ln1_b, w_in=w_in,
             mu_shift=mu_shift, w0=w0, w2=w2, a0=a0, a2=a2, g2=g2, k_k=k_k, k_a=k_a, r_k=r_k,
             gn_g=gn_g, gn_b=gn_b, conv_w=conv_w, conv_b=conv_b, i_bias=i_bias, f_bias=f_bias,
             mh_g=mh_g, w_out=w_out, ln2_g=ln2_g, ln2_b=ln2_b, ffn2_wg=ffn2_wg, ffn2_wu=ffn2_wu,
             ffn2_wd=ffn2_wd, ln3_g=ln3_g, ln3_b=ln3_b)
    lw = [_layer_weights(l, w) for l in range(DEPTH)]
    ffn_w = ((ffn1_wg.astype(BF16), ffn1_wu.astype(BF16), ffn1_wd.astype(BF16)),
             (ffn2_wg.astype(BF16), ffn2_wu.astype(BF16), ffn2_wd.astype(BF16)))
    bp = x_prompt.shape[0]
    dt = x_prompt.dtype
    init = [jnp.zeros((DEPTH, bp, RWKV_COLS), dt),
            jnp.zeros((DEPTH, bp, H_A, N_A, N_A), dt),
            jnp.zeros((DEPTH, bp, CONV_W - 1, 2 * D_B), dt),
            jnp.zeros((DEPTH, bp, H_B, DK_B, DK_B), dt),
            jnp.zeros((DEPTH, bp, H_B, DK_B), dt),
            jnp.zeros((DEPTH, bp, H_B), dt)]
    y_p, ps = _trunk(x_prompt, init, lw, ffn_w, nb_r=8, tb_r=64, nb_m=1, tb_m=256)
    y_s, ss = _trunk(x_sample, [state_shift, state_wkv, state_conv, state_C, state_n, state_m], lw, ffn_w,
                     nb_r=8, tb_r=8, nb_m=8, tb_m=8)
    return (y_p, y_s, *ps, *ss)
```

```python
import functools
import math

import jax
import jax.numpy as jnp
from jax import lax
from jax.experimental import pallas as pl
from jax.experimental.pallas import tpu as pltpu

D_MODEL = 1024
DEPTH = 2
D_A = 512
N_A = 64
H_A = 8
D_B = 512
H_B = 4
DK_B = 128
R_W = 64
R_A = 64
R_G = 128
RWKV_COLS = 3 * D_A + R_W + R_A + R_G
CONV_W = 4
CHUNK = 64
D_FF = 2816
ALPHA = (2.0 * DEPTH) ** 0.25
LN_EPS = 1e-5
GN_EPS = 64e-5
MH_EPS = 1e-6

GATE_COLS = 256
ML_COLS = 4 * D_B
P_COLS = RWKV_COLS + GATE_COLS + ML_COLS
VMEM_LIMIT = 56 * 1024 * 1024
FFN_ROW_CHUNK = 256

F32 = jnp.float32
BF16 = jnp.bfloat16


def _bdot(a, b):
    return jnp.dot(a.astype(BF16), b.astype(BF16), preferred_element_type=F32)


def _seg_dot(x_bf16, seg_half):
    h = seg_half.shape[0]
    return jnp.concatenate([jnp.dot(x_bf16[:, :h], seg_half, preferred_element_type=F32),
                            jnp.dot(x_bf16[:, h:], seg_half, preferred_element_type=F32)], axis=1)


def _seg_sum(x, seg_half):
    hi = x.astype(BF16)
    lo = (x - hi.astype(F32)).astype(BF16)
    return _seg_dot(hi, seg_half) + _seg_dot(lo, seg_half)


def _layer_norm_rows(y, g, b):
    mu = jnp.mean(y, axis=-1, keepdims=True)
    d = y - mu
    var = jnp.mean(d * d, axis=-1, keepdims=True)
    return d * lax.rsqrt(var + LN_EPS) * g + b


def _sigmoid(x):
    return 1.0 / (1.0 + jnp.exp(-x))


def _ffn_ln_kernel(*refs, mix, n_chunks):
    if mix:
        x_ref, ya_ref, yb_ref, wa_ref, wb_ref, g2_ref, b2_ref = refs[:7]
        wg_ref, wu_ref, wd_ref, g_ref, b_ref, o_ref, xb_scr, acc_scr, xin_scr = refs[7:]
    else:
        x_ref, wg_ref, wu_ref, wd_ref, g_ref, b_ref, o_ref, xb_scr, acc_scr = refs
    j = pl.program_id(1)
    last = pl.num_programs(1) - 1
    rc = x_ref.shape[0] // n_chunks

    def partial_ffn(xb, wg, wu, wd):
        hg = jnp.dot(xb, wg, preferred_element_type=F32)
        hu = jnp.dot(xb, wu, preferred_element_type=F32)
        h = (hg * _sigmoid(hg)) * hu
        return jnp.dot(h.astype(BF16), wd, preferred_element_type=F32)

    @pl.when(j == 0)
    def _():
        wg, wu, wd = wg_ref[...].astype(BF16), wu_ref[...].astype(BF16), wd_ref[...].astype(BF16)
        chunks = [pl.ds(c * rc, rc) for c in range(n_chunks)]
        if mix:
            projs = [jnp.dot(ya_ref[rows, :].astype(BF16), wa_ref[...], preferred_element_type=F32)
                     + jnp.dot(yb_ref[rows, :].astype(BF16), wb_ref[...], preferred_element_type=F32)
                     for rows in chunks]
        xbs = []
        for c, rows in enumerate(chunks):
            if mix:
                xin = _layer_norm_rows(ALPHA * x_ref[rows, :] + projs[c], g2_ref[...], b2_ref[...])
                xin_scr[rows, :] = xin
            else:
                xin = x_ref[rows, :]
            xb = xin.astype(BF16)
            xb_scr[rows, :] = xb
            xbs.append(xb)
        for c, rows in enumerate(chunks):
            acc_scr[rows, :] = partial_ffn(xbs[c], wg, wu, wd)

    @pl.when((j > 0) & (j < last))
    def _():
        acc_scr[...] += partial_ffn(xb_scr[...], wg_ref[...].astype(BF16), wu_ref[...].astype(BF16),
                                    wd_ref[...].astype(BF16))

    @pl.when(j == last)
    def _():
        wg, wu, wd = wg_ref[...].astype(BF16), wu_ref[...].astype(BF16), wd_ref[...].astype(BF16)
        for c in range(n_chunks):
            rows = pl.ds(c * rc, rc)
            acc = acc_scr[rows, :] + partial_ffn(xb_scr[rows, :], wg, wu, wd)
            xin = xin_scr[rows, :] if mix else x_ref[rows, :]
            o_ref[rows, :] = _layer_norm_rows(ALPHA * xin + 0.5 * acc, g_ref[...], b_ref[...])


def _ffn_ln(x, wg, wu, wd, g, b, l, tm, tf=256, mix_in=None):
    n = x.shape[0]
    row_tile = lambda width: pl.BlockSpec((tm, width), lambda i, j: (i, 0))
    const = lambda shape: pl.BlockSpec(shape, lambda i, j: (0, 0))
    in_specs = [row_tile(D_MODEL)]
    args = [x]
    scratch = [pltpu.VMEM((tm, D_MODEL), BF16), pltpu.VMEM((tm, D_MODEL), F32)]
    if mix_in is not None:
        in_specs += [row_tile(D_A), row_tile(D_B), const((D_A, D_MODEL)), const((D_B, D_MODEL)),
                     const((1, D_MODEL)), const((1, D_MODEL))]
        args += list(mix_in)
        scratch.append(pltpu.VMEM((tm, D_MODEL), F32))
    in_specs += [
        pl.BlockSpec((None, D_MODEL, tf), lambda i, j: (l, 0, j)),
        pl.BlockSpec((None, D_MODEL, tf), lambda i, j: (l, 0, j)),
        pl.BlockSpec((None, tf, D_MODEL), lambda i, j: (l, j, 0)),
        const((1, D_MODEL)), const((1, D_MODEL)),
    ]
    args += [wg, wu, wd, g, b]
    return pl.pallas_call(
        functools.partial(_ffn_ln_kernel, mix=mix_in is not None, n_chunks=tm // FFN_ROW_CHUNK),
        grid=(n // tm, D_FF // tf),
        in_specs=in_specs,
        out_specs=pl.BlockSpec((tm, D_MODEL), lambda i, j: (i, 0)),
        out_shape=jax.ShapeDtypeStruct((n, D_MODEL), F32),
        scratch_shapes=scratch,
        compiler_params=pltpu.CompilerParams(
            dimension_semantics=("parallel", "arbitrary"), vmem_limit_bytes=VMEM_LIMIT),
        name="ffn_ln",
    )(*args)


def _in_proj_kernel(x_ref, w_ref, o_ref):
    o_ref[...] = jnp.dot(x_ref[...].astype(BF16), w_ref[...], preferred_element_type=F32)


def _in_proj(x, w, tm):
    n = x.shape[0]
    return pl.pallas_call(
        _in_proj_kernel,
        grid=(n // tm,),
        in_specs=[
            pl.BlockSpec((tm, D_MODEL), lambda i: (i, 0)),
            pl.BlockSpec((D_MODEL, P_COLS), lambda i: (0, 0)),
        ],
        out_specs=pl.BlockSpec((tm, P_COLS), lambda i: (i, 0)),
        out_shape=jax.ShapeDtypeStruct((n, P_COLS), F32),
        compiler_params=pltpu.CompilerParams(
            dimension_semantics=("parallel",), vmem_limit_bytes=VMEM_LIMIT),
        name="in_proj",
    )(x, w)


def _rwkv_kernel(*refs, nb, tb, n_alias):
    (p_ref, sp_ref, s0_ref, mu_ref, w0_ref, w2a_ref, a0_ref, g2_ref, kk_ref, ka_ref, rk_ref,
     gng_ref, gnb_ref, seg_ref) = refs[:14]
    (ya_ref, so_ref, st_ref,
     s_scr, carry_scr, nk_scr, p2_scr, q1_scr, rp_scr, v_scr, vsw_scr, ya1_scr, b_scr, yc1_scr, k_scr, w12_scr,
     cbr_scr, ckr_scr, bonus_scr, g_scr, sa_scr, q_scr, yo_scr) = refs[14 + n_alias:]
    j = pl.program_id(1)
    n_hp = H_A // 2

    @pl.when(j == 0)
    def _():
        for b in range(nb):
            for hp in range(n_hp):
                s_scr[b, hp] = jnp.concatenate([s0_ref[b, 2 * hp], s0_ref[b, 2 * hp + 1]], axis=-1)
        carry_scr[...] = sp_ref[...]
        sa_scr[...] = jnp.zeros_like(sa_scr)
        q_scr[...] = jnp.zeros_like(q_scr)
        yo_scr[...] = jnp.zeros_like(yo_scr)

    seg = seg_ref[...]
    rows = nb * tb
    lane128 = lax.broadcasted_iota(jnp.int32, (rows, 128), 1)
    rowid = lax.broadcasted_iota(jnp.int32, (tb, RWKV_COLS), 0)
    rowid_a = lax.broadcasted_iota(jnp.int32, (tb, D_A), 0)

    pms, rps, r_last = [], [], []
    for b in range(nb):
        p = p_ref[b]
        prev = jnp.where(rowid == 0, carry_scr[b], pltpu.roll(p, 1, axis=0))
        carry_scr[b] = p[tb - 1:tb, :]
        pm_b = p + (prev - p) * mu_ref[...]
        r_b = pm_b[:, 0:D_A]
        pms.append(pm_b)
        rps.append(jnp.where(rowid_a == 0, 0.0, pltpu.roll(r_b, 1, axis=0)))
        r_last.append(r_b[tb - 1:tb, :])
    pm = jnp.concatenate(pms, axis=0)
    r = pm[:, 0:D_A]
    k = pm[:, D_A:2 * D_A]
    v = pm[:, 2 * D_A:3 * D_A]
    z = pm[:, 3 * D_A:3 * D_A + 128]
    xg = pm[:, 3 * D_A + 128:RWKV_COLS]
    zt = jnp.where(lane128 < R_W, jnp.tanh(z), z)
    lr = _bdot(zt, w2a_ref[...])
    g_scr[...] = _bdot(_sigmoid(xg), g2_ref[...]).reshape(nb, tb, D_A)
    kk = k * kk_ref[...]
    ss = _seg_sum(kk * kk, seg)
    w = jnp.exp(-math.exp(-0.5) * _sigmoid(w0_ref[...] + lr[:, :D_A]))
    a = _sigmoid(a0_ref[...] + lr[:, D_A:])
    kk = kk * lax.rsqrt(jnp.maximum(ss, 1e-24))
    kmod = k * (1.0 + (a - 1.0) * ka_ref[...])
    bonus_scr[...] = (_seg_sum(r * kmod * rk_ref[...], seg) * v).reshape(nb, tb, D_A)
    bb = kk * a
    nxt = lambda x: pltpu.roll(x, rows - 1, axis=0)
    kk_n, w_n, bb_n = nxt(kk), nxt(w), nxt(bb)
    cc = _seg_dot(jnp.concatenate([bb * kk_n, kmod * kk_n, bb * r, kmod * r], axis=0).astype(BF16), seg)
    cbk, ckk = cc[0:rows], cc[rows:2 * rows]
    cbr_scr[...] = cc[2 * rows:3 * rows].reshape(nb, tb, D_A)
    ckr_scr[...] = cc[3 * rows:4 * rows].reshape(nb, tb, D_A)
    def put(scr, arr):
        for hp in range(n_hp):
            scr[:, hp] = arr[:, hp * 128:(hp + 1) * 128].reshape(nb, tb, 128)

    put(nk_scr, -kk)
    put(p2_scr, -(w * kk_n))
    put(q1_scr, w * r)
    put(rp_scr, jnp.concatenate(rps, axis=0))
    put(ya1_scr, bb * w_n - cbk * bb_n)
    put(b_scr, bb)
    put(yc1_scr, kmod * w_n - ckk * bb_n)
    put(k_scr, kmod)
    put(w12_scr, w * w_n)
    put(v_scr, v)
    for hp in range(n_hp):
        vsw_scr[:, hp] = pltpu.roll(v[:, hp * 128:(hp + 1) * 128], 64, axis=1).reshape(nb, tb, 128)

    row8 = lax.broadcasted_iota(jnp.int32, (8, 128), 0)
    lane8 = lax.broadcasted_iota(jnp.int32, (8, 128), 1)
    pair8 = lax.shift_right_logical(row8, 1)
    half8 = ((((row8 & 1) == 0) & (lane8 < 64)) | (((row8 & 1) == 1) & (lane8 >= 64))).astype(F32)
    mk3 = ((row8 >= 6) & ((((row8 & 1) == 0) & (lane8 < 64)) | (((row8 & 1) == 1) & (lane8 >= 64)))).astype(F32)
    row64 = lax.broadcasted_iota(jnp.int32, (8, 64), 0)

    def pair_tile(q0, q1, q2, q3):
        return jnp.where(pair8 == 0, q0, jnp.where(pair8 == 1, q1, jnp.where(pair8 == 2, q2, q3))) * half8

    chains = [(b, hp) for b in range(nb) for hp in range(n_hp)]
    nt_dims = (((1,), (1,)), ((), ()))
    tn_dims = (((0,), (0,)), ((), ()))
    grp = min(16, tb)
    n_pair = grp // 2

    def natural_rows(rows_h0, rows_h1):
        pad = [jnp.zeros((8 - n_pair, N_A), F32)] if n_pair < 8 else []
        return jnp.concatenate([jnp.concatenate(rows_h0 + pad, axis=0),
                                jnp.concatenate(rows_h1 + pad, axis=0)], axis=1)

    def group(tg, carry):
        t0 = tg * grp

        def row(ref, c, t, n=8):
            return ref[c[0], c[1], pl.ds(t0 + t, n, stride=0), :]

        out = {c: [[] for _ in range(6)] for c in chains}
        for pi in range(n_pair):
            i = 2 * pi
            reds = {}
            for c in chains:
                a_mat = pair_tile(row(nk_scr, c, i), row(p2_scr, c, i), row(q1_scr, c, i), row(rp_scr, c, i))
                reds[c] = lax.dot_general(a_mat.astype(BF16), s_scr[c[0], c[1]].astype(BF16), nt_dims,
                                          preferred_element_type=F32)
            xs = {}
            for c in chains:
                x_mat = jnp.where(
                    row64 < 4, reds[c],
                    jnp.where(row64 == 4, row(v_scr, c, i)[:, :64],
                              jnp.where(row64 == 5, row(vsw_scr, c, i)[:, :64],
                                        jnp.where(row64 == 6, row(v_scr, c, i + 1)[:, :64],
                                                  row(vsw_scr, c, i + 1)[:, :64]))))
                xs[c] = x_mat.astype(BF16)
            for c in chains:
                y_mat = pair_tile(row(ya1_scr, c, i), row(b_scr, c, i + 1), row(yc1_scr, c, i), row(k_scr, c, i + 1))
                d_s = lax.dot_general(xs[c], y_mat.astype(BF16), tn_dims, preferred_element_type=F32)
                s_scr[c[0], c[1]] = s_scr[c[0], c[1]] * row(w12_scr, c, i, N_A) + d_s
                for slot, red_row in enumerate((0, 1, 4, 5, 6, 7)):
                    out[c][slot].append(reds[c][red_row:red_row + 1, :])
        even0 = pl.multiple_of(tg * grp, 8)
        for c in chains:
            o = out[c]
            sa_scr[c[0], c[1], pl.ds(even0, 8, stride=2), :] = natural_rows(o[0], o[1])
            q_scr[c[0], c[1], pl.ds(even0, 8, stride=2), :] = natural_rows(o[2], o[3])
            yo_scr[c[0], c[1], pl.ds(even0 + 8, 8, stride=2), :] = natural_rows(o[4], o[5])
        return carry

    lax.fori_loop(0, tb // grp, group, 0)

    for (b, hp) in chains:
        rl = r_last[b][:, hp * 128:(hp + 1) * 128]
        a_mat = rl * mk3
        red = lax.dot_general(a_mat.astype(BF16), s_scr[b, hp].astype(BF16), nt_dims, preferred_element_type=F32)
        y_last = jnp.concatenate([red[6:7, :], red[7:8, :]], axis=1)
        yo_scr[b, hp, pl.ds(tb + 8, 8), :] = jnp.broadcast_to(y_last, (8, 128))

    def slab(scr, first):
        return jnp.concatenate(
            [jnp.concatenate([scr[b, hp, pl.ds(first, tb), :] for hp in range(n_hp)], axis=1) for b in range(nb)],
            axis=0)

    y_even = (slab(q_scr, 0) + slab(sa_scr, 0) * cbr_scr[...].reshape(rows, D_A)
              + slab(v_scr, 0) * ckr_scr[...].reshape(rows, D_A))
    parity = lax.broadcasted_iota(jnp.int32, (rows, D_A), 0) & 1
    y = jnp.where(parity == 0, y_even, slab(yo_scr, 9))
    mu = _seg_sum(y, seg) * (1.0 / N_A)
    d = y - mu
    var = _seg_sum(d * d, seg) * (1.0 / N_A)
    yn = d * lax.rsqrt(var + GN_EPS) * gng_ref[...] + gnb_ref[...]
    ya_ref[...] = (yn.reshape(nb, tb, D_A) + bonus_scr[...]) * g_scr[...]

    @pl.when(j == pl.num_programs(1) - 1)
    def _():
        for b in range(nb):
            for hp in range(n_hp):
                s_pair = s_scr[b, hp]
                st_ref[b, 2 * hp] = s_pair[:, :N_A]
                st_ref[b, 2 * hp + 1] = s_pair[:, N_A:]
        so_ref[...] = carry_scr[...]


def _rwkv(p3, shift_prev, wkv_all, l, wkv_out_prev, wts, nb, tb):
    bsz, t, _ = p3.shape
    n_hp = H_A // 2
    full = lambda shape: pl.BlockSpec(shape, lambda i, j: (0,) * len(shape))
    blk = lambda: pltpu.VMEM((nb, tb, D_A), F32)
    n_alias = 0 if wkv_out_prev is None else 1
    kern = functools.partial(_rwkv_kernel, nb=nb, tb=tb, n_alias=n_alias)
    state_spec = pl.BlockSpec((None, nb, H_A, N_A, N_A), lambda i, j: (l, i, 0, 0, 0))
    in_specs = [
        pl.BlockSpec((nb, tb, RWKV_COLS), lambda i, j: (i, j, 0)),
        pl.BlockSpec((nb, 1, RWKV_COLS), lambda i, j: (i, 0, 0)),
        state_spec,
        full((1, RWKV_COLS)), full((1, D_A)), full((128, 2 * D_A)), full((1, D_A)), full((R_G, D_A)),
        full((1, D_A)), full((1, D_A)), full((1, D_A)), full((1, D_A)), full((1, D_A)), full((D_A // 2, D_A // 2)),
    ]
    args = [p3, shift_prev, wkv_all, *wts]
    aliases = {}
    if n_alias:
        in_specs.append(pl.BlockSpec(memory_space=pl.ANY))
        args.append(wkv_out_prev)
        aliases = {len(args) - 1: 2}
    return pl.pallas_call(
        kern,
        grid=(bsz // nb, t // tb),
        in_specs=in_specs,
        out_specs=[
            pl.BlockSpec((nb, tb, D_A), lambda i, j: (i, j, 0)),
            pl.BlockSpec((nb, 1, RWKV_COLS), lambda i, j: (i, 0, 0)),
            state_spec,
        ],
        out_shape=[
            jax.ShapeDtypeStruct((bsz, t, D_A), F32),
            jax.ShapeDtypeStruct((bsz, 1, RWKV_COLS), F32),
            jax.ShapeDtypeStruct((DEPTH, bsz, H_A, N_A, N_A), F32),
        ],
        scratch_shapes=[
            pltpu.VMEM((nb, n_hp, N_A, 128), F32),
            pltpu.VMEM((nb, 1, RWKV_COLS), F32),
            *[pltpu.VMEM((nb, n_hp, tb, 2 * N_A), F32) for _ in range(11)],
            *[blk() for _ in range(4)],
            *[pltpu.VMEM((nb, n_hp, tb + 24, 2 * N_A), F32) for _ in range(3)],
        ],
        input_output_aliases=aliases,
        compiler_params=pltpu.CompilerParams(
            dimension_semantics=("parallel", "arbitrary"), vmem_limit_bytes=VMEM_LIMIT),
        name="rwkv7",
    )(*args)


def _mlstm_kernel(*refs, nb, tb, lc, n_alias):
    (pm_ref, pg_ref, cp_ref, c0_ref, n0_ref, m0_ref, cw_ref, cb_ref, gb_ref, mhg_ref) = refs[:10]
    (yb_ref, co_ref, ct_ref, nt_ref, mt_ref, x_scr, c_scr, n_scr, m_scr) = refs[10 + n_alias:]
    j = pl.program_id(1)
    hi = lax.Precision.HIGHEST
    nch = tb // lc
    lc_shift = lc.bit_length() - 1

    @pl.when(j == 0)
    def _():
        c_scr[...] = c0_ref[...]
        n_scr[...] = n0_ref[...]
        m_scr[...] = jnp.broadcast_to(m0_ref[...], m_scr.shape)
        x_scr[:, 5:8, :] = cp_ref[...]

    rr = lax.broadcasted_iota(jnp.int32, (lc, lc), 0)
    cc = lax.broadcasted_iota(jnp.int32, (lc, lc), 1)
    causal = rr >= cc
    rb = lax.broadcasted_iota(jnp.int32, (tb, tb), 0)
    cb = lax.broadcasted_iota(jnp.int32, (tb, tb), 1)
    tri_blk = ((rb >= cb) & (lax.shift_right_logical(rb, lc_shift) == lax.shift_right_logical(cb, lc_shift))
               ).astype(F32)
    lane_g = lax.broadcasted_iota(jnp.int32, (tb, 128), 1)
    tn_dims = (((0,), (0,)), ((), ()))
    nt_dims = (((1,), (1,)), ((), ()))

    units = [(b, c, h) for b in range(nb) for c in range(nch) for h in range(H_B)]
    q_u, k_u, v_u, o_u = {}, {}, {}, {}
    li_col, li_row, b_col, b_row, b_l = {}, {}, {}, {}, {}

    for b in range(nb):
        x = pm_ref[b, :, 0:2 * D_B]
        x_scr[b, pl.ds(8, tb), :] = x
        conv = cb_ref[...] + x * cw_ref[3:4, :]
        for s in range(1, CONV_W):
            conv = conv + x_scr[b, pl.ds(8 - s, tb), :] * cw_ref[3 - s:4 - s, :]
        x_scr[b, 5:8, :] = x_scr[b, pl.ds(8 + tb - 3, 3), :]
        sc = conv * _sigmoid(conv)
        q_all = sc[:, :D_B]
        k_all = sc[:, D_B:] * (DK_B ** -0.5)
        v_all = pm_ref[b, :, 2 * D_B:3 * D_B]
        o_all = pm_ref[b, :, 3 * D_B:4 * D_B]
        gp = pg_ref[b, :, 0:128] + gb_ref[...]
        gates = jnp.where(lane_g < H_B, gp, jnp.minimum(gp, 0.0) - jnp.log1p(jnp.exp(-jnp.abs(gp))))
        csum_col = jnp.dot(tri_blk, gates, precision=hi, preferred_element_type=F32)
        gt = gates.T
        csum_row = lax.dot_general(gt, tri_blk, nt_dims, precision=hi, preferred_element_type=F32)
        for c in range(nch):
            rs = slice(c * lc, (c + 1) * lc)
            for h in range(H_B):
                u = (b, c, h)
                hs = slice(h * DK_B, (h + 1) * DK_B)
                q_u[u], k_u[u], v_u[u], o_u[u] = q_all[rs, hs], k_all[rs, hs], v_all[rs, hs], o_all[rs, hs]
                li_col[u] = gates[rs, h:h + 1]
                li_row[u] = gt[h:h + 1, rs]
                b_col[u] = csum_col[rs, H_B + h:H_B + h + 1]
                b_row[u] = csum_row[H_B + h:H_B + h + 1, rs]
                b_l[u] = csum_col[(c + 1) * lc - 1:(c + 1) * lc, H_B + h:H_B + h + 1]

    rep = lambda col: jnp.broadcast_to(col, (lc, DK_B))
    bc = {u: rep(b_col[u]) for u in units}
    lic = {u: rep(li_col[u]) for u in units}

    last_max = {u: jnp.max(b_l[u] - b_row[u] + li_row[u], axis=-1, keepdims=True) for u in units}
    m_prev, m_new = {}, {}
    for b in range(nb):
        for h in range(H_B):
            m_p = m_scr[b, h][:, 0:1]
            for c in range(nch):
                u = (b, c, h)
                m_prev[u] = m_p
                m_p = jnp.maximum(b_l[u] + m_p, last_max[u])
                m_new[u] = m_p
            m_scr[b, h] = jnp.broadcast_to(m_p, (1, 128))

    dmat = {u: jnp.where(causal, bc[u][:, :lc] - b_row[u] + li_row[u], -jnp.inf) for u in units}
    row_max = {u: rep(jnp.max(dmat[u], axis=-1, keepdims=True)) for u in units}
    m_t, g_inter, e_mat = {}, {}, {}
    for u in units:
        inter = bc[u] + m_prev[u]
        m_t[u] = jnp.maximum(inter, row_max[u])
        g_inter[u] = jnp.exp(inter - m_t[u])
        e_mat[u] = jnp.exp(dmat[u] - m_t[u][:, :lc])

    qk = {u: lax.dot_general(q_u[u].astype(BF16), k_u[u].astype(BF16), nt_dims, preferred_element_type=F32)
          for u in units}
    s_mat = {u: qk[u] * e_mat[u] for u in units}
    sv = {u: _bdot(s_mat[u], v_u[u]) for u in units}
    s_sum = {u: rep(jnp.sum(s_mat[u], axis=-1, keepdims=True)) for u in units}
    kw = {u: k_u[u] * jnp.exp(b_l[u] - bc[u] + lic[u] - m_new[u]) for u in units}
    kwv = {u: lax.dot_general(kw[u].astype(BF16), v_u[u].astype(BF16), tn_dims, preferred_element_type=F32)
           for u in units}

    q_c, qn_prod = {}, {}
    c_cur = {(b, h): c_scr[b, h] for b in range(nb) for h in range(H_B)}
    n_cur = {(b, h): n_scr[b, h] for b in range(nb) for h in range(H_B)}
    for c in range(nch):
        for b in range(nb):
            for h in range(H_B):
                u = (b, c, h)
                q_c[u] = _bdot(q_u[u], c_cur[(b, h)])
                qn_prod[u] = q_u[u] * n_cur[(b, h)]
                dec = jnp.exp(b_l[u] + m_prev[u] - m_new[u])
                c_cur[(b, h)] = dec * c_cur[(b, h)] + kwv[u]
                n_cur[(b, h)] = dec * n_cur[(b, h)] + jnp.sum(kw[u], axis=0, keepdims=True)
    for b in range(nb):
        for h in range(H_B):
            c_scr[b, h] = c_cur[(b, h)]
            n_scr[b, h] = n_cur[(b, h)]
    q_n = {u: rep(jnp.sum(qn_prod[u], axis=-1, keepdims=True)) for u in units}

    hh = {}
    for u in units:
        num = g_inter[u] * q_c[u] + sv[u]
        den = g_inter[u] * q_n[u] + s_sum[u]
        hh[u] = num / jnp.maximum(jnp.abs(den), jnp.exp(-m_t[u]))
    mu = {u: rep(jnp.sum(hh[u], axis=-1, keepdims=True)) * (1.0 / DK_B) for u in units}
    dev = {u: hh[u] - mu[u] for u in units}
    var = {u: rep(jnp.sum(dev[u] * dev[u], axis=-1, keepdims=True)) * (1.0 / DK_B) for u in units}
    for u in units:
        b, c, h = u
        hs = slice(h * DK_B, (h + 1) * DK_B)
        hn = dev[u] * lax.rsqrt(var[u] + MH_EPS) * mhg_ref[:, hs]
        yb_ref[b, c * lc:(c + 1) * lc, hs] = _sigmoid(o_u[u]) * hn

    @pl.when(j == pl.num_programs(1) - 1)
    def _():
        ct_ref[...] = c_scr[...]
        nt_ref[...] = n_scr[...]
        mt_ref[...] = m_scr[:, :, :, 0:1]
        co_ref[...] = x_scr[:, 5:8, :]


def _mlstm(p3, conv_prev, c_all, l, c_out_prev, n0, m0, wts, nb, tb, lc):
    bsz, t, _ = p3.shape
    full = lambda shape: pl.BlockSpec(shape, lambda i, j: (0,) * len(shape))
    n_alias = 0 if c_out_prev is None else 1
    kern = functools.partial(_mlstm_kernel, nb=nb, tb=tb, lc=lc, n_alias=n_alias)
    gate_blk = RWKV_COLS // GATE_COLS
    c_spec = pl.BlockSpec((None, nb, H_B, DK_B, DK_B), lambda i, j: (l, i, 0, 0, 0))
    in_specs = [
        pl.BlockSpec((nb, tb, ML_COLS), lambda i, j: (i, j, 1)),
        pl.BlockSpec((nb, tb, GATE_COLS), lambda i, j: (i, j, gate_blk)),
        pl.BlockSpec((nb, CONV_W - 1, 2 * D_B), lambda i, j: (i, 0, 0)),
        c_spec,
        pl.BlockSpec((nb, H_B, 1, DK_B), lambda i, j: (i, 0, 0, 0)),
        pl.BlockSpec((nb, H_B, 1, 1), lambda i, j: (i, 0, 0, 0)),
        full((CONV_W, 2 * D_B)), full((1, 2 * D_B)), full((1, 128)), full((1, D_B)),
    ]
    args = [p3, p3, conv_prev, c_all, n0, m0, *wts]
    aliases = {}
    if n_alias:
        in_specs.append(pl.BlockSpec(memory_space=pl.ANY))
        args.append(c_out_prev)
        aliases = {len(args) - 1: 2}
    return pl.pallas_call(
        kern,
        grid=(bsz // nb, t // tb),
        in_specs=in_specs,
        out_specs=[
            pl.BlockSpec((nb, tb, D_B), lambda i, j: (i, j, 0)),
            pl.BlockSpec((nb, CONV_W - 1, 2 * D_B), lambda i, j: (i, 0, 0)),
            c_spec,
            pl.BlockSpec((nb, H_B, 1, DK_B), lambda i, j: (i, 0, 0, 0)),
            pl.BlockSpec((nb, H_B, 1, 1), lambda i, j: (i, 0, 0, 0)),
        ],
        out_shape=[
            jax.ShapeDtypeStruct((bsz, t, D_B), F32),
            jax.ShapeDtypeStruct((bsz, CONV_W - 1, 2 * D_B), F32),
            jax.ShapeDtypeStruct((DEPTH, bsz, H_B, DK_B, DK_B), F32),
            jax.ShapeDtypeStruct((bsz, H_B, 1, DK_B), F32),
            jax.ShapeDtypeStruct((bsz, H_B, 1, 1), F32),
        ],
        scratch_shapes=[
            pltpu.VMEM((nb, tb + 8, 2 * D_B), F32),
            pltpu.VMEM((nb, H_B, DK_B, DK_B), F32),
            pltpu.VMEM((nb, H_B, 1, DK_B), F32),
            pltpu.VMEM((nb, H_B, 1, 128), F32),
        ],
        input_output_aliases=aliases,
        compiler_params=pltpu.CompilerParams(
            dimension_semantics=("parallel", "arbitrary"), vmem_limit_bytes=VMEM_LIMIT),
        name="mlstm",
    )(*args)


def _layer_weights(l, w):
    bf = lambda a: a.astype(BF16)
    row = lambda a: a.reshape(1, -1)
    w_in = w['w_in'][l]
    w_gates = jnp.pad(w_in[:, RWKV_COLS + ML_COLS:], ((0, 0), (0, GATE_COLS - 2 * H_B)))
    w_cat = jnp.concatenate([w_in[:, :RWKV_COLS], w_gates, w_in[:, RWKV_COLS:RWKV_COLS + ML_COLS]], axis=1)
    zero = jnp.zeros((R_W, D_A), F32)
    w2a = jnp.concatenate([jnp.concatenate([w['w2'][l], zero], axis=1),
                           jnp.concatenate([zero, w['a2'][l]], axis=1)], axis=0)
    ids = jnp.arange(D_A // 2) // N_A
    seg = (ids[:, None] == ids[None, :]).astype(BF16)
    gate_bias = jnp.pad(jnp.concatenate([w['i_bias'][l], w['f_bias'][l]]), (0, 128 - 2 * H_B)).reshape(1, 128)
    return dict(
        ln1=(row(w['ln1_g'][l]), row(w['ln1_b'][l])),
        ln3=(row(w['ln3_g'][l]), row(w['ln3_b'][l])),
        w_cat=bf(w_cat),
        rwkv=(row(w['mu_shift'][l]), row(w['w0'][l]), bf(w2a), row(w['a0'][l]), bf(w['g2'][l]), row(w['k_k'][l]),
              row(w['k_a'][l]), row(w['r_k'][l]), row(w['gn_g'][l]), row(w['gn_b'][l]), seg),
        mlstm=(w['conv_w'][l], row(w['conv_b'][l]), gate_bias, row(w['mh_g'][l])),
        out=(bf(w['w_out'][l][:D_A]), bf(w['w_out'][l][D_A:]), row(w['ln2_g'][l]), row(w['ln2_b'][l])),
    )


def _trunk(x, states, lw, ffn_w, nb_r, tb_r, nb_m, tb_m):
    bsz, t, _ = x.shape
    n = bsz * t
    tm = min(1024, n)
    lc = math.gcd(t, CHUNK)
    xf = x.reshape(n, D_MODEL)
    st_shift, st_wkv, st_conv, st_c, st_n, st_m = states
    new = [[] for _ in range(4)]
    wkv_out = jnp.zeros((DEPTH, bsz, H_A, N_A, N_A), F32)
    c_out = jnp.zeros((DEPTH, bsz, H_B, DK_B, DK_B), F32)
    for l in range(DEPTH):
        wl = lw[l]
        x1 = _ffn_ln(xf, *ffn_w[0], *wl['ln1'], l=l, tm=tm)
        p3 = _in_proj(x1, wl['w_cat'], tm=min(tm, 512)).reshape(bsz, t, P_COLS)
        ya, shift, wkv_out = _rwkv(p3, st_shift[l].reshape(bsz, 1, RWKV_COLS), st_wkv, l, wkv_out,
                                   wl['rwkv'], nb_r, tb_r)
        yb, conv, c_out, n_t, m_t = _mlstm(p3, st_conv[l], st_c, l, c_out,
                                           st_n[l].reshape(bsz, H_B, 1, DK_B), st_m[l].reshape(bsz, H_B, 1, 1),
                                           wl['mlstm'], nb_m, tb_m, lc)
        xf = _ffn_ln(x1, *ffn_w[1], *wl['ln3'], l=l, tm=tm,
                     mix_in=(ya.reshape(n, D_A), yb.reshape(n, D_B), *wl['out']))
        for idx, s in enumerate((shift.reshape(bsz, RWKV_COLS), conv, n_t.reshape(bsz, H_B, DK_B),
                                 m_t.reshape(bsz, H_B))):
            new[idx].append(s)
    shift_o, conv_o, n_o, m_o = [jnp.stack(s) for s in new]
    return xf.reshape(bsz, t, D_MODEL), [shift_o, wkv_out, conv_o, c_out, n_o, m_o]


def kernel(x_prompt, x_sample, state_shift, state_wkv, state_conv, state_C, state_n, state_m,
           ffn1_wg, ffn1_wu, ffn1_wd, ln1_g, ln1_b, w_in, mu_shift, w0, w2, a0, a2, g2, k_k, k_a, r_k,
           gn_g, gn_b, conv_w, conv_b, i_bias, f_bias, mh_g, w_out, ln2_g, ln2_b,
           ffn2_wg, ffn2_wu, ffn2_wd, ln3_g, ln3_b):
    w = dict(ffn1_wg=ffn1_wg, ffn1_wu=ffn1_wu, ffn1_wd=ffn1_wd, ln1_g=ln1_g, ln1_b=ln1_b, w_in=w_in,
             mu_shift=mu_shift, w0=w0, w2=w2, a0=a0, a2=a2, g2=g2, k_k=k_k, k_a=k_a, r_k=r_k,
             gn_g=gn_g, gn_b=gn_b, conv_w=conv_w, conv_b=conv_b, i_bias=i_bias, f_bias=f_bias,
             mh_g=mh_g, w_out=w_out, ln2_g=ln2_g, ln2_b=ln2_b, ffn2_wg=ffn2_wg, ffn2_wu=ffn2_wu,
             ffn2_wd=ffn2_wd, ln3_g=ln3_g, ln3_b=ln3_b)
    lw = [_layer_weights(l, w) for l in range(DEPTH)]
    ffn_w = ((ffn1_wg, ffn1_wu, ffn1_wd), (ffn2_wg, ffn2_wu, ffn2_wd))
    bp = x_prompt.shape[0]
    dt = x_prompt.dtype
    init = [jnp.zeros((DEPTH, bp, RWKV_COLS), dt),
            jnp.zeros((DEPTH, bp, H_A, N_A, N_A), dt),
            jnp.zeros((DEPTH, bp, CONV_W - 1, 2 * D_B), dt),
            jnp.zeros((DEPTH, bp, H_B, DK_B, DK_B), dt),
            jnp.zeros((DEPTH, bp, H_B, DK_B), dt),
            jnp.zeros((DEPTH, bp, H_B), dt)]
    y_p, ps = _trunk(x_prompt, init, lw, ffn_w, nb_r=8, tb_r=64, nb_m=1, tb_m=256)
    y_s, ss = _trunk(x_sample, [state_shift, state_wkv, state_conv, state_C, state_n, state_m], lw, ffn_w,
                     nb_r=8, tb_r=8, nb_m=8, tb_m=8)
    return (y_p, y_s, *ps, *ss)
```

```python
import functools
import math

import jax
import jax.numpy as jnp
from jax import lax
from jax.experimental import pallas as pl
from jax.experimental.pallas import tpu as pltpu

D_MODEL = 1024
DEPTH = 2
D_A = 512
N_A = 64
H_A = 8
D_B = 512
H_B = 4
DK_B = 128
R_W = 64
R_A = 64
R_G = 128
RWKV_COLS = 3 * D_A + R_W + R_A + R_G
CONV_W = 4
CHUNK = 64
D_FF = 2816
ALPHA = (2.0 * DEPTH) ** 0.25
LN_EPS = 1e-5
GN_EPS = 64e-5
MH_EPS = 1e-6

GATE_COLS = 256
ML_COLS = 4 * D_B
P_COLS = RWKV_COLS + GATE_COLS + ML_COLS
VMEM_LIMIT = 56 * 1024 * 1024
FFN_ROW_CHUNK = 256
FFN_COL_BLOCK = 256

F32 = jnp.float32
BF16 = jnp.bfloat16


def _bdot(a, b):
    return jnp.dot(a.astype(BF16), b.astype(BF16), preferred_element_type=F32)


def _seg_dot(x_bf16, seg_half):
    h = seg_half.shape[0]
    return jnp.concatenate([jnp.dot(x_bf16[:, :h], seg_half, preferred_element_type=F32),
                            jnp.dot(x_bf16[:, h:], seg_half, preferred_element_type=F32)], axis=1)


def _seg_sum(x, seg_half):
    hi = x.astype(BF16)
    lo = (x - hi.astype(F32)).astype(BF16)
    return _seg_dot(hi, seg_half) + _seg_dot(lo, seg_half)


def _layer_norm_rows(y, g, b):
    mu = jnp.mean(y, axis=-1, keepdims=True)
    d = y - mu
    var = jnp.mean(d * d, axis=-1, keepdims=True)
    return d * lax.rsqrt(var + LN_EPS) * g + b


def _sigmoid(x):
    return 1.0 / (1.0 + jnp.exp(-x))


def _ffn_ln_kernel(*refs, mix, n_chunks):
    if mix:
        x_ref, ya_ref, yb_ref, wa_ref, wb_ref, g2_ref, b2_ref = refs[:7]
        wg_ref, wu_ref, wd_ref, g_ref, b_ref, o_ref, xb_scr, acc_scr, xin_scr = refs[7:]
    else:
        x_ref, wg_ref, wu_ref, wd_ref, g_ref, b_ref, o_ref, xb_scr, acc_scr = refs
    j = pl.program_id(1)
    last = pl.num_programs(1) - 1
    rc = x_ref.shape[0] // n_chunks

    def partial_ffn(xb, wg, wu, wd):
        hg = jnp.dot(xb, wg, preferred_element_type=F32)
        hu = jnp.dot(xb, wu, preferred_element_type=F32)
        h = (hg * _sigmoid(hg)) * hu
        return jnp.dot(h.astype(BF16), wd, preferred_element_type=F32)

    @pl.when(j == 0)
    def _():
        wg, wu, wd = wg_ref[...], wu_ref[...], wd_ref[...]
        chunks = [pl.ds(c * rc, rc) for c in range(n_chunks)]
        if mix:
            projs = [jnp.dot(ya_ref[rows, :].astype(BF16), wa_ref[...], preferred_element_type=F32)
                     + jnp.dot(yb_ref[rows, :].astype(BF16), wb_ref[...], preferred_element_type=F32)
                     for rows in chunks]
        xbs = []
        for c, rows in enumerate(chunks):
            if mix:
                xin = _layer_norm_rows(ALPHA * x_ref[rows, :] + projs[c], g2_ref[...], b2_ref[...])
                xin_scr[rows, :] = xin
            else:
                xin = x_ref[rows, :]
            xb = xin.astype(BF16)
            xb_scr[rows, :] = xb
            xbs.append(xb)
        for c, rows in enumerate(chunks):
            acc_scr[rows, :] = partial_ffn(xbs[c], wg, wu, wd)

    @pl.when((j > 0) & (j < last))
    def _():
        acc_scr[...] += partial_ffn(xb_scr[...], wg_ref[...], wu_ref[...], wd_ref[...])

    @pl.when(j == last)
    def _():
        wg, wu, wd = wg_ref[...], wu_ref[...], wd_ref[...]
        for c in range(n_chunks):
            rows = pl.ds(c * rc, rc)
            acc = acc_scr[rows, :] + partial_ffn(xb_scr[rows, :], wg, wu, wd)
            xin = xin_scr[rows, :] if mix else x_ref[rows, :]
            o_ref[rows, :] = _layer_norm_rows(ALPHA * xin + 0.5 * acc, g_ref[...], b_ref[...])


def _ffn_ln(x, wg, wu, wd, g, b, l, tm, mix_in=None):
    n = x.shape[0]
    row_tile = lambda width: pl.BlockSpec((tm, width), lambda i, j: (i, 0))
    const = lambda shape: pl.BlockSpec(shape, lambda i, j: (0, 0))
    in_specs = [row_tile(D_MODEL)]
    args = [x]
    scratch = [pltpu.VMEM((tm, D_MODEL), BF16), pltpu.VMEM((tm, D_MODEL), F32)]
    if mix_in is not None:
        in_specs += [row_tile(D_A), row_tile(D_B), const((D_A, D_MODEL)), const((D_B, D_MODEL)),
                     const((1, D_MODEL)), const((1, D_MODEL))]
        args += list(mix_in)
        scratch.append(pltpu.VMEM((tm, D_MODEL), F32))
    in_specs += [
        pl.BlockSpec((None, None, D_MODEL, FFN_COL_BLOCK), lambda i, j: (l, j, 0, 0)),
        pl.BlockSpec((None, None, D_MODEL, FFN_COL_BLOCK), lambda i, j: (l, j, 0, 0)),
        pl.BlockSpec((None, FFN_COL_BLOCK, D_MODEL), lambda i, j: (l, j, 0)),
        const((1, D_MODEL)), const((1, D_MODEL)),
    ]
    args += [wg, wu, wd, g, b]
    return pl.pallas_call(
        functools.partial(_ffn_ln_kernel, mix=mix_in is not None, n_chunks=tm // FFN_ROW_CHUNK),
        grid=(n // tm, D_FF // FFN_COL_BLOCK),
        in_specs=in_specs,
        out_specs=pl.BlockSpec((tm, D_MODEL), lambda i, j: (i, 0)),
        out_shape=jax.ShapeDtypeStruct((n, D_MODEL), F32),
        scratch_shapes=scratch,
        compiler_params=pltpu.CompilerParams(
            dimension_semantics=("parallel", "arbitrary"), vmem_limit_bytes=VMEM_LIMIT),
        name="ffn_ln",
    )(*args)


def _in_proj_kernel(x_ref, w_ref, o_ref):
    o_ref[...] = jnp.dot(x_ref[...].astype(BF16), w_ref[...], preferred_element_type=F32)


def _in_proj(x, w, tm):
    n = x.shape[0]
    return pl.pallas_call(
        _in_proj_kernel,
        grid=(n // tm,),
        in_specs=[
            pl.BlockSpec((tm, D_MODEL), lambda i: (i, 0)),
            pl.BlockSpec((D_MODEL, P_COLS), lambda i: (0, 0)),
        ],
        out_specs=pl.BlockSpec((tm, P_COLS), lambda i: (i, 0)),
        out_shape=jax.ShapeDtypeStruct((n, P_COLS), F32),
        compiler_params=pltpu.CompilerParams(
            dimension_semantics=("parallel",), vmem_limit_bytes=VMEM_LIMIT),
        name="in_proj",
    )(x, w)


def _rwkv_kernel(*refs, nb, tb, n_alias):
    (p_ref, sp_ref, s0_ref, mu_ref, w0_ref, w2a_ref, a0_ref, g2_ref, kk_ref, ka_ref, rk_ref,
     gng_ref, gnb_ref, seg_ref) = refs[:14]
    (ya_ref, so_ref, st_ref,
     s_scr, carry_scr, nk_scr, p2_scr, q1_scr, rp_scr, v_scr, vsw_scr, ya1_scr, b_scr, yc1_scr, k_scr, w12_scr,
     cbr_scr, ckr_scr, bonus_scr, g_scr, sa_scr, q_scr, yo_scr) = refs[14 + n_alias:]
    j = pl.program_id(1)
    n_hp = H_A // 2

    @pl.when(j == 0)
    def _():
        for b in range(nb):
            for hp in range(n_hp):
                s_scr[b, hp] = jnp.concatenate([s0_ref[b, 2 * hp], s0_ref[b, 2 * hp + 1]], axis=-1)
        carry_scr[...] = sp_ref[...]
        sa_scr[...] = jnp.zeros_like(sa_scr)
        q_scr[...] = jnp.zeros_like(q_scr)
        yo_scr[...] = jnp.zeros_like(yo_scr)

    seg = seg_ref[...]
    rows = nb * tb
    lane128 = lax.broadcasted_iota(jnp.int32, (rows, 128), 1)
    rowid = lax.broadcasted_iota(jnp.int32, (tb, RWKV_COLS), 0)
    rowid_a = lax.broadcasted_iota(jnp.int32, (tb, D_A), 0)

    pms, rps, r_last = [], [], []
    for b in range(nb):
        p = p_ref[b]
        prev = jnp.where(rowid == 0, carry_scr[b], pltpu.roll(p, 1, axis=0))
        carry_scr[b] = p[tb - 1:tb, :]
        pm_b = p + (prev - p) * mu_ref[...]
        r_b = pm_b[:, 0:D_A]
        pms.append(pm_b)
        rps.append(jnp.where(rowid_a == 0, 0.0, pltpu.roll(r_b, 1, axis=0)))
        r_last.append(r_b[tb - 1:tb, :])
    pm = jnp.concatenate(pms, axis=0)
    r = pm[:, 0:D_A]
    k = pm[:, D_A:2 * D_A]
    v = pm[:, 2 * D_A:3 * D_A]
    z = pm[:, 3 * D_A:3 * D_A + 128]
    xg = pm[:, 3 * D_A + 128:RWKV_COLS]
    zt = jnp.where(lane128 < R_W, jnp.tanh(z), z)
    lr = _bdot(zt, w2a_ref[...])
    g_scr[...] = _bdot(_sigmoid(xg), g2_ref[...]).reshape(nb, tb, D_A)
    kk = k * kk_ref[...]
    ss = _seg_sum(kk * kk, seg)
    w = jnp.exp(-math.exp(-0.5) * _sigmoid(w0_ref[...] + lr[:, :D_A]))
    a = _sigmoid(a0_ref[...] + lr[:, D_A:])
    kk = kk * lax.rsqrt(jnp.maximum(ss, 1e-24))
    kmod = k * (1.0 + (a - 1.0) * ka_ref[...])
    bonus_scr[...] = (_seg_sum(r * kmod * rk_ref[...], seg) * v).reshape(nb, tb, D_A)
    bb = kk * a
    nxt = lambda x: pltpu.roll(x, rows - 1, axis=0)
    kk_n, w_n, bb_n = nxt(kk), nxt(w), nxt(bb)
    cc = _seg_dot(jnp.concatenate([bb * kk_n, kmod * kk_n, bb * r, kmod * r], axis=0).astype(BF16), seg)
    cbk, ckk = cc[0:rows], cc[rows:2 * rows]
    cbr_scr[...] = cc[2 * rows:3 * rows].reshape(nb, tb, D_A)
    ckr_scr[...] = cc[3 * rows:4 * rows].reshape(nb, tb, D_A)
    def put(scr, arr):
        for hp in range(n_hp):
            scr[:, hp] = arr[:, hp * 128:(hp + 1) * 128].reshape(nb, tb, 128)

    put(nk_scr, -kk)
    put(p2_scr, -(w * kk_n))
    put(q1_scr, w * r)
    put(rp_scr, jnp.concatenate(rps, axis=0))
    put(ya1_scr, bb * w_n - cbk * bb_n)
    put(b_scr, bb)
    put(yc1_scr, kmod * w_n - ckk * bb_n)
    put(k_scr, kmod)
    put(w12_scr, w * w_n)
    put(v_scr, v)
    for hp in range(n_hp):
        vsw_scr[:, hp] = pltpu.roll(v[:, hp * 128:(hp + 1) * 128], 64, axis=1).reshape(nb, tb, 128)

    row8 = lax.broadcasted_iota(jnp.int32, (8, 128), 0)
    lane8 = lax.broadcasted_iota(jnp.int32, (8, 128), 1)
    pair8 = lax.shift_right_logical(row8, 1)
    half8 = ((((row8 & 1) == 0) & (lane8 < 64)) | (((row8 & 1) == 1) & (lane8 >= 64))).astype(F32)
    mk3 = ((row8 >= 6) & ((((row8 & 1) == 0) & (lane8 < 64)) | (((row8 & 1) == 1) & (lane8 >= 64)))).astype(F32)
    row64 = lax.broadcasted_iota(jnp.int32, (8, 64), 0)

    def pair_tile(q0, q1, q2, q3):
        return jnp.where(pair8 == 0, q0, jnp.where(pair8 == 1, q1, jnp.where(pair8 == 2, q2, q3))) * half8

    chains = [(b, hp) for b in range(nb) for hp in range(n_hp)]
    nt_dims = (((1,), (1,)), ((), ()))
    tn_dims = (((0,), (0,)), ((), ()))
    grp = min(16, tb)
    n_pair = grp // 2

    def natural_rows(rows_h0, rows_h1):
        pad = [jnp.zeros((8 - n_pair, N_A), F32)] if n_pair < 8 else []
        return jnp.concatenate([jnp.concatenate(rows_h0 + pad, axis=0),
                                jnp.concatenate(rows_h1 + pad, axis=0)], axis=1)

    def group(tg, carry):
        t0 = tg * grp

        def row(ref, c, t, n=8):
            return ref[c[0], c[1], pl.ds(t0 + t, n, stride=0), :]

        out = {c: [[] for _ in range(6)] for c in chains}
        for pi in range(n_pair):
            i = 2 * pi
            reds = {}
            for c in chains:
                a_mat = pair_tile(row(nk_scr, c, i), row(p2_scr, c, i), row(q1_scr, c, i), row(rp_scr, c, i))
                reds[c] = lax.dot_general(a_mat.astype(BF16), s_scr[c[0], c[1]].astype(BF16), nt_dims,
                                          preferred_element_type=F32)
            xs = {}
            for c in chains:
                x_mat = jnp.where(
                    row64 < 4, reds[c],
                    jnp.where(row64 == 4, row(v_scr, c, i)[:, :64],
                              jnp.where(row64 == 5, row(vsw_scr, c, i)[:, :64],
                                        jnp.where(row64 == 6, row(v_scr, c, i + 1)[:, :64],
                                                  row(vsw_scr, c, i + 1)[:, :64]))))
                xs[c] = x_mat.astype(BF16)
            for c in chains:
                y_mat = pair_tile(row(ya1_scr, c, i), row(b_scr, c, i + 1), row(yc1_scr, c, i), row(k_scr, c, i + 1))
                d_s = lax.dot_general(xs[c], y_mat.astype(BF16), tn_dims, preferred_element_type=F32)
                s_scr[c[0], c[1]] = s_scr[c[0], c[1]] * row(w12_scr, c, i, N_A) + d_s
                for slot, red_row in enumerate((0, 1, 4, 5, 6, 7)):
                    out[c][slot].append(reds[c][red_row:red_row + 1, :])
        even0 = pl.multiple_of(tg * grp, 8)
        for c in chains:
            o = out[c]
            sa_scr[c[0], c[1], pl.ds(even0, 8, stride=2), :] = natural_rows(o[0], o[1])
            q_scr[c[0], c[1], pl.ds(even0, 8, stride=2), :] = natural_rows(o[2], o[3])
            yo_scr[c[0], c[1], pl.ds(even0 + 8, 8, stride=2), :] = natural_rows(o[4], o[5])
        return carry

    lax.fori_loop(0, tb // grp, group, 0)

    for (b, hp) in chains:
        rl = r_last[b][:, hp * 128:(hp + 1) * 128]
        a_mat = rl * mk3
        red = lax.dot_general(a_mat.astype(BF16), s_scr[b, hp].astype(BF16), nt_dims, preferred_element_type=F32)
        y_last = jnp.concatenate([red[6:7, :], red[7:8, :]], axis=1)
        yo_scr[b, hp, pl.ds(tb + 8, 8), :] = jnp.broadcast_to(y_last, (8, 128))

    def slab(scr, first):
        return jnp.concatenate(
            [jnp.concatenate([scr[b, hp, pl.ds(first, tb), :] for hp in range(n_hp)], axis=1) for b in range(nb)],
            axis=0)

    y_even = (slab(q_scr, 0) + slab(sa_scr, 0) * cbr_scr[...].reshape(rows, D_A)
              + slab(v_scr, 0) * ckr_scr[...].reshape(rows, D_A))
    parity = lax.broadcasted_iota(jnp.int32, (rows, D_A), 0) & 1
    y = jnp.where(parity == 0, y_even, slab(yo_scr, 9))
    mu = _seg_sum(y, seg) * (1.0 / N_A)
    d = y - mu
    var = _seg_sum(d * d, seg) * (1.0 / N_A)
    yn = d * lax.rsqrt(var + GN_EPS) * gng_ref[...] + gnb_ref[...]
    ya_ref[...] = (yn.reshape(nb, tb, D_A) + bonus_scr[...]) * g_scr[...]

    @pl.when(j == pl.num_programs(1) - 1)
    def _():
        for b in range(nb):
            for hp in range(n_hp):
                s_pair = s_scr[b, hp]
                st_ref[b, 2 * hp] = s_pair[:, :N_A]
                st_ref[b, 2 * hp + 1] = s_pair[:, N_A:]
        so_ref[...] = carry_scr[...]


def _rwkv(p3, shift_prev, wkv_all, l, wkv_out_prev, wts, nb, tb):
    bsz, t, _ = p3.shape
    n_hp = H_A // 2
    full = lambda shape: pl.BlockSpec(shape, lambda i, j: (0,) * len(shape))
    blk = lambda: pltpu.VMEM((nb, tb, D_A), F32)
    n_alias = 0 if wkv_out_prev is None else 1
    kern = functools.partial(_rwkv_kernel, nb=nb, tb=tb, n_alias=n_alias)
    state_spec = pl.BlockSpec((None, nb, H_A, N_A, N_A), lambda i, j: (l, i, 0, 0, 0))
    in_specs = [
        pl.BlockSpec((nb, tb, RWKV_COLS), lambda i, j: (i, j, 0)),
        pl.BlockSpec((nb, 1, RWKV_COLS), lambda i, j: (i, 0, 0)),
        state_spec,
        full((1, RWKV_COLS)), full((1, D_A)), full((128, 2 * D_A)), full((1, D_A)), full((R_G, D_A)),
        full((1, D_A)), full((1, D_A)), full((1, D_A)), full((1, D_A)), full((1, D_A)), full((D_A // 2, D_A // 2)),
    ]
    args = [p3, shift_prev, wkv_all, *wts]
    aliases = {}
    if n_alias:
        in_specs.append(pl.BlockSpec(memory_space=pl.ANY))
        args.append(wkv_out_prev)
        aliases = {len(args) - 1: 2}
    return pl.pallas_call(
        kern,
        grid=(bsz // nb, t // tb),
        in_specs=in_specs,
        out_specs=[
            pl.BlockSpec((nb, tb, D_A), lambda i, j: (i, j, 0)),
            pl.BlockSpec((nb, 1, RWKV_COLS), lambda i, j: (i, 0, 0)),
            state_spec,
        ],
        out_shape=[
            jax.ShapeDtypeStruct((bsz, t, D_A), F32),
            jax.ShapeDtypeStruct((bsz, 1, RWKV_COLS), F32),
            jax.ShapeDtypeStruct((DEPTH, bsz, H_A, N_A, N_A), F32),
        ],
        scratch_shapes=[
            pltpu.VMEM((nb, n_hp, N_A, 128), F32),
            pltpu.VMEM((nb, 1, RWKV_COLS), F32),
            *[pltpu.VMEM((nb, n_hp, tb, 2 * N_A), F32) for _ in range(11)],
            *[blk() for _ in range(4)],
            *[pltpu.VMEM((nb, n_hp, tb + 24, 2 * N_A), F32) for _ in range(3)],
        ],
        input_output_aliases=aliases,
        compiler_params=pltpu.CompilerParams(
            dimension_semantics=("parallel", "arbitrary"), vmem_limit_bytes=VMEM_LIMIT),
        name="rwkv7",
    )(*args)


def _mlstm_kernel(*refs, nb, tb, lc, n_alias):
    (pm_ref, pg_ref, cp_ref, c0_ref, n0_ref, m0_ref, cw_ref, cb_ref, gb_ref, mhg_ref) = refs[:10]
    (yb_ref, co_ref, ct_ref, nt_ref, mt_ref, x_scr, c_scr, n_scr, m_scr) = refs[10 + n_alias:]
    j = pl.program_id(1)
    hi = lax.Precision.HIGHEST
    nch = tb // lc
    lc_shift = lc.bit_length() - 1

    @pl.when(j == 0)
    def _():
        c_scr[...] = c0_ref[...]
        n_scr[...] = n0_ref[...]
        m_scr[...] = jnp.broadcast_to(m0_ref[...], m_scr.shape)
        x_scr[:, 5:8, :] = cp_ref[...]

    rr = lax.broadcasted_iota(jnp.int32, (lc, lc), 0)
    cc = lax.broadcasted_iota(jnp.int32, (lc, lc), 1)
    causal = rr >= cc
    rb = lax.broadcasted_iota(jnp.int32, (tb, tb), 0)
    cb = lax.broadcasted_iota(jnp.int32, (tb, tb), 1)
    tri_blk = ((rb >= cb) & (lax.shift_right_logical(rb, lc_shift) == lax.shift_right_logical(cb, lc_shift))
               ).astype(F32)
    lane_g = lax.broadcasted_iota(jnp.int32, (tb, 128), 1)
    tn_dims = (((0,), (0,)), ((), ()))
    nt_dims = (((1,), (1,)), ((), ()))

    units = [(b, c, h) for b in range(nb) for c in range(nch) for h in range(H_B)]
    q_u, k_u, v_u, o_u = {}, {}, {}, {}
    li_col, li_row, b_col, b_row, b_l = {}, {}, {}, {}, {}

    for b in range(nb):
        x = pm_ref[b, :, 0:2 * D_B]
        x_scr[b, pl.ds(8, tb), :] = x
        conv = cb_ref[...] + x * cw_ref[3:4, :]
        for s in range(1, CONV_W):
            conv = conv + x_scr[b, pl.ds(8 - s, tb), :] * cw_ref[3 - s:4 - s, :]
        x_scr[b, 5:8, :] = x_scr[b, pl.ds(8 + tb - 3, 3), :]
        sc = conv * _sigmoid(conv)
        q_all = sc[:, :D_B]
        k_all = sc[:, D_B:] * (DK_B ** -0.5)
        v_all = pm_ref[b, :, 2 * D_B:3 * D_B]
        o_all = pm_ref[b, :, 3 * D_B:4 * D_B]
        gp = pg_ref[b, :, 0:128] + gb_ref[...]
        gates = jnp.where(lane_g < H_B, gp, jnp.minimum(gp, 0.0) - jnp.log1p(jnp.exp(-jnp.abs(gp))))
        csum_col = jnp.dot(tri_blk, gates, precision=hi, preferred_element_type=F32)
        gt = gates.T
        csum_row = lax.dot_general(gt, tri_blk, nt_dims, precision=hi, preferred_element_type=F32)
        for c in range(nch):
            rs = slice(c * lc, (c + 1) * lc)
            for h in range(H_B):
                u = (b, c, h)
                hs = slice(h * DK_B, (h + 1) * DK_B)
                q_u[u], k_u[u], v_u[u], o_u[u] = q_all[rs, hs], k_all[rs, hs], v_all[rs, hs], o_all[rs, hs]
                li_col[u] = gates[rs, h:h + 1]
                li_row[u] = gt[h:h + 1, rs]
                b_col[u] = csum_col[rs, H_B + h:H_B + h + 1]
                b_row[u] = csum_row[H_B + h:H_B + h + 1, rs]
                b_l[u] = csum_col[(c + 1) * lc - 1:(c + 1) * lc, H_B + h:H_B + h + 1]

    rep = lambda col: jnp.broadcast_to(col, (lc, DK_B))
    bc = {u: rep(b_col[u]) for u in units}
    lic = {u: rep(li_col[u]) for u in units}

    last_max = {u: jnp.max(b_l[u] - b_row[u] + li_row[u], axis=-1, keepdims=True) for u in units}
    m_prev, m_new = {}, {}
    for b in range(nb):
        for h in range(H_B):
            m_p = m_scr[b, h][:, 0:1]
            for c in range(nch):
                u = (b, c, h)
                m_prev[u] = m_p
                m_p = jnp.maximum(b_l[u] + m_p, last_max[u])
                m_new[u] = m_p
            m_scr[b, h] = jnp.broadcast_to(m_p, (1, 128))

    dmat = {u: jnp.where(causal, bc[u][:, :lc] - b_row[u] + li_row[u], -jnp.inf) for u in units}
    row_max = {u: rep(jnp.max(dmat[u], axis=-1, keepdims=True)) for u in units}
    m_t, g_inter, e_mat = {}, {}, {}
    for u in units:
        inter = bc[u] + m_prev[u]
        m_t[u] = jnp.maximum(inter, row_max[u])
        g_inter[u] = jnp.exp(inter - m_t[u])
        e_mat[u] = jnp.exp(dmat[u] - m_t[u][:, :lc])

    qk = {u: lax.dot_general(q_u[u].astype(BF16), k_u[u].astype(BF16), nt_dims, preferred_element_type=F32)
          for u in units}
    s_mat = {u: qk[u] * e_mat[u] for u in units}
    sv = {u: _bdot(s_mat[u], v_u[u]) for u in units}
    s_sum = {u: rep(jnp.sum(s_mat[u], axis=-1, keepdims=True)) for u in units}
    kw = {u: k_u[u] * jnp.exp(b_l[u] - bc[u] + lic[u] - m_new[u]) for u in units}
    kwv = {u: lax.dot_general(kw[u].astype(BF16), v_u[u].astype(BF16), tn_dims, preferred_element_type=F32)
           for u in units}

    q_c, qn_prod = {}, {}
    c_cur = {(b, h): c_scr[b, h] for b in range(nb) for h in range(H_B)}
    n_cur = {(b, h): n_scr[b, h] for b in range(nb) for h in range(H_B)}
    for c in range(nch):
        for b in range(nb):
            for h in range(H_B):
                u = (b, c, h)
                q_c[u] = _bdot(q_u[u], c_cur[(b, h)])
                qn_prod[u] = q_u[u] * n_cur[(b, h)]
                dec = jnp.exp(b_l[u] + m_prev[u] - m_new[u])
                c_cur[(b, h)] = dec * c_cur[(b, h)] + kwv[u]
                n_cur[(b, h)] = dec * n_cur[(b, h)] + jnp.sum(kw[u], axis=0, keepdims=True)
    for b in range(nb):
        for h in range(H_B):
            c_scr[b, h] = c_cur[(b, h)]
            n_scr[b, h] = n_cur[(b, h)]
    q_n = {u: rep(jnp.sum(qn_prod[u], axis=-1, keepdims=True)) for u in units}

    hh = {}
    for u in units:
        num = g_inter[u] * q_c[u] + sv[u]
        den = g_inter[u] * q_n[u] + s_sum[u]
        hh[u] = num / jnp.maximum(jnp.abs(den), jnp.exp(-m_t[u]))
    mu = {u: rep(jnp.sum(hh[u], axis=-1, keepdims=True)) * (1.0 / DK_B) for u in units}
    dev = {u: hh[u] - mu[u] for u in units}
    var = {u: rep(jnp.sum(dev[u] * dev[u], axis=-1, keepdims=True)) * (1.0 / DK_B) for u in units}
    for u in units:
        b, c, h = u
        hs = slice(h * DK_B, (h + 1) * DK_B)
        hn = dev[u] * lax.rsqrt(var[u] + MH_EPS) * mhg_ref[:, hs]
        yb_ref[b, c * lc:(c + 1) * lc, hs] = _sigmoid(o_u[u]) * hn

    @pl.when(j == pl.num_programs(1) - 1)
    def _():
        ct_ref[...] = c_scr[...]
        nt_ref[...] = n_scr[...]
        mt_ref[...] = m_scr[:, :, :, 0:1]
        co_ref[...] = x_scr[:, 5:8, :]


def _mlstm(p3, conv_prev, c_all, l, c_out_prev, n0, m0, wts, nb, tb, lc):
    bsz, t, _ = p3.shape
    full = lambda shape: pl.BlockSpec(shape, lambda i, j: (0,) * len(shape))
    n_alias = 0 if c_out_prev is None else 1
    kern = functools.partial(_mlstm_kernel, nb=nb, tb=tb, lc=lc, n_alias=n_alias)
    gate_blk = RWKV_COLS // GATE_COLS
    c_spec = pl.BlockSpec((None, nb, H_B, DK_B, DK_B), lambda i, j: (l, i, 0, 0, 0))
    in_specs = [
        pl.BlockSpec((nb, tb, ML_COLS), lambda i, j: (i, j, 1)),
        pl.BlockSpec((nb, tb, GATE_COLS), lambda i, j: (i, j, gate_blk)),
        pl.BlockSpec((nb, CONV_W - 1, 2 * D_B), lambda i, j: (i, 0, 0)),
        c_spec,
        pl.BlockSpec((nb, H_B, 1, DK_B), lambda i, j: (i, 0, 0, 0)),
        pl.BlockSpec((nb, H_B, 1, 1), lambda i, j: (i, 0, 0, 0)),
        full((CONV_W, 2 * D_B)), full((1, 2 * D_B)), full((1, 128)), full((1, D_B)),
    ]
    args = [p3, p3, conv_prev, c_all, n0, m0, *wts]
    aliases = {}
    if n_alias:
        in_specs.append(pl.BlockSpec(memory_space=pl.ANY))
        args.append(c_out_prev)
        aliases = {len(args) - 1: 2}
    return pl.pallas_call(
        kern,
        grid=(bsz // nb, t // tb),
        in_specs=in_specs,
        out_specs=[
            pl.BlockSpec((nb, tb, D_B), lambda i, j: (i, j, 0)),
            pl.BlockSpec((nb, CONV_W - 1, 2 * D_B), lambda i, j: (i, 0, 0)),
            c_spec,
            pl.BlockSpec((nb, H_B, 1, DK_B), lambda i, j: (i, 0, 0, 0)),
            pl.BlockSpec((nb, H_B, 1, 1), lambda i, j: (i, 0, 0, 0)),
        ],
        out_shape=[
            jax.ShapeDtypeStruct((bsz, t, D_B), F32),
            jax.ShapeDtypeStruct((bsz, CONV_W - 1, 2 * D_B), F32),
            jax.ShapeDtypeStruct((DEPTH, bsz, H_B, DK_B, DK_B), F32),
            jax.ShapeDtypeStruct((bsz, H_B, 1, DK_B), F32),
            jax.ShapeDtypeStruct((bsz, H_B, 1, 1), F32),
        ],
        scratch_shapes=[
            pltpu.VMEM((nb, tb + 8, 2 * D_B), F32),
            pltpu.VMEM((nb, H_B, DK_B, DK_B), F32),
            pltpu.VMEM((nb, H_B, 1, DK_B), F32),
            pltpu.VMEM((nb, H_B, 1, 128), F32),
        ],
        input_output_aliases=aliases,
        compiler_params=pltpu.CompilerParams(
            dimension_semantics=("parallel", "arbitrary"), vmem_limit_bytes=VMEM_LIMIT),
        name="mlstm",
    )(*args)


def _layer_weights(l, w):
    bf = lambda a: a.astype(BF16)
    row = lambda a: a.reshape(1, -1)
    w_in = w['w_in'][l]
    w_gates = jnp.pad(w_in[:, RWKV_COLS + ML_COLS:], ((0, 0), (0, GATE_COLS - 2 * H_B)))
    w_cat = jnp.concatenate([w_in[:, :RWKV_COLS], w_gates, w_in[:, RWKV_COLS:RWKV_COLS + ML_COLS]], axis=1)
    zero = jnp.zeros((R_W, D_A), F32)
    w2a = jnp.concatenate([jnp.concatenate([w['w2'][l], zero], axis=1),
                           jnp.concatenate([zero, w['a2'][l]], axis=1)], axis=0)
    ids = jnp.arange(D_A // 2) // N_A
    seg = (ids[:, None] == ids[None, :]).astype(BF16)
    gate_bias = jnp.pad(jnp.concatenate([w['i_bias'][l], w['f_bias'][l]]), (0, 128 - 2 * H_B)).reshape(1, 128)
    return dict(
        ln1=(row(w['ln1_g'][l]), row(w['ln1_b'][l])),
        ln3=(row(w['ln3_g'][l]), row(w['ln3_b'][l])),
        w_cat=bf(w_cat),
        rwkv=(row(w['mu_shift'][l]), row(w['w0'][l]), bf(w2a), row(w['a0'][l]), bf(w['g2'][l]), row(w['k_k'][l]),
              row(w['k_a'][l]), row(w['r_k'][l]), row(w['gn_g'][l]), row(w['gn_b'][l]), seg),
        mlstm=(w['conv_w'][l], row(w['conv_b'][l]), gate_bias, row(w['mh_g'][l])),
        out=(bf(w['w_out'][l][:D_A]), bf(w['w_out'][l][D_A:]), row(w['ln2_g'][l]), row(w['ln2_b'][l])),
    )


def _trunk(x, states, lw, ffn_w, nb_r, tb_r, nb_m, tb_m):
    bsz, t, _ = x.shape
    n = bsz * t
    tm = min(1024, n)
    lc = math.gcd(t, CHUNK)
    xf = x.reshape(n, D_MODEL)
    st_shift, st_wkv, st_conv, st_c, st_n, st_m = states
    new = [[] for _ in range(4)]
    wkv_out = jnp.zeros((DEPTH, bsz, H_A, N_A, N_A), F32)
    c_out = jnp.zeros((DEPTH, bsz, H_B, DK_B, DK_B), F32)
    for l in range(DEPTH):
        wl = lw[l]
        x1 = _ffn_ln(xf, *ffn_w[0], *wl['ln1'], l=l, tm=tm)
        p3 = _in_proj(x1, wl['w_cat'], tm=min(tm, 512)).reshape(bsz, t, P_COLS)
        ya, shift, wkv_out = _rwkv(p3, st_shift[l].reshape(bsz, 1, RWKV_COLS), st_wkv, l, wkv_out,
                                   wl['rwkv'], nb_r, tb_r)
        yb, conv, c_out, n_t, m_t = _mlstm(p3, st_conv[l], st_c, l, c_out,
                                           st_n[l].reshape(bsz, H_B, 1, DK_B), st_m[l].reshape(bsz, H_B, 1, 1),
                                           wl['mlstm'], nb_m, tb_m, lc)
        xf = _ffn_ln(x1, *ffn_w[1], *wl['ln3'], l=l, tm=tm,
                     mix_in=(ya.reshape(n, D_A), yb.reshape(n, D_B), *wl['out']))
        for idx, s in enumerate((shift.reshape(bsz, RWKV_COLS), conv, n_t.reshape(bsz, H_B, DK_B),
                                 m_t.reshape(bsz, H_B))):
            new[idx].append(s)
    shift_o, conv_o, n_o, m_o = [jnp.stack(s) for s in new]
    return xf.reshape(bsz, t, D_MODEL), [shift_o, wkv_out, conv_o, c_out, n_o, m_o]


def kernel(x_prompt, x_sample, state_shift, state_wkv, state_conv, state_C, state_n, state_m,
           ffn1_wg, ffn1_wu, ffn1_wd, ln1_g, ln1_b, w_in, mu_shift, w0, w2, a0, a2, g2, k_k, k_a, r_k,
           gn_g, gn_b, conv_w, conv_b, i_bias, f_bias, mh_g, w_out, ln2_g, ln2_b,
           ffn2_wg, ffn2_wu, ffn2_wd, ln3_g, ln3_b):
    w = dict(ffn1_wg=ffn1_wg, ffn1_wu=ffn1_wu, ffn1_wd=ffn1_wd, ln1_g=ln1_g, ln1_b=ln1_b, w_in=w_in,
             mu_shift=mu_shift, w0=w0, w2=w2, a0=a0, a2=a2, g2=g2, k_k=k_k, k_a=k_a, r_k=r_k,
             gn_g=gn_g, gn_b=gn_b, conv_w=conv_w, conv_b=conv_b, i_bias=i_bias, f_bias=f_bias,
             mh_g=mh_g, w_out=w_out, ln2_g=ln2_g, ln2_b=ln2_b, ffn2_wg=ffn2_wg, ffn2_wu=ffn2_wu,
             ffn2_wd=ffn2_wd, ln3_g=ln3_g, ln3_b=ln3_b)
    lw = [_layer_weights(l, w) for l in range(DEPTH)]
    def col_blocks(wm):
        return wm.astype(BF16).reshape(DEPTH, D_MODEL, D_FF // FFN_COL_BLOCK, FFN_COL_BLOCK).transpose(0, 2, 1, 3)

    ffn_w = ((col_blocks(ffn1_wg), col_blocks(ffn1_wu), ffn1_wd.astype(BF16)),
             (col_blocks(ffn2_wg), col_blocks(ffn2_wu), ffn2_wd.astype(BF16)))
    bp = x_prompt.shape[0]
    dt = x_prompt.dtype
    init = [jnp.zeros((DEPTH, bp, RWKV_COLS), dt),
            jnp.zeros((DEPTH, bp, H_A, N_A, N_A), dt),
            jnp.zeros((DEPTH, bp, CONV_W - 1, 2 * D_B), dt),
            jnp.zeros((DEPTH, bp, H_B, DK_B, DK_B), dt),
            jnp.zeros((DEPTH, bp, H_B, DK_B), dt),
            jnp.zeros((DEPTH, bp, H_B), dt)]
    y_p, ps = _trunk(x_prompt, init, lw, ffn_w, nb_r=8, tb_r=64, nb_m=1, tb_m=256)
    y_s, ss = _trunk(x_sample, [state_shift, state_wkv, state_conv, state_C, state_n, state_m], lw, ffn_w,
                     nb_r=8, tb_r=8, nb_m=8, tb_m=8)
    return (y_p, y_s, *ps, *ss)
```

```python
import functools
import math

import jax
import jax.numpy as jnp
from jax import lax
from jax.experimental import pallas as pl
from jax.experimental.pallas import tpu as pltpu

D_MODEL = 1024
DEPTH = 2
D_A = 512
N_A = 64
H_A = 8
D_B = 512
H_B = 4
DK_B = 128
R_W = 64
R_A = 64
R_G = 128
RWKV_COLS = 3 * D_A + R_W + R_A + R_G
CONV_W = 4
CHUNK = 64
D_FF = 2816
ALPHA = (2.0 * DEPTH) ** 0.25
LN_EPS = 1e-5
GN_EPS = 64e-5
MH_EPS = 1e-6

GATE_COLS = 256
ML_COLS = 4 * D_B
P_COLS = RWKV_COLS + GATE_COLS + ML_COLS
VMEM_LIMIT = 56 * 1024 * 1024
FFN_ROW_CHUNK = 256
FFN_COL_BLOCK = 256
FFN_MID_CHUNK = 512
FFN_TILE_ROWS = 2048
MIX_TILE_ROWS = 1024
PROJ_TILE_ROWS = 512

F32 = jnp.float32
BF16 = jnp.bfloat16


def _bdot(a, b):
    return jnp.dot(a.astype(BF16), b.astype(BF16), preferred_element_type=F32)


def _seg_dot(x_bf16, seg_half):
    h = seg_half.shape[0]
    return jnp.concatenate([jnp.dot(x_bf16[:, :h], seg_half, preferred_element_type=F32),
                            jnp.dot(x_bf16[:, h:], seg_half, preferred_element_type=F32)], axis=1)


def _seg_sum(x, seg_half):
    hi = x.astype(BF16)
    lo = (x - hi.astype(F32)).astype(BF16)
    return _seg_dot(hi, seg_half) + _seg_dot(lo, seg_half)


def _layer_norm_rows(y, g, b):
    mu = jnp.mean(y, axis=-1, keepdims=True)
    d = y - mu
    var = jnp.mean(d * d, axis=-1, keepdims=True)
    return d * lax.rsqrt(var + LN_EPS) * g + b


def _sigmoid(x):
    return 1.0 / (1.0 + jnp.exp(-x))


def _ffn_ln_kernel(*refs, mix, n_chunks):
    if mix:
        x_ref, ya_ref, yb_ref, wa_ref, wb_ref, g2_ref, b2_ref = refs[:7]
        wg_ref, wu_ref, wd_ref, g_ref, b_ref, o_ref, xb_scr, xin_scr = refs[7:]
    else:
        x_ref, wg_ref, wu_ref, wd_ref, g_ref, b_ref, o_ref, xb_scr = refs
    j = pl.program_id(1)
    last = pl.num_programs(1) - 1
    tm = x_ref.shape[0]
    rc = tm // n_chunks

    def partial_ffn(xb, wg, wu, wd):
        hg = jnp.dot(xb, wg, preferred_element_type=F32)
        hu = jnp.dot(xb, wu, preferred_element_type=F32)
        h = (hg * _sigmoid(hg)) * hu
        return jnp.dot(h.astype(BF16), wd, preferred_element_type=F32)

    def weights():
        return wg_ref[...].astype(BF16), wu_ref[...].astype(BF16), wd_ref[...].astype(BF16)

    @pl.when(j == 0)
    def _():
        wg, wu, wd = weights()
        chunks = [pl.ds(c * rc, rc) for c in range(n_chunks)]
        if mix:
            projs = [jnp.dot(ya_ref[rows, :].astype(BF16), wa_ref[...], preferred_element_type=F32)
                     + jnp.dot(yb_ref[rows, :].astype(BF16), wb_ref[...], preferred_element_type=F32)
                     for rows in chunks]
        xbs = []
        for c, rows in enumerate(chunks):
            if mix:
                xin = _layer_norm_rows(ALPHA * x_ref[rows, :] + projs[c], g2_ref[...], b2_ref[...])
                xin_scr[rows, :] = xin
            else:
                xin = x_ref[rows, :]
            xb = xin.astype(BF16)
            xb_scr[rows, :] = xb
            xbs.append(xb)
        for c, rows in enumerate(chunks):
            o_ref[rows, :] = partial_ffn(xbs[c], wg, wu, wd)

    @pl.when((j > 0) & (j < last))
    def _():
        wg, wu, wd = weights()
        for c in range(tm // FFN_MID_CHUNK):
            rows = pl.ds(c * FFN_MID_CHUNK, FFN_MID_CHUNK)
            o_ref[rows, :] += partial_ffn(xb_scr[rows, :], wg, wu, wd)

    @pl.when(j == last)
    def _():
        wg, wu, wd = weights()
        for c in range(n_chunks):
            rows = pl.ds(c * rc, rc)
            acc = o_ref[rows, :] + partial_ffn(xb_scr[rows, :], wg, wu, wd)
            xin = xin_scr[rows, :] if mix else x_ref[rows, :]
            o_ref[rows, :] = _layer_norm_rows(ALPHA * xin + 0.5 * acc, g_ref[...], b_ref[...])


def _ffn_ln(x, wg, wu, wd, g, b, l, tm, mix_in=None):
    n = x.shape[0]
    row_tile = lambda width: pl.BlockSpec((tm, width), lambda i, j: (i, 0))
    const = lambda shape: pl.BlockSpec(shape, lambda i, j: (0, 0))
    in_specs = [row_tile(D_MODEL)]
    args = [x]
    scratch = [pltpu.VMEM((tm, D_MODEL), BF16)]
    if mix_in is not None:
        in_specs += [row_tile(D_A), row_tile(D_B), const((D_A, D_MODEL)), const((D_B, D_MODEL)),
                     const((1, D_MODEL)), const((1, D_MODEL))]
        args += list(mix_in)
        scratch.append(pltpu.VMEM((tm, D_MODEL), F32))
    in_specs += [
        pl.BlockSpec((None, D_MODEL, FFN_COL_BLOCK), lambda i, j: (l, 0, j)),
        pl.BlockSpec((None, D_MODEL, FFN_COL_BLOCK), lambda i, j: (l, 0, j)),
        pl.BlockSpec((None, FFN_COL_BLOCK, D_MODEL), lambda i, j: (l, j, 0)),
        const((1, D_MODEL)), const((1, D_MODEL)),
    ]
    args += [wg, wu, wd, g, b]
    return pl.pallas_call(
        functools.partial(_ffn_ln_kernel, mix=mix_in is not None, n_chunks=tm // FFN_ROW_CHUNK),
        grid=(n // tm, D_FF // FFN_COL_BLOCK),
        in_specs=in_specs,
        out_specs=pl.BlockSpec((tm, D_MODEL), lambda i, j: (i, 0)),
        out_shape=jax.ShapeDtypeStruct((n, D_MODEL), F32),
        scratch_shapes=scratch,
        compiler_params=pltpu.CompilerParams(
            dimension_semantics=("parallel", "arbitrary"), vmem_limit_bytes=VMEM_LIMIT),
        name="ffn_ln",
    )(*args)


def _in_proj_kernel(x_ref, w_ref, o_ref):
    o_ref[...] = jnp.dot(x_ref[...].astype(BF16), w_ref[...], preferred_element_type=F32)


def _in_proj(x, w, tm):
    n = x.shape[0]
    return pl.pallas_call(
        _in_proj_kernel,
        grid=(n // tm,),
        in_specs=[
            pl.BlockSpec((tm, D_MODEL), lambda i: (i, 0)),
            pl.BlockSpec((D_MODEL, P_COLS), lambda i: (0, 0)),
        ],
        out_specs=pl.BlockSpec((tm, P_COLS), lambda i: (i, 0)),
        out_shape=jax.ShapeDtypeStruct((n, P_COLS), F32),
        compiler_params=pltpu.CompilerParams(
            dimension_semantics=("parallel",), vmem_limit_bytes=VMEM_LIMIT),
        name="in_proj",
    )(x, w)


def _rwkv_kernel(*refs, nb, tb, n_alias):
    (p_ref, sp_ref, s0_ref, mu_ref, w0_ref, w2a_ref, a0_ref, g2_ref, kk_ref, ka_ref, rk_ref,
     gng_ref, gnb_ref, seg_ref) = refs[:14]
    (ya_ref, so_ref, st_ref,
     s_scr, carry_scr, nk_scr, p2_scr, q1_scr, rp_scr, v_scr, vsw_scr, ya1_scr, b_scr, yc1_scr, k_scr, w12_scr,
     cbr_scr, ckr_scr, bonus_scr, g_scr, sa_scr, q_scr, yo_scr) = refs[14 + n_alias:]
    j = pl.program_id(1)
    n_hp = H_A // 2

    @pl.when(j == 0)
    def _():
        for b in range(nb):
            for hp in range(n_hp):
                s_scr[b, hp] = jnp.concatenate([s0_ref[b, 2 * hp], s0_ref[b, 2 * hp + 1]], axis=-1)
        carry_scr[...] = sp_ref[...]
        sa_scr[...] = jnp.zeros_like(sa_scr)
        q_scr[...] = jnp.zeros_like(q_scr)
        yo_scr[...] = jnp.zeros_like(yo_scr)

    seg = seg_ref[...]
    rows = nb * tb
    lane128 = lax.broadcasted_iota(jnp.int32, (rows, 128), 1)
    rowid = lax.broadcasted_iota(jnp.int32, (tb, RWKV_COLS), 0)
    rowid_a = lax.broadcasted_iota(jnp.int32, (tb, D_A), 0)

    pms, rps, r_last = [], [], []
    for b in range(nb):
        p = p_ref[b]
        prev = jnp.where(rowid == 0, carry_scr[b], pltpu.roll(p, 1, axis=0))
        carry_scr[b] = p[tb - 1:tb, :]
        pm_b = p + (prev - p) * mu_ref[...]
        r_b = pm_b[:, 0:D_A]
        pms.append(pm_b)
        rps.append(jnp.where(rowid_a == 0, 0.0, pltpu.roll(r_b, 1, axis=0)))
        r_last.append(r_b[tb - 1:tb, :])
    pm = jnp.concatenate(pms, axis=0)
    r = pm[:, 0:D_A]
    k = pm[:, D_A:2 * D_A]
    v = pm[:, 2 * D_A:3 * D_A]
    z = pm[:, 3 * D_A:3 * D_A + 128]
    xg = pm[:, 3 * D_A + 128:RWKV_COLS]
    zt = jnp.where(lane128 < R_W, jnp.tanh(z), z)
    lr = _bdot(zt, w2a_ref[...])
    g_scr[...] = _bdot(_sigmoid(xg), g2_ref[...]).reshape(nb, tb, D_A)
    kk = k * kk_ref[...]
    ss = _seg_sum(kk * kk, seg)
    w = jnp.exp(-math.exp(-0.5) * _sigmoid(w0_ref[...] + lr[:, :D_A]))
    a = _sigmoid(a0_ref[...] + lr[:, D_A:])
    kk = kk * lax.rsqrt(jnp.maximum(ss, 1e-24))
    kmod = k * (1.0 + (a - 1.0) * ka_ref[...])
    bonus_scr[...] = (_seg_sum(r * kmod * rk_ref[...], seg) * v).reshape(nb, tb, D_A)
    bb = kk * a
    nxt = lambda x: pltpu.roll(x, rows - 1, axis=0)
    kk_n, w_n, bb_n = nxt(kk), nxt(w), nxt(bb)
    cc = _seg_dot(jnp.concatenate([bb * kk_n, kmod * kk_n, bb * r, kmod * r], axis=0).astype(BF16), seg)
    cbk, ckk = cc[0:rows], cc[rows:2 * rows]
    cbr_scr[...] = cc[2 * rows:3 * rows].reshape(nb, tb, D_A)
    ckr_scr[...] = cc[3 * rows:4 * rows].reshape(nb, tb, D_A)
    def put(scr, arr):
        for hp in range(n_hp):
            scr[:, hp] = arr[:, hp * 128:(hp + 1) * 128].reshape(nb, tb, 128)

    put(nk_scr, -kk)
    put(p2_scr, -(w * kk_n))
    put(q1_scr, w * r)
    put(rp_scr, jnp.concatenate(rps, axis=0))
    put(ya1_scr, bb * w_n - cbk * bb_n)
    put(b_scr, bb)
    put(yc1_scr, kmod * w_n - ckk * bb_n)
    put(k_scr, kmod)
    put(w12_scr, w * w_n)
    put(v_scr, v)
    for hp in range(n_hp):
        vsw_scr[:, hp] = pltpu.roll(v[:, hp * 128:(hp + 1) * 128], 64, axis=1).reshape(nb, tb, 128)

    row8 = lax.broadcasted_iota(jnp.int32, (8, 128), 0)
    lane8 = lax.broadcasted_iota(jnp.int32, (8, 128), 1)
    pair8 = lax.shift_right_logical(row8, 1)
    half8 = ((((row8 & 1) == 0) & (lane8 < 64)) | (((row8 & 1) == 1) & (lane8 >= 64))).astype(F32)
    mk3 = ((row8 >= 6) & ((((row8 & 1) == 0) & (lane8 < 64)) | (((row8 & 1) == 1) & (lane8 >= 64)))).astype(F32)
    row64 = lax.broadcasted_iota(jnp.int32, (8, 64), 0)

    def pair_tile(q0, q1, q2, q3):
        return jnp.where(pair8 == 0, q0, jnp.where(pair8 == 1, q1, jnp.where(pair8 == 2, q2, q3))) * half8

    chains = [(b, hp) for b in range(nb) for hp in range(n_hp)]
    nt_dims = (((1,), (1,)), ((), ()))
    tn_dims = (((0,), (0,)), ((), ()))
    grp = min(16, tb)
    n_pair = grp // 2

    def natural_rows(rows_h0, rows_h1):
        pad = [jnp.zeros((8 - n_pair, N_A), F32)] if n_pair < 8 else []
        return jnp.concatenate([jnp.concatenate(rows_h0 + pad, axis=0),
                                jnp.concatenate(rows_h1 + pad, axis=0)], axis=1)

    def group(tg, carry):
        t0 = tg * grp

        def row(ref, c, t, n=8):
            return ref[c[0], c[1], pl.ds(t0 + t, n, stride=0), :]

        out = {c: [[] for _ in range(6)] for c in chains}
        for pi in range(n_pair):
            i = 2 * pi
            reds = {}
            for c in chains:
                a_mat = pair_tile(row(nk_scr, c, i), row(p2_scr, c, i), row(q1_scr, c, i), row(rp_scr, c, i))
                reds[c] = lax.dot_general(a_mat.astype(BF16), s_scr[c[0], c[1]].astype(BF16), nt_dims,
                                          preferred_element_type=F32)
            xs = {}
            for c in chains:
                x_mat = jnp.where(
                    row64 < 4, reds[c],
                    jnp.where(row64 == 4, row(v_scr, c, i)[:, :64],
                              jnp.where(row64 == 5, row(vsw_scr, c, i)[:, :64],
                                        jnp.where(row64 == 6, row(v_scr, c, i + 1)[:, :64],
                                                  row(vsw_scr, c, i + 1)[:, :64]))))
                xs[c] = x_mat.astype(BF16)
            for c in chains:
                y_mat = pair_tile(row(ya1_scr, c, i), row(b_scr, c, i + 1), row(yc1_scr, c, i), row(k_scr, c, i + 1))
                d_s = lax.dot_general(xs[c], y_mat.astype(BF16), tn_dims, preferred_element_type=F32)
                s_scr[c[0], c[1]] = s_scr[c[0], c[1]] * row(w12_scr, c, i, N_A) + d_s
                for slot, red_row in enumerate((0, 1, 4, 5, 6, 7)):
                    out[c][slot].append(reds[c][red_row:red_row + 1, :])
        even0 = pl.multiple_of(tg * grp, 8)
        for c in chains:
            o = out[c]
            sa_scr[c[0], c[1], pl.ds(even0, 8, stride=2), :] = natural_rows(o[0], o[1])
            q_scr[c[0], c[1], pl.ds(even0, 8, stride=2), :] = natural_rows(o[2], o[3])
            yo_scr[c[0], c[1], pl.ds(even0 + 8, 8, stride=2), :] = natural_rows(o[4], o[5])
        return carry

    lax.fori_loop(0, tb // grp, group, 0)

    for (b, hp) in chains:
        rl = r_last[b][:, hp * 128:(hp + 1) * 128]
        a_mat = rl * mk3
        red = lax.dot_general(a_mat.astype(BF16), s_scr[b, hp].astype(BF16), nt_dims, preferred_element_type=F32)
        y_last = jnp.concatenate([red[6:7, :], red[7:8, :]], axis=1)
        yo_scr[b, hp, pl.ds(tb + 8, 8), :] = jnp.broadcast_to(y_last, (8, 128))

    def slab(scr, first):
        return jnp.concatenate(
            [jnp.concatenate([scr[b, hp, pl.ds(first, tb), :] for hp in range(n_hp)], axis=1) for b in range(nb)],
            axis=0)

    y_even = (slab(q_scr, 0) + slab(sa_scr, 0) * cbr_scr[...].reshape(rows, D_A)
              + slab(v_scr, 0) * ckr_scr[...].reshape(rows, D_A))
    parity = lax.broadcasted_iota(jnp.int32, (rows, D_A), 0) & 1
    y = jnp.where(parity == 0, y_even, slab(yo_scr, 9))
    mu = _seg_sum(y, seg) * (1.0 / N_A)
    d = y - mu
    var = _seg_sum(d * d, seg) * (1.0 / N_A)
    yn = d * lax.rsqrt(var + GN_EPS) * gng_ref[...] + gnb_ref[...]
    ya_ref[...] = (yn.reshape(nb, tb, D_A) + bonus_scr[...]) * g_scr[...]

    @pl.when(j == pl.num_programs(1) - 1)
    def _():
        for b in range(nb):
            for hp in range(n_hp):
                s_pair = s_scr[b, hp]
                st_ref[b, 2 * hp] = s_pair[:, :N_A]
                st_ref[b, 2 * hp + 1] = s_pair[:, N_A:]
        so_ref[...] = carry_scr[...]


def _rwkv(p3, shift_prev, wkv_all, l, wkv_out_prev, wts, nb, tb):
    bsz, t, _ = p3.shape
    n_hp = H_A // 2
    full = lambda shape: pl.BlockSpec(shape, lambda i, j: (0,) * len(shape))
    blk = lambda: pltpu.VMEM((nb, tb, D_A), F32)
    n_alias = 0 if wkv_out_prev is None else 1
    kern = functools.partial(_rwkv_kernel, nb=nb, tb=tb, n_alias=n_alias)
    state_spec = pl.BlockSpec((None, nb, H_A, N_A, N_A), lambda i, j: (l, i, 0, 0, 0))
    in_specs = [
        pl.BlockSpec((nb, tb, RWKV_COLS), lambda i, j: (i, j, 0)),
        pl.BlockSpec((nb, 1, RWKV_COLS), lambda i, j: (i, 0, 0)),
        state_spec,
        full((1, RWKV_COLS)), full((1, D_A)), full((128, 2 * D_A)), full((1, D_A)), full((R_G, D_A)),
        full((1, D_A)), full((1, D_A)), full((1, D_A)), full((1, D_A)), full((1, D_A)), full((D_A // 2, D_A // 2)),
    ]
    args = [p3, shift_prev, wkv_all, *wts]
    aliases = {}
    if n_alias:
        in_specs.append(pl.BlockSpec(memory_space=pl.ANY))
        args.append(wkv_out_prev)
        aliases = {len(args) - 1: 2}
    return pl.pallas_call(
        kern,
        grid=(bsz // nb, t // tb),
        in_specs=in_specs,
        out_specs=[
            pl.BlockSpec((nb, tb, D_A), lambda i, j: (i, j, 0)),
            pl.BlockSpec((nb, 1, RWKV_COLS), lambda i, j: (i, 0, 0)),
            state_spec,
        ],
        out_shape=[
            jax.ShapeDtypeStruct((bsz, t, D_A), F32),
            jax.ShapeDtypeStruct((bsz, 1, RWKV_COLS), F32),
            jax.ShapeDtypeStruct((DEPTH, bsz, H_A, N_A, N_A), F32),
        ],
        scratch_shapes=[
            pltpu.VMEM((nb, n_hp, N_A, 128), F32),
            pltpu.VMEM((nb, 1, RWKV_COLS), F32),
            *[pltpu.VMEM((nb, n_hp, tb, 2 * N_A), F32) for _ in range(11)],
            *[blk() for _ in range(4)],
            *[pltpu.VMEM((nb, n_hp, tb + 24, 2 * N_A), F32) for _ in range(3)],
        ],
        input_output_aliases=aliases,
        compiler_params=pltpu.CompilerParams(
            dimension_semantics=("parallel", "arbitrary"), vmem_limit_bytes=VMEM_LIMIT),
        name="rwkv7",
    )(*args)


def _mlstm_kernel(*refs, nb, tb, lc, n_alias):
    (pm_ref, pg_ref, cp_ref, c0_ref, n0_ref, m0_ref, cw_ref, cb_ref, gb_ref, mhg_ref) = refs[:10]
    (yb_ref, co_ref, ct_ref, nt_ref, mt_ref, x_scr, c_scr, n_scr, m_scr) = refs[10 + n_alias:]
    j = pl.program_id(1)
    hi = lax.Precision.HIGHEST
    nch = tb // lc
    lc_shift = lc.bit_length() - 1

    @pl.when(j == 0)
    def _():
        c_scr[...] = c0_ref[...]
        n_scr[...] = n0_ref[...]
        m_scr[...] = jnp.broadcast_to(m0_ref[...], m_scr.shape)
        x_scr[:, 5:8, :] = cp_ref[...]

    rr = lax.broadcasted_iota(jnp.int32, (lc, lc), 0)
    cc = lax.broadcasted_iota(jnp.int32, (lc, lc), 1)
    causal = rr >= cc
    rb = lax.broadcasted_iota(jnp.int32, (tb, tb), 0)
    cb = lax.broadcasted_iota(jnp.int32, (tb, tb), 1)
    tri_blk = ((rb >= cb) & (lax.shift_right_logical(rb, lc_shift) == lax.shift_right_logical(cb, lc_shift))
               ).astype(F32)
    lane_g = lax.broadcasted_iota(jnp.int32, (tb, 128), 1)
    tn_dims = (((0,), (0,)), ((), ()))
    nt_dims = (((1,), (1,)), ((), ()))

    units = [(b, c, h) for b in range(nb) for c in range(nch) for h in range(H_B)]
    q_u, k_u, v_u, o_u = {}, {}, {}, {}
    li_col, li_row, b_col, b_row, b_l = {}, {}, {}, {}, {}

    for b in range(nb):
        x = pm_ref[b, :, 0:2 * D_B]
        x_scr[b, pl.ds(8, tb), :] = x
        conv = cb_ref[...] + x * cw_ref[3:4, :]
        for s in range(1, CONV_W):
            conv = conv + x_scr[b, pl.ds(8 - s, tb), :] * cw_ref[3 - s:4 - s, :]
        x_scr[b, 5:8, :] = x_scr[b, pl.ds(8 + tb - 3, 3), :]
        sc = conv * _sigmoid(conv)
        q_all = sc[:, :D_B]
        k_all = sc[:, D_B:] * (DK_B ** -0.5)
        v_all = pm_ref[b, :, 2 * D_B:3 * D_B]
        o_all = pm_ref[b, :, 3 * D_B:4 * D_B]
        gp = pg_ref[b, :, 0:128] + gb_ref[...]
        gates = jnp.where(lane_g < H_B, gp, jnp.minimum(gp, 0.0) - jnp.log1p(jnp.exp(-jnp.abs(gp))))
        csum_col = jnp.dot(tri_blk, gates, precision=hi, preferred_element_type=F32)
        gt = gates.T
        csum_row = lax.dot_general(gt, tri_blk, nt_dims, precision=hi, preferred_element_type=F32)
        for c in range(nch):
            rs = slice(c * lc, (c + 1) * lc)
            for h in range(H_B):
                u = (b, c, h)
                hs = slice(h * DK_B, (h + 1) * DK_B)
                q_u[u], k_u[u], v_u[u], o_u[u] = q_all[rs, hs], k_all[rs, hs], v_all[rs, hs], o_all[rs, hs]
                li_col[u] = gates[rs, h:h + 1]
                li_row[u] = gt[h:h + 1, rs]
                b_col[u] = csum_col[rs, H_B + h:H_B + h + 1]
                b_row[u] = csum_row[H_B + h:H_B + h + 1, rs]
                b_l[u] = csum_col[(c + 1) * lc - 1:(c + 1) * lc, H_B + h:H_B + h + 1]

    rep = lambda col: jnp.broadcast_to(col, (lc, DK_B))
    bc = {u: rep(b_col[u]) for u in units}
    lic = {u: rep(li_col[u]) for u in units}

    last_max = {u: jnp.max(b_l[u] - b_row[u] + li_row[u], axis=-1, keepdims=True) for u in units}
    m_prev, m_new = {}, {}
    for b in range(nb):
        for h in range(H_B):
            m_p = m_scr[b, h][:, 0:1]
            for c in range(nch):
                u = (b, c, h)
                m_prev[u] = m_p
                m_p = jnp.maximum(b_l[u] + m_p, last_max[u])
                m_new[u] = m_p
            m_scr[b, h] = jnp.broadcast_to(m_p, (1, 128))

    dmat = {u: jnp.where(causal, bc[u][:, :lc] - b_row[u] + li_row[u], -jnp.inf) for u in units}
    row_max = {u: rep(jnp.max(dmat[u], axis=-1, keepdims=True)) for u in units}
    m_t, g_inter, e_mat = {}, {}, {}
    for u in units:
        inter = bc[u] + m_prev[u]
        m_t[u] = jnp.maximum(inter, row_max[u])
        g_inter[u] = jnp.exp(inter - m_t[u])
        e_mat[u] = jnp.exp(dmat[u] - m_t[u][:, :lc])

    qk = {u: lax.dot_general(q_u[u].astype(BF16), k_u[u].astype(BF16), nt_dims, preferred_element_type=F32)
          for u in units}
    s_mat = {u: qk[u] * e_mat[u] for u in units}
    sv = {u: _bdot(s_mat[u], v_u[u]) for u in units}
    s_sum = {u: rep(jnp.sum(s_mat[u], axis=-1, keepdims=True)) for u in units}
    kw = {u: k_u[u] * jnp.exp(b_l[u] - bc[u] + lic[u] - m_new[u]) for u in units}
    kwv = {u: lax.dot_general(kw[u].astype(BF16), v_u[u].astype(BF16), tn_dims, preferred_element_type=F32)
           for u in units}

    q_c, qn_prod = {}, {}
    c_cur = {(b, h): c_scr[b, h] for b in range(nb) for h in range(H_B)}
    n_cur = {(b, h): n_scr[b, h] for b in range(nb) for h in range(H_B)}
    for c in range(nch):
        for b in range(nb):
            for h in range(H_B):
                u = (b, c, h)
                q_c[u] = _bdot(q_u[u], c_cur[(b, h)])
                qn_prod[u] = q_u[u] * n_cur[(b, h)]
                dec = jnp.exp(b_l[u] + m_prev[u] - m_new[u])
                c_cur[(b, h)] = dec * c_cur[(b, h)] + kwv[u]
                n_cur[(b, h)] = dec * n_cur[(b, h)] + jnp.sum(kw[u], axis=0, keepdims=True)
    for b in range(nb):
        for h in range(H_B):
            c_scr[b, h] = c_cur[(b, h)]
            n_scr[b, h] = n_cur[(b, h)]
    q_n = {u: rep(jnp.sum(qn_prod[u], axis=-1, keepdims=True)) for u in units}

    hh = {}
    for u in units:
        num = g_inter[u] * q_c[u] + sv[u]
        den = g_inter[u] * q_n[u] + s_sum[u]
        hh[u] = num / jnp.maximum(jnp.abs(den), jnp.exp(-m_t[u]))
    mu = {u: rep(jnp.sum(hh[u], axis=-1, keepdims=True)) * (1.0 / DK_B) for u in units}
    dev = {u: hh[u] - mu[u] for u in units}
    var = {u: rep(jnp.sum(dev[u] * dev[u], axis=-1, keepdims=True)) * (1.0 / DK_B) for u in units}
    for u in units:
        b, c, h = u
        hs = slice(h * DK_B, (h + 1) * DK_B)
        hn = dev[u] * lax.rsqrt(var[u] + MH_EPS) * mhg_ref[:, hs]
        yb_ref[b, c * lc:(c + 1) * lc, hs] = _sigmoid(o_u[u]) * hn

    @pl.when(j == pl.num_programs(1) - 1)
    def _():
        ct_ref[...] = c_scr[...]
        nt_ref[...] = n_scr[...]
        mt_ref[...] = m_scr[:, :, :, 0:1]
        co_ref[...] = x_scr[:, 5:8, :]


def _mlstm(p3, conv_prev, c_all, l, c_out_prev, n0, m0, wts, nb, tb, lc):
    bsz, t, _ = p3.shape
    full = lambda shape: pl.BlockSpec(shape, lambda i, j: (0,) * len(shape))
    n_alias = 0 if c_out_prev is None else 1
    kern = functools.partial(_mlstm_kernel, nb=nb, tb=tb, lc=lc, n_alias=n_alias)
    gate_blk = RWKV_COLS // GATE_COLS
    c_spec = pl.BlockSpec((None, nb, H_B, DK_B, DK_B), lambda i, j: (l, i, 0, 0, 0))
    in_specs = [
        pl.BlockSpec((nb, tb, ML_COLS), lambda i, j: (i, j, 1)),
        pl.BlockSpec((nb, tb, GATE_COLS), lambda i, j: (i, j, gate_blk)),
        pl.BlockSpec((nb, CONV_W - 1, 2 * D_B), lambda i, j: (i, 0, 0)),
        c_spec,
        pl.BlockSpec((nb, H_B, 1, DK_B), lambda i, j: (i, 0, 0, 0)),
        pl.BlockSpec((nb, H_B, 1, 1), lambda i, j: (i, 0, 0, 0)),
        full((CONV_W, 2 * D_B)), full((1, 2 * D_B)), full((1, 128)), full((1, D_B)),
    ]
    args = [p3, p3, conv_prev, c_all, n0, m0, *wts]
    aliases = {}
    if n_alias:
        in_specs.append(pl.BlockSpec(memory_space=pl.ANY))
        args.append(c_out_prev)
        aliases = {len(args) - 1: 2}
    return pl.pallas_call(
        kern,
        grid=(bsz // nb, t // tb),
        in_specs=in_specs,
        out_specs=[
            pl.BlockSpec((nb, tb, D_B), lambda i, j: (i, j, 0)),
            pl.BlockSpec((nb, CONV_W - 1, 2 * D_B), lambda i, j: (i, 0, 0)),
            c_spec,
            pl.BlockSpec((nb, H_B, 1, DK_B), lambda i, j: (i, 0, 0, 0)),
            pl.BlockSpec((nb, H_B, 1, 1), lambda i, j: (i, 0, 0, 0)),
        ],
        out_shape=[
            jax.ShapeDtypeStruct((bsz, t, D_B), F32),
            jax.ShapeDtypeStruct((bsz, CONV_W - 1, 2 * D_B), F32),
            jax.ShapeDtypeStruct((DEPTH, bsz, H_B, DK_B, DK_B), F32),
            jax.ShapeDtypeStruct((bsz, H_B, 1, DK_B), F32),
            jax.ShapeDtypeStruct((bsz, H_B, 1, 1), F32),
        ],
        scratch_shapes=[
            pltpu.VMEM((nb, tb + 8, 2 * D_B), F32),
            pltpu.VMEM((nb, H_B, DK_B, DK_B), F32),
            pltpu.VMEM((nb, H_B, 1, DK_B), F32),
            pltpu.VMEM((nb, H_B, 1, 128), F32),
        ],
        input_output_aliases=aliases,
        compiler_params=pltpu.CompilerParams(
            dimension_semantics=("parallel", "arbitrary"), vmem_limit_bytes=VMEM_LIMIT),
        name="mlstm",
    )(*args)


def _layer_weights(l, w):
    bf = lambda a: a.astype(BF16)
    row = lambda a: a.reshape(1, -1)
    w_in = w['w_in'][l]
    w_gates = jnp.pad(w_in[:, RWKV_COLS + ML_COLS:], ((0, 0), (0, GATE_COLS - 2 * H_B)))
    w_cat = jnp.concatenate([w_in[:, :RWKV_COLS], w_gates, w_in[:, RWKV_COLS:RWKV_COLS + ML_COLS]], axis=1)
    zero = jnp.zeros((R_W, D_A), F32)
    w2a = jnp.concatenate([jnp.concatenate([w['w2'][l], zero], axis=1),
                           jnp.concatenate([zero, w['a2'][l]], axis=1)], axis=0)
    ids = jnp.arange(D_A // 2) // N_A
    seg = (ids[:, None] == ids[None, :]).astype(BF16)
    gate_bias = jnp.pad(jnp.concatenate([w['i_bias'][l], w['f_bias'][l]]), (0, 128 - 2 * H_B)).reshape(1, 128)
    return dict(
        ln1=(row(w['ln1_g'][l]), row(w['ln1_b'][l])),
        ln3=(row(w['ln3_g'][l]), row(w['ln3_b'][l])),
        w_cat=bf(w_cat),
        rwkv=(row(w['mu_shift'][l]), row(w['w0'][l]), bf(w2a), row(w['a0'][l]), bf(w['g2'][l]), row(w['k_k'][l]),
              row(w['k_a'][l]), row(w['r_k'][l]), row(w['gn_g'][l]), row(w['gn_b'][l]), seg),
        mlstm=(w['conv_w'][l], row(w['conv_b'][l]), gate_bias, row(w['mh_g'][l])),
        out=(bf(w['w_out'][l][:D_A]), bf(w['w_out'][l][D_A:]), row(w['ln2_g'][l]), row(w['ln2_b'][l])),
    )


def _trunk(x, states, lw, ffn_w, nb_r, tb_r, nb_m, tb_m):
    bsz, t, _ = x.shape
    n = bsz * t
    tm_ffn, tm_mix, tm_proj = min(FFN_TILE_ROWS, n), min(MIX_TILE_ROWS, n), min(PROJ_TILE_ROWS, n)
    lc = math.gcd(t, CHUNK)
    xf = x.reshape(n, D_MODEL)
    st_shift, st_wkv, st_conv, st_c, st_n, st_m = states
    new = [[] for _ in range(4)]
    wkv_out = jnp.zeros((DEPTH, bsz, H_A, N_A, N_A), F32)
    c_out = jnp.zeros((DEPTH, bsz, H_B, DK_B, DK_B), F32)
    for l in range(DEPTH):
        wl = lw[l]
        x1 = _ffn_ln(xf, *ffn_w[0], *wl['ln1'], l=l, tm=tm_ffn)
        p3 = _in_proj(x1, wl['w_cat'], tm=tm_proj).reshape(bsz, t, P_COLS)
        ya, shift, wkv_out = _rwkv(p3, st_shift[l].reshape(bsz, 1, RWKV_COLS), st_wkv, l, wkv_out,
                                   wl['rwkv'], nb_r, tb_r)
        yb, conv, c_out, n_t, m_t = _mlstm(p3, st_conv[l], st_c, l, c_out,
                                           st_n[l].reshape(bsz, H_B, 1, DK_B), st_m[l].reshape(bsz, H_B, 1, 1),
                                           wl['mlstm'], nb_m, tb_m, lc)
        xf = _ffn_ln(x1, *ffn_w[1], *wl['ln3'], l=l, tm=tm_mix,
                     mix_in=(ya.reshape(n, D_A), yb.reshape(n, D_B), *wl['out']))
        for idx, s in enumerate((shift.reshape(bsz, RWKV_COLS), conv, n_t.reshape(bsz, H_B, DK_B),
                                 m_t.reshape(bsz, H_B))):
            new[idx].append(s)
    shift_o, conv_o, n_o, m_o = [jnp.stack(s) for s in new]
    return xf.reshape(bsz, t, D_MODEL), [shift_o, wkv_out, conv_o, c_out, n_o, m_o]


def kernel(x_prompt, x_sample, state_shift, state_wkv, state_conv, state_C, state_n, state_m,
           ffn1_wg, ffn1_wu, ffn1_wd, ln1_g, ln1_b, w_in, mu_shift, w0, w2, a0, a2, g2, k_k, k_a, r_k,
           gn_g, gn_b, conv_w, conv_b, i_bias, f_bias, mh_g, w_out, ln2_g, ln2_b,
           ffn2_wg, ffn2_wu, ffn2_wd, ln3_g, ln3_b):
    w = dict(ffn1_wg=ffn1_wg, ffn1_wu=ffn1_wu, ffn1_wd=ffn1_wd, ln1_g=ln1_g, ln1_b=ln1_b, w_in=w_in,
             mu_shift=mu_shift, w0=w0, w2=w2, a0=a0, a2=a2, g2=g2, k_k=k_k, k_a=k_a, r_k=r_k,
             gn_g=gn_g, gn_b=gn_b, conv_w=conv_w, conv_b=conv_b, i_bias=i_bias, f_bias=f_bias,
             mh_g=mh_g, w_out=w_out, ln2_g=ln2_g, ln2_b=ln2_b, ffn2_wg=ffn2_wg, ffn2_wu=ffn2_wu,
             ffn2_wd=ffn2_wd, ln3_g=ln3_g, ln3_b=ln3_b)
    lw = [_layer_weights(l, w) for l in range(DEPTH)]
    ffn_w = ((ffn1_wg, ffn1_wu, ffn1_wd), (ffn2_wg, ffn2_wu, ffn2_wd))
    bp = x_prompt.shape[0]
    dt = x_prompt.dtype
    init = [jnp.zeros((DEPTH, bp, RWKV_COLS), dt),
            jnp.zeros((DEPTH, bp, H_A, N_A, N_A), dt),
            jnp.zeros((DEPTH, bp, CONV_W - 1, 2 * D_B), dt),
            jnp.zeros((DEPTH, bp, H_B, DK_B, DK_B), dt),
            jnp.zeros((DEPTH, bp, H_B, DK_B), dt),
            jnp.zeros((DEPTH, bp, H_B), dt)]
    y_p, ps = _trunk(x_prompt, init, lw, ffn_w, nb_r=8, tb_r=64, nb_m=1, tb_m=256)
    y_s, ss = _trunk(x_sample, [state_shift, state_wkv, state_conv, state_C, state_n, state_m], lw, ffn_w,
                     nb_r=8, tb_r=8, nb_m=8, tb_m=8)
    return (y_p, y_s, *ps, *ss)
```

```python
import functools
import math

import jax
import jax.numpy as jnp
from jax import lax
from jax.experimental import pallas as pl
from jax.experimental.pallas import tpu as pltpu

D_MODEL = 1024
DEPTH = 2
D_A = 512
N_A = 64
H_A = 8
D_B = 512
H_B = 4
DK_B = 128
R_W = 64
R_A = 64
R_G = 128
RWKV_COLS = 3 * D_A + R_W + R_A + R_G
CONV_W = 4
CHUNK = 64
D_FF = 2816
ALPHA = (2.0 * DEPTH) ** 0.25
LN_EPS = 1e-5
GN_EPS = 64e-5
MH_EPS = 1e-6

GATE_COLS = 256
ML_COLS = 4 * D_B
P_COLS = RWKV_COLS + GATE_COLS + ML_COLS
VMEM_LIMIT = 56 * 1024 * 1024
FFN_ROW_CHUNK = 256
FFN_COL_BLOCK = 256
FFN_MID_CHUNK = 512
FFN_TILE_ROWS = 2048
OUT_PROJ_TILE_ROWS = 2048
PROJ_TILE_ROWS = 512

F32 = jnp.float32
BF16 = jnp.bfloat16


def _bdot(a, b):
    return jnp.dot(a.astype(BF16), b.astype(BF16), preferred_element_type=F32)


def _seg_dot(x_bf16, seg_half):
    h = seg_half.shape[0]
    return jnp.concatenate([jnp.dot(x_bf16[:, :h], seg_half, preferred_element_type=F32),
                            jnp.dot(x_bf16[:, h:], seg_half, preferred_element_type=F32)], axis=1)


def _seg_sum(x, seg_half):
    hi = x.astype(BF16)
    lo = (x - hi.astype(F32)).astype(BF16)
    return _seg_dot(hi, seg_half) + _seg_dot(lo, seg_half)


def _layer_norm_rows(y, g, b):
    mu = jnp.mean(y, axis=-1, keepdims=True)
    d = y - mu
    var = jnp.mean(d * d, axis=-1, keepdims=True)
    return d * lax.rsqrt(var + LN_EPS) * g + b


def _sigmoid(x):
    return 1.0 / (1.0 + jnp.exp(-x))


def _ffn_ln_kernel(x_ref, wg_ref, wu_ref, wd_ref, g_ref, b_ref, o_ref, xb_scr, *, n_chunks):
    j = pl.program_id(1)
    last = pl.num_programs(1) - 1
    tm = x_ref.shape[0]
    rc = tm // n_chunks

    def partial_ffn(xb, wg, wu, wd):
        hg = jnp.dot(xb, wg, preferred_element_type=F32)
        hu = jnp.dot(xb, wu, preferred_element_type=F32)
        h = (hg * _sigmoid(hg)) * hu
        return jnp.dot(h.astype(BF16), wd, preferred_element_type=F32)

    def weights():
        return wg_ref[...].astype(BF16), wu_ref[...].astype(BF16), wd_ref[...].astype(BF16)

    @pl.when(j == 0)
    def _():
        wg, wu, wd = weights()
        for c in range(n_chunks):
            rows = pl.ds(c * rc, rc)
            xb = x_ref[rows, :].astype(BF16)
            xb_scr[rows, :] = xb
            o_ref[rows, :] = partial_ffn(xb, wg, wu, wd)

    @pl.when((j > 0) & (j < last))
    def _():
        wg, wu, wd = weights()
        for c in range(tm // FFN_MID_CHUNK):
            rows = pl.ds(c * FFN_MID_CHUNK, FFN_MID_CHUNK)
            o_ref[rows, :] += partial_ffn(xb_scr[rows, :], wg, wu, wd)

    @pl.when(j == last)
    def _():
        wg, wu, wd = weights()
        for c in range(n_chunks):
            rows = pl.ds(c * rc, rc)
            acc = o_ref[rows, :] + partial_ffn(xb_scr[rows, :], wg, wu, wd)
            o_ref[rows, :] = _layer_norm_rows(ALPHA * x_ref[rows, :] + 0.5 * acc, g_ref[...], b_ref[...])


def _ffn_ln(x, wg, wu, wd, g, b, l, tm):
    n = x.shape[0]
    const = lambda shape: pl.BlockSpec(shape, lambda i, j: (0, 0))
    return pl.pallas_call(
        functools.partial(_ffn_ln_kernel, n_chunks=tm // FFN_ROW_CHUNK),
        grid=(n // tm, D_FF // FFN_COL_BLOCK),
        in_specs=[
            pl.BlockSpec((tm, D_MODEL), lambda i, j: (i, 0)),
            pl.BlockSpec((None, D_MODEL, FFN_COL_BLOCK), lambda i, j: (l, 0, j)),
            pl.BlockSpec((None, D_MODEL, FFN_COL_BLOCK), lambda i, j: (l, 0, j)),
            pl.BlockSpec((None, FFN_COL_BLOCK, D_MODEL), lambda i, j: (l, j, 0)),
            const((1, D_MODEL)), const((1, D_MODEL)),
        ],
        out_specs=pl.BlockSpec((tm, D_MODEL), lambda i, j: (i, 0)),
        out_shape=jax.ShapeDtypeStruct((n, D_MODEL), F32),
        scratch_shapes=[pltpu.VMEM((tm, D_MODEL), BF16)],
        compiler_params=pltpu.CompilerParams(
            dimension_semantics=("parallel", "arbitrary"), vmem_limit_bytes=VMEM_LIMIT),
        name="ffn_ln",
    )(x, wg, wu, wd, g, b)


def _in_proj_kernel(x_ref, w_ref, o_ref):
    o_ref[...] = jnp.dot(x_ref[...].astype(BF16), w_ref[...], preferred_element_type=F32)


def _in_proj(x, w, tm):
    n = x.shape[0]
    return pl.pallas_call(
        _in_proj_kernel,
        grid=(n // tm,),
        in_specs=[
            pl.BlockSpec((tm, D_MODEL), lambda i: (i, 0)),
            pl.BlockSpec((D_MODEL, P_COLS), lambda i: (0, 0)),
        ],
        out_specs=pl.BlockSpec((tm, P_COLS), lambda i: (i, 0)),
        out_shape=jax.ShapeDtypeStruct((n, P_COLS), F32),
        compiler_params=pltpu.CompilerParams(
            dimension_semantics=("parallel",), vmem_limit_bytes=VMEM_LIMIT),
        name="in_proj",
    )(x, w)


def _out_proj_ln_kernel(x_ref, ya_ref, yb_ref, wa_ref, wb_ref, g_ref, b_ref, o_ref):
    mix = (jnp.dot(ya_ref[...].astype(BF16), wa_ref[...], preferred_element_type=F32)
           + jnp.dot(yb_ref[...].astype(BF16), wb_ref[...], preferred_element_type=F32))
    o_ref[...] = _layer_norm_rows(ALPHA * x_ref[...] + mix, g_ref[...], b_ref[...])


def _out_proj_ln(x, ya, yb, wa, wb, g, b, tm):
    n = x.shape[0]
    return pl.pallas_call(
        _out_proj_ln_kernel,
        grid=(n // tm,),
        in_specs=[
            pl.BlockSpec((tm, D_MODEL), lambda i: (i, 0)),
            pl.BlockSpec((tm, D_A), lambda i: (i, 0)),
            pl.BlockSpec((tm, D_B), lambda i: (i, 0)),
            pl.BlockSpec((D_A, D_MODEL), lambda i: (0, 0)),
            pl.BlockSpec((D_B, D_MODEL), lambda i: (0, 0)),
            pl.BlockSpec((1, D_MODEL), lambda i: (0, 0)),
            pl.BlockSpec((1, D_MODEL), lambda i: (0, 0)),
        ],
        out_specs=pl.BlockSpec((tm, D_MODEL), lambda i: (i, 0)),
        out_shape=jax.ShapeDtypeStruct((n, D_MODEL), F32),
        compiler_params=pltpu.CompilerParams(
            dimension_semantics=("parallel",), vmem_limit_bytes=VMEM_LIMIT),
        name="out_proj_ln",
    )(x, ya, yb, wa, wb, g, b)


def _rwkv_kernel(*refs, nb, tb, n_alias):
    (p_ref, sp_ref, s0_ref, mu_ref, w0_ref, w2a_ref, a0_ref, g2_ref, kk_ref, ka_ref, rk_ref,
     gng_ref, gnb_ref, seg_ref) = refs[:14]
    (ya_ref, so_ref, st_ref,
     s_scr, carry_scr, nk_scr, p2_scr, q1_scr, rp_scr, v_scr, vsw_scr, ya1_scr, b_scr, yc1_scr, k_scr, w12_scr,
     cbr_scr, ckr_scr, bonus_scr, g_scr, sa_scr, q_scr, yo_scr) = refs[14 + n_alias:]
    j = pl.program_id(1)
    n_hp = H_A // 2

    @pl.when(j == 0)
    def _():
        for b in range(nb):
            for hp in range(n_hp):
                s_scr[b, hp] = jnp.concatenate([s0_ref[b, 2 * hp], s0_ref[b, 2 * hp + 1]], axis=-1)
        carry_scr[...] = sp_ref[...]
        sa_scr[...] = jnp.zeros_like(sa_scr)
        q_scr[...] = jnp.zeros_like(q_scr)
        yo_scr[...] = jnp.zeros_like(yo_scr)

    seg = seg_ref[...]
    rows = nb * tb
    lane128 = lax.broadcasted_iota(jnp.int32, (rows, 128), 1)
    rowid = lax.broadcasted_iota(jnp.int32, (tb, RWKV_COLS), 0)
    rowid_a = lax.broadcasted_iota(jnp.int32, (tb, D_A), 0)

    pms, rps, r_last = [], [], []
    for b in range(nb):
        p = p_ref[b]
        prev = jnp.where(rowid == 0, carry_scr[b], pltpu.roll(p, 1, axis=0))
        carry_scr[b] = p[tb - 1:tb, :]
        pm_b = p + (prev - p) * mu_ref[...]
        r_b = pm_b[:, 0:D_A]
        pms.append(pm_b)
        rps.append(jnp.where(rowid_a == 0, 0.0, pltpu.roll(r_b, 1, axis=0)))
        r_last.append(r_b[tb - 1:tb, :])
    pm = jnp.concatenate(pms, axis=0)
    r = pm[:, 0:D_A]
    k = pm[:, D_A:2 * D_A]
    v = pm[:, 2 * D_A:3 * D_A]
    z = pm[:, 3 * D_A:3 * D_A + 128]
    xg = pm[:, 3 * D_A + 128:RWKV_COLS]
    zt = jnp.where(lane128 < R_W, jnp.tanh(z), z)
    lr = _bdot(zt, w2a_ref[...])
    g_scr[...] = _bdot(_sigmoid(xg), g2_ref[...]).reshape(nb, tb, D_A)
    kk = k * kk_ref[...]
    ss = _seg_sum(kk * kk, seg)
    w = jnp.exp(-math.exp(-0.5) * _sigmoid(w0_ref[...] + lr[:, :D_A]))
    a = _sigmoid(a0_ref[...] + lr[:, D_A:])
    kk = kk * lax.rsqrt(jnp.maximum(ss, 1e-24))
    kmod = k * (1.0 + (a - 1.0) * ka_ref[...])
    bonus_scr[...] = (_seg_sum(r * kmod * rk_ref[...], seg) * v).reshape(nb, tb, D_A)
    bb = kk * a
    nxt = lambda x: pltpu.roll(x, rows - 1, axis=0)
    kk_n, w_n, bb_n = nxt(kk), nxt(w), nxt(bb)
    cc = _seg_dot(jnp.concatenate([bb * kk_n, kmod * kk_n, bb * r, kmod * r], axis=0).astype(BF16), seg)
    cbk, ckk = cc[0:rows], cc[rows:2 * rows]
    cbr_scr[...] = cc[2 * rows:3 * rows].reshape(nb, tb, D_A)
    ckr_scr[...] = cc[3 * rows:4 * rows].reshape(nb, tb, D_A)
    def put(scr, arr):
        for hp in range(n_hp):
            scr[:, hp] = arr[:, hp * 128:(hp + 1) * 128].reshape(nb, tb, 128)

    put(nk_scr, -kk)
    put(p2_scr, -(w * kk_n))
    put(q1_scr, w * r)
    put(rp_scr, jnp.concatenate(rps, axis=0))
    put(ya1_scr, bb * w_n - cbk * bb_n)
    put(b_scr, bb)
    put(yc1_scr, kmod * w_n - ckk * bb_n)
    put(k_scr, kmod)
    put(w12_scr, w * w_n)
    put(v_scr, v)
    for hp in range(n_hp):
        vsw_scr[:, hp] = pltpu.roll(v[:, hp * 128:(hp + 1) * 128], 64, axis=1).reshape(nb, tb, 128)

    row8 = lax.broadcasted_iota(jnp.int32, (8, 128), 0)
    lane8 = lax.broadcasted_iota(jnp.int32, (8, 128), 1)
    pair8 = lax.shift_right_logical(row8, 1)
    half8 = ((((row8 & 1) == 0) & (lane8 < 64)) | (((row8 & 1) == 1) & (lane8 >= 64))).astype(F32)
    mk3 = ((row8 >= 6) & ((((row8 & 1) == 0) & (lane8 < 64)) | (((row8 & 1) == 1) & (lane8 >= 64)))).astype(F32)
    row64 = lax.broadcasted_iota(jnp.int32, (8, 64), 0)

    def pair_tile(q0, q1, q2, q3):
        return jnp.where(pair8 == 0, q0, jnp.where(pair8 == 1, q1, jnp.where(pair8 == 2, q2, q3))) * half8

    chains = [(b, hp) for b in range(nb) for hp in range(n_hp)]
    nt_dims = (((1,), (1,)), ((), ()))
    tn_dims = (((0,), (0,)), ((), ()))
    grp = min(16, tb)
    n_pair = grp // 2

    def natural_rows(rows_h0, rows_h1):
        pad = [jnp.zeros((8 - n_pair, N_A), F32)] if n_pair < 8 else []
        return jnp.concatenate([jnp.concatenate(rows_h0 + pad, axis=0),
                                jnp.concatenate(rows_h1 + pad, axis=0)], axis=1)

    def group(tg, carry):
        t0 = tg * grp

        def row(ref, c, t, n=8):
            return ref[c[0], c[1], pl.ds(t0 + t, n, stride=0), :]

        out = {c: [[] for _ in range(6)] for c in chains}
        for pi in range(n_pair):
            i = 2 * pi
            reds = {}
            for c in chains:
                a_mat = pair_tile(row(nk_scr, c, i), row(p2_scr, c, i), row(q1_scr, c, i), row(rp_scr, c, i))
                reds[c] = lax.dot_general(a_mat.astype(BF16), s_scr[c[0], c[1]].astype(BF16), nt_dims,
                                          preferred_element_type=F32)
            xs = {}
            for c in chains:
                x_mat = jnp.where(
                    row64 < 4, reds[c],
                    jnp.where(row64 == 4, row(v_scr, c, i)[:, :64],
                              jnp.where(row64 == 5, row(vsw_scr, c, i)[:, :64],
                                        jnp.where(row64 == 6, row(v_scr, c, i + 1)[:, :64],
                                                  row(vsw_scr, c, i + 1)[:, :64]))))
                xs[c] = x_mat.astype(BF16)
            for c in chains:
                y_mat = pair_tile(row(ya1_scr, c, i), row(b_scr, c, i + 1), row(yc1_scr, c, i), row(k_scr, c, i + 1))
                d_s = lax.dot_general(xs[c], y_mat.astype(BF16), tn_dims, preferred_element_type=F32)
                s_scr[c[0], c[1]] = s_scr[c[0], c[1]] * row(w12_scr, c, i, N_A) + d_s
                for slot, red_row in enumerate((0, 1, 4, 5, 6, 7)):
                    out[c][slot].append(reds[c][red_row:red_row + 1, :])
        even0 = pl.multiple_of(tg * grp, 8)
        for c in chains:
            o = out[c]
            sa_scr[c[0], c[1], pl.ds(even0, 8, stride=2), :] = natural_rows(o[0], o[1])
            q_scr[c[0], c[1], pl.ds(even0, 8, stride=2), :] = natural_rows(o[2], o[3])
            yo_scr[c[0], c[1], pl.ds(even0 + 8, 8, stride=2), :] = natural_rows(o[4], o[5])
        return carry

    lax.fori_loop(0, tb // grp, group, 0)

    for (b, hp) in chains:
        rl = r_last[b][:, hp * 128:(hp + 1) * 128]
        a_mat = rl * mk3
        red = lax.dot_general(a_mat.astype(BF16), s_scr[b, hp].astype(BF16), nt_dims, preferred_element_type=F32)
        y_last = jnp.concatenate([red[6:7, :], red[7:8, :]], axis=1)
        yo_scr[b, hp, pl.ds(tb + 8, 8), :] = jnp.broadcast_to(y_last, (8, 128))

    def slab(scr, first):
        return jnp.concatenate(
            [jnp.concatenate([scr[b, hp, pl.ds(first, tb), :] for hp in range(n_hp)], axis=1) for b in range(nb)],
            axis=0)

    y_even = (slab(q_scr, 0) + slab(sa_scr, 0) * cbr_scr[...].reshape(rows, D_A)
              + slab(v_scr, 0) * ckr_scr[...].reshape(rows, D_A))
    parity = lax.broadcasted_iota(jnp.int32, (rows, D_A), 0) & 1
    y = jnp.where(parity == 0, y_even, slab(yo_scr, 9))
    mu = _seg_sum(y, seg) * (1.0 / N_A)
    d = y - mu
    var = _seg_sum(d * d, seg) * (1.0 / N_A)
    yn = d * lax.rsqrt(var + GN_EPS) * gng_ref[...] + gnb_ref[...]
    ya_ref[...] = ((yn.reshape(nb, tb, D_A) + bonus_scr[...]) * g_scr[...]).astype(ya_ref.dtype)

    @pl.when(j == pl.num_programs(1) - 1)
    def _():
        for b in range(nb):
            for hp in range(n_hp):
                s_pair = s_scr[b, hp]
                st_ref[b, 2 * hp] = s_pair[:, :N_A]
                st_ref[b, 2 * hp + 1] = s_pair[:, N_A:]
        so_ref[...] = carry_scr[...]


def _rwkv(p3, shift_prev, wkv_all, l, wkv_out_prev, wts, nb, tb, y_dtype):
    bsz, t, _ = p3.shape
    n_hp = H_A // 2
    full = lambda shape: pl.BlockSpec(shape, lambda i, j: (0,) * len(shape))
    blk = lambda: pltpu.VMEM((nb, tb, D_A), F32)
    n_alias = 0 if wkv_out_prev is None else 1
    kern = functools.partial(_rwkv_kernel, nb=nb, tb=tb, n_alias=n_alias)
    state_spec = pl.BlockSpec((None, nb, H_A, N_A, N_A), lambda i, j: (l, i, 0, 0, 0))
    in_specs = [
        pl.BlockSpec((nb, tb, RWKV_COLS), lambda i, j: (i, j, 0)),
        pl.BlockSpec((nb, 1, RWKV_COLS), lambda i, j: (i, 0, 0)),
        state_spec,
        full((1, RWKV_COLS)), full((1, D_A)), full((128, 2 * D_A)), full((1, D_A)), full((R_G, D_A)),
        full((1, D_A)), full((1, D_A)), full((1, D_A)), full((1, D_A)), full((1, D_A)), full((D_A // 2, D_A // 2)),
    ]
    args = [p3, shift_prev, wkv_all, *wts]
    aliases = {}
    if n_alias:
        in_specs.append(pl.BlockSpec(memory_space=pl.ANY))
        args.append(wkv_out_prev)
        aliases = {len(args) - 1: 2}
    return pl.pallas_call(
        kern,
        grid=(bsz // nb, t // tb),
        in_specs=in_specs,
        out_specs=[
            pl.BlockSpec((nb, tb, D_A), lambda i, j: (i, j, 0)),
            pl.BlockSpec((nb, 1, RWKV_COLS), lambda i, j: (i, 0, 0)),
            state_spec,
        ],
        out_shape=[
            jax.ShapeDtypeStruct((bsz, t, D_A), y_dtype),
            jax.ShapeDtypeStruct((bsz, 1, RWKV_COLS), F32),
            jax.ShapeDtypeStruct((DEPTH, bsz, H_A, N_A, N_A), F32),
        ],
        scratch_shapes=[
            pltpu.VMEM((nb, n_hp, N_A, 128), F32),
            pltpu.VMEM((nb, 1, RWKV_COLS), F32),
            *[pltpu.VMEM((nb, n_hp, tb, 2 * N_A), F32) for _ in range(11)],
            *[blk() for _ in range(4)],
            *[pltpu.VMEM((nb, n_hp, tb + 24, 2 * N_A), F32) for _ in range(3)],
        ],
        input_output_aliases=aliases,
        compiler_params=pltpu.CompilerParams(
            dimension_semantics=("parallel", "arbitrary"), vmem_limit_bytes=VMEM_LIMIT),
        name="rwkv7",
    )(*args)


def _mlstm_kernel(*refs, nb, tb, lc, n_alias):
    (pm_ref, pg_ref, cp_ref, c0_ref, n0_ref, m0_ref, cw_ref, cb_ref, gb_ref, mhg_ref) = refs[:10]
    (yb_ref, co_ref, ct_ref, nt_ref, mt_ref, x_scr, c_scr, n_scr, m_scr) = refs[10 + n_alias:]
    j = pl.program_id(1)
    hi = lax.Precision.HIGHEST
    nch = tb // lc
    lc_shift = lc.bit_length() - 1

    @pl.when(j == 0)
    def _():
        c_scr[...] = c0_ref[...]
        n_scr[...] = n0_ref[...]
        m_scr[...] = jnp.broadcast_to(m0_ref[...], m_scr.shape)
        x_scr[:, 5:8, :] = cp_ref[...]

    rr = lax.broadcasted_iota(jnp.int32, (lc, lc), 0)
    cc = lax.broadcasted_iota(jnp.int32, (lc, lc), 1)
    causal = rr >= cc
    rb = lax.broadcasted_iota(jnp.int32, (tb, tb), 0)
    cb = lax.broadcasted_iota(jnp.int32, (tb, tb), 1)
    tri_blk = ((rb >= cb) & (lax.shift_right_logical(rb, lc_shift) == lax.shift_right_logical(cb, lc_shift))
               ).astype(F32)
    lane_g = lax.broadcasted_iota(jnp.int32, (tb, 128), 1)
    tn_dims = (((0,), (0,)), ((), ()))
    nt_dims = (((1,), (1,)), ((), ()))

    units = [(b, c, h) for b in range(nb) for c in range(nch) for h in range(H_B)]
    q_u, k_u, v_u, o_u = {}, {}, {}, {}
    li_col, li_row, b_col, b_row, b_l = {}, {}, {}, {}, {}

    for b in range(nb):
        x = pm_ref[b, :, 0:2 * D_B]
        x_scr[b, pl.ds(8, tb), :] = x
        conv = cb_ref[...] + x * cw_ref[3:4, :]
        for s in range(1, CONV_W):
            conv = conv + x_scr[b, pl.ds(8 - s, tb), :] * cw_ref[3 - s:4 - s, :]
        x_scr[b, 5:8, :] = x_scr[b, pl.ds(8 + tb - 3, 3), :]
        sc = conv * _sigmoid(conv)
        q_all = sc[:, :D_B]
        k_all = sc[:, D_B:] * (DK_B ** -0.5)
        v_all = pm_ref[b, :, 2 * D_B:3 * D_B]
        o_all = pm_ref[b, :, 3 * D_B:4 * D_B]
        gp = pg_ref[b, :, 0:128] + gb_ref[...]
        gates = jnp.where(lane_g < H_B, gp, jnp.minimum(gp, 0.0) - jnp.log1p(jnp.exp(-jnp.abs(gp))))
        csum_col = jnp.dot(tri_blk, gates, precision=hi, preferred_element_type=F32)
        gt = gates.T
        csum_row = lax.dot_general(gt, tri_blk, nt_dims, precision=hi, preferred_element_type=F32)
        for c in range(nch):
            rs = slice(c * lc, (c + 1) * lc)
            for h in range(H_B):
                u = (b, c, h)
                hs = slice(h * DK_B, (h + 1) * DK_B)
                q_u[u], k_u[u], v_u[u], o_u[u] = q_all[rs, hs], k_all[rs, hs], v_all[rs, hs], o_all[rs, hs]
                li_col[u] = gates[rs, h:h + 1]
                li_row[u] = gt[h:h + 1, rs]
                b_col[u] = csum_col[rs, H_B + h:H_B + h + 1]
                b_row[u] = csum_row[H_B + h:H_B + h + 1, rs]
                b_l[u] = csum_col[(c + 1) * lc - 1:(c + 1) * lc, H_B + h:H_B + h + 1]

    rep = lambda col: jnp.broadcast_to(col, (lc, DK_B))
    bc = {u: rep(b_col[u]) for u in units}
    lic = {u: rep(li_col[u]) for u in units}

    last_max = {u: jnp.max(b_l[u] - b_row[u] + li_row[u], axis=-1, keepdims=True) for u in units}
    m_prev, m_new = {}, {}
    for b in range(nb):
        for h in range(H_B):
            m_p = m_scr[b, h][:, 0:1]
            for c in range(nch):
                u = (b, c, h)
                m_prev[u] = m_p
                m_p = jnp.maximum(b_l[u] + m_p, last_max[u])
                m_new[u] = m_p
            m_scr[b, h] = jnp.broadcast_to(m_p, (1, 128))

    dmat = {u: jnp.where(causal, bc[u][:, :lc] - b_row[u] + li_row[u], -jnp.inf) for u in units}
    row_max = {u: rep(jnp.max(dmat[u], axis=-1, keepdims=True)) for u in units}
    m_t, g_inter, e_mat = {}, {}, {}
    for u in units:
        inter = bc[u] + m_prev[u]
        m_t[u] = jnp.maximum(inter, row_max[u])
        g_inter[u] = jnp.exp(inter - m_t[u])
        e_mat[u] = jnp.exp(dmat[u] - m_t[u][:, :lc])

    qk = {u: lax.dot_general(q_u[u].astype(BF16), k_u[u].astype(BF16), nt_dims, preferred_element_type=F32)
          for u in units}
    s_mat = {u: qk[u] * e_mat[u] for u in units}
    sv = {u: _bdot(s_mat[u], v_u[u]) for u in units}
    s_sum = {u: rep(jnp.sum(s_mat[u], axis=-1, keepdims=True)) for u in units}
    kw = {u: k_u[u] * jnp.exp(b_l[u] - bc[u] + lic[u] - m_new[u]) for u in units}
    kwv = {u: lax.dot_general(kw[u].astype(BF16), v_u[u].astype(BF16), tn_dims, preferred_element_type=F32)
           for u in units}

    q_c, qn_prod = {}, {}
    c_cur = {(b, h): c_scr[b, h] for b in range(nb) for h in range(H_B)}
    n_cur = {(b, h): n_scr[b, h] for b in range(nb) for h in range(H_B)}
    for c in range(nch):
        for b in range(nb):
            for h in range(H_B):
                u = (b, c, h)
                q_c[u] = _bdot(q_u[u], c_cur[(b, h)])
                qn_prod[u] = q_u[u] * n_cur[(b, h)]
                dec = jnp.exp(b_l[u] + m_prev[u] - m_new[u])
                c_cur[(b, h)] = dec * c_cur[(b, h)] + kwv[u]
                n_cur[(b, h)] = dec * n_cur[(b, h)] + jnp.sum(kw[u], axis=0, keepdims=True)
    for b in range(nb):
        for h in range(H_B):
            c_scr[b, h] = c_cur[(b, h)]
            n_scr[b, h] = n_cur[(b, h)]
    q_n = {u: rep(jnp.sum(qn_prod[u], axis=-1, keepdims=True)) for u in units}

    hh = {}
    for u in units:
        num = g_inter[u] * q_c[u] + sv[u]
        den = g_inter[u] * q_n[u] + s_sum[u]
        hh[u] = num / jnp.maximum(jnp.abs(den), jnp.exp(-m_t[u]))
    mu = {u: rep(jnp.sum(hh[u], axis=-1, keepdims=True)) * (1.0 / DK_B) for u in units}
    dev = {u: hh[u] - mu[u] for u in units}
    var = {u: rep(jnp.sum(dev[u] * dev[u], axis=-1, keepdims=True)) * (1.0 / DK_B) for u in units}
    for u in units:
        b, c, h = u
        hs = slice(h * DK_B, (h + 1) * DK_B)
        hn = dev[u] * lax.rsqrt(var[u] + MH_EPS) * mhg_ref[:, hs]
        yb_ref[b, c * lc:(c + 1) * lc, hs] = (_sigmoid(o_u[u]) * hn).astype(yb_ref.dtype)

    @pl.when(j == pl.num_programs(1) - 1)
    def _():
        ct_ref[...] = c_scr[...]
        nt_ref[...] = n_scr[...]
        mt_ref[...] = m_scr[:, :, :, 0:1]
        co_ref[...] = x_scr[:, 5:8, :]


def _mlstm(p3, conv_prev, c_all, l, c_out_prev, n0, m0, wts, nb, tb, lc, y_dtype):
    bsz, t, _ = p3.shape
    full = lambda shape: pl.BlockSpec(shape, lambda i, j: (0,) * len(shape))
    n_alias = 0 if c_out_prev is None else 1
    kern = functools.partial(_mlstm_kernel, nb=nb, tb=tb, lc=lc, n_alias=n_alias)
    gate_blk = RWKV_COLS // GATE_COLS
    c_spec = pl.BlockSpec((None, nb, H_B, DK_B, DK_B), lambda i, j: (l, i, 0, 0, 0))
    in_specs = [
        pl.BlockSpec((nb, tb, ML_COLS), lambda i, j: (i, j, 1)),
        pl.BlockSpec((nb, tb, GATE_COLS), lambda i, j: (i, j, gate_blk)),
        pl.BlockSpec((nb, CONV_W - 1, 2 * D_B), lambda i, j: (i, 0, 0)),
        c_spec,
        pl.BlockSpec((nb, H_B, 1, DK_B), lambda i, j: (i, 0, 0, 0)),
        pl.BlockSpec((nb, H_B, 1, 1), lambda i, j: (i, 0, 0, 0)),
        full((CONV_W, 2 * D_B)), full((1, 2 * D_B)), full((1, 128)), full((1, D_B)),
    ]
    args = [p3, p3, conv_prev, c_all, n0, m0, *wts]
    aliases = {}
    if n_alias:
        in_specs.append(pl.BlockSpec(memory_space=pl.ANY))
        args.append(c_out_prev)
        aliases = {len(args) - 1: 2}
    return pl.pallas_call(
        kern,
        grid=(bsz // nb, t // tb),
        in_specs=in_specs,
        out_specs=[
            pl.BlockSpec((nb, tb, D_B), lambda i, j: (i, j, 0)),
            pl.BlockSpec((nb, CONV_W - 1, 2 * D_B), lambda i, j: (i, 0, 0)),
            c_spec,
            pl.BlockSpec((nb, H_B, 1, DK_B), lambda i, j: (i, 0, 0, 0)),
            pl.BlockSpec((nb, H_B, 1, 1), lambda i, j: (i, 0, 0, 0)),
        ],
        out_shape=[
            jax.ShapeDtypeStruct((bsz, t, D_B), y_dtype),
            jax.ShapeDtypeStruct((bsz, CONV_W - 1, 2 * D_B), F32),
            jax.ShapeDtypeStruct((DEPTH, bsz, H_B, DK_B, DK_B), F32),
            jax.ShapeDtypeStruct((bsz, H_B, 1, DK_B), F32),
            jax.ShapeDtypeStruct((bsz, H_B, 1, 1), F32),
        ],
        scratch_shapes=[
            pltpu.VMEM((nb, tb + 8, 2 * D_B), F32),
            pltpu.VMEM((nb, H_B, DK_B, DK_B), F32),
            pltpu.VMEM((nb, H_B, 1, DK_B), F32),
            pltpu.VMEM((nb, H_B, 1, 128), F32),
        ],
        input_output_aliases=aliases,
        compiler_params=pltpu.CompilerParams(
            dimension_semantics=("parallel", "arbitrary"), vmem_limit_bytes=VMEM_LIMIT),
        name="mlstm",
    )(*args)


def _layer_weights(l, w):
    bf = lambda a: a.astype(BF16)
    row = lambda a: a.reshape(1, -1)
    w_in = w['w_in'][l]
    w_gates = jnp.pad(w_in[:, RWKV_COLS + ML_COLS:], ((0, 0), (0, GATE_COLS - 2 * H_B)))
    w_cat = jnp.concatenate([w_in[:, :RWKV_COLS], w_gates, w_in[:, RWKV_COLS:RWKV_COLS + ML_COLS]], axis=1)
    zero = jnp.zeros((R_W, D_A), F32)
    w2a = jnp.concatenate([jnp.concatenate([w['w2'][l], zero], axis=1),
                           jnp.concatenate([zero, w['a2'][l]], axis=1)], axis=0)
    ids = jnp.arange(D_A // 2) // N_A
    seg = (ids[:, None] == ids[None, :]).astype(BF16)
    gate_bias = jnp.pad(jnp.concatenate([w['i_bias'][l], w['f_bias'][l]]), (0, 128 - 2 * H_B)).reshape(1, 128)
    return dict(
        ln1=(row(w['ln1_g'][l]), row(w['ln1_b'][l])),
        ln3=(row(w['ln3_g'][l]), row(w['ln3_b'][l])),
        w_cat=bf(w_cat),
        rwkv=(row(w['mu_shift'][l]), row(w['w0'][l]), bf(w2a), row(w['a0'][l]), bf(w['g2'][l]), row(w['k_k'][l]),
              row(w['k_a'][l]), row(w['r_k'][l]), row(w['gn_g'][l]), row(w['gn_b'][l]), seg),
        mlstm=(w['conv_w'][l], row(w['conv_b'][l]), gate_bias, row(w['mh_g'][l])),
        out=(bf(w['w_out'][l][:D_A]), bf(w['w_out'][l][D_A:]), row(w['ln2_g'][l]), row(w['ln2_b'][l])),
    )


def _trunk(x, states, lw, ffn_w, nb_r, tb_r, nb_m, tb_m):
    bsz, t, _ = x.shape
    n = bsz * t
    tm_ffn, tm_out, tm_proj = min(FFN_TILE_ROWS, n), min(OUT_PROJ_TILE_ROWS, n), min(PROJ_TILE_ROWS, n)
    y_dtype = BF16 if t % 16 == 0 else F32
    lc = math.gcd(t, CHUNK)
    xf = x.reshape(n, D_MODEL)
    st_shift, st_wkv, st_conv, st_c, st_n, st_m = states
    new = [[] for _ in range(4)]
    wkv_out = jnp.zeros((DEPTH, bsz, H_A, N_A, N_A), F32)
    c_out = jnp.zeros((DEPTH, bsz, H_B, DK_B, DK_B), F32)
    for l in range(DEPTH):
        wl = lw[l]
        x1 = _ffn_ln(xf, *ffn_w[0], *wl['ln1'], l=l, tm=tm_ffn)
        p3 = _in_proj(x1, wl['w_cat'], tm=tm_proj).reshape(bsz, t, P_COLS)
        ya, shift, wkv_out = _rwkv(p3, st_shift[l].reshape(bsz, 1, RWKV_COLS), st_wkv, l, wkv_out,
                                   wl['rwkv'], nb_r, tb_r, y_dtype)
        yb, conv, c_out, n_t, m_t = _mlstm(p3, st_conv[l], st_c, l, c_out,
                                           st_n[l].reshape(bsz, H_B, 1, DK_B), st_m[l].reshape(bsz, H_B, 1, 1),
                                           wl['mlstm'], nb_m, tb_m, lc, y_dtype)
        x2 = _out_proj_ln(x1, ya.reshape(n, D_A), yb.reshape(n, D_B), *wl['out'], tm=tm_out)
        xf = _ffn_ln(x2, *ffn_w[1], *wl['ln3'], l=l, tm=tm_ffn)
        for idx, s in enumerate((shift.reshape(bsz, RWKV_COLS), conv, n_t.reshape(bsz, H_B, DK_B),
                                 m_t.reshape(bsz, H_B))):
            new[idx].append(s)
    shift_o, conv_o, n_o, m_o = [jnp.stack(s) for s in new]
    return xf.reshape(bsz, t, D_MODEL), [shift_o, wkv_out, conv_o, c_out, n_o, m_o]


def kernel(x_prompt, x_sample, state_shift, state_wkv, state_conv, state_C, state_n, state_m,
           ffn1_wg, ffn1_wu, ffn1_wd, ln1_g, ln1_b, w_in, mu_shift, w0, w2, a0, a2, g2, k_k, k_a, r_k,
           gn_g, gn_b, conv_w, conv_b, i_bias, f_bias, mh_g, w_out, ln2_g, ln2_b,
           ffn2_wg, ffn2_wu, ffn2_wd, ln3_g, ln3_b):
    w = dict(ffn1_wg=ffn1_wg, ffn1_wu=ffn1_wu, ffn1_wd=ffn1_wd, ln1_g=ln1_g, ln1_b=ln1_b, w_in=w_in,
             mu_shift=mu_shift, w0=w0, w2=w2, a0=a0, a2=a2, g2=g2, k_k=k_k, k_a=k_a, r_k=r_k,
             gn_g=gn_g, gn_b=gn_b, conv_w=conv_w, conv_b=conv_b, i_bias=i_bias, f_bias=f_bias,
             mh_g=mh_g, w_out=w_out, ln2_g=ln2_g, ln2_b=ln2_b, ffn2_wg=ffn2_wg, ffn2_wu=ffn2_wu,
             ffn2_wd=ffn2_wd, ln3_g=ln3_g, ln3_b=ln3_b)
    lw = [_layer_weights(l, w) for l in range(DEPTH)]
    ffn_w = ((ffn1_wg, ffn1_wu, ffn1_wd), (ffn2_wg, ffn2_wu, ffn2_wd))
    bp = x_prompt.shape[0]
    dt = x_prompt.dtype
    init = [jnp.zeros((DEPTH, bp, RWKV_COLS), dt),
            jnp.zeros((DEPTH, bp, H_A, N_A, N_A), dt),
            jnp.zeros((DEPTH, bp, CONV_W - 1, 2 * D_B), dt),
            jnp.zeros((DEPTH, bp, H_B, DK_B, DK_B), dt),
            jnp.zeros((DEPTH, bp, H_B, DK_B), dt),
            jnp.zeros((DEPTH, bp, H_B), dt)]
    y_p, ps = _trunk(x_prompt, init, lw, ffn_w, nb_r=8, tb_r=64, nb_m=1, tb_m=256)
    y_s, ss = _trunk(x_sample, [state_shift, state_wkv, state_conv, state_C, state_n, state_m], lw, ffn_w,
                     nb_r=8, tb_r=8, nb_m=8, tb_m=8)
    return (y_p, y_s, *ps, *ss)
```

```python
import functools
import math

import jax
import jax.numpy as jnp
from jax import lax
from jax.experimental import pallas as pl
from jax.experimental.pallas import tpu as pltpu

D_MODEL = 1024
DEPTH = 2
D_A = 512
N_A = 64
H_A = 8
D_B = 512
H_B = 4
DK_B = 128
R_W = 64
R_A = 64
R_G = 128
RWKV_COLS = 3 * D_A + R_W + R_A + R_G
CONV_W = 4
CHUNK = 64
D_FF = 2816
ALPHA = (2.0 * DEPTH) ** 0.25
LN_EPS = 1e-5
GN_EPS = 64e-5
MH_EPS = 1e-6

GATE_COLS = 256
ML_COLS = 4 * D_B
P_COLS = RWKV_COLS + GATE_COLS + ML_COLS
VMEM_LIMIT = 56 * 1024 * 1024
FFN_ROW_CHUNK = 256
FFN_COL_BLOCK = 256
FFN_MID_CHUNK = 512
FFN_TILE_ROWS = 2048
OUT_PROJ_TILE_ROWS = 2048
PROJ_TILE_ROWS = 512

F32 = jnp.float32
BF16 = jnp.bfloat16


def _bdot(a, b):
    return jnp.dot(a.astype(BF16), b.astype(BF16), preferred_element_type=F32)


def _seg_dot(x_bf16, seg_half):
    h = seg_half.shape[0]
    return jnp.concatenate([jnp.dot(x_bf16[:, :h], seg_half, preferred_element_type=F32),
                            jnp.dot(x_bf16[:, h:], seg_half, preferred_element_type=F32)], axis=1)


def _seg_sum(x, seg_half):
    hi = x.astype(BF16)
    lo = (x - hi.astype(F32)).astype(BF16)
    return _seg_dot(hi, seg_half) + _seg_dot(lo, seg_half)


def _layer_norm_rows(y, g, b):
    mu = jnp.mean(y, axis=-1, keepdims=True)
    d = y - mu
    var = jnp.mean(d * d, axis=-1, keepdims=True)
    return d * lax.rsqrt(var + LN_EPS) * g + b


def _sigmoid(x):
    return 1.0 / (1.0 + jnp.exp(-x))


def _ffn_ln_kernel(x_ref, wg_ref, wu_ref, wd_ref, g_ref, b_ref, o_ref, xb_scr, *, n_chunks):
    j = pl.program_id(1)
    last = pl.num_programs(1) - 1
    tm = x_ref.shape[0]
    rc = tm // n_chunks

    def partial_ffn(xb, wg, wu, wd):
        hg = jnp.dot(xb, wg, preferred_element_type=F32)
        hu = jnp.dot(xb, wu, preferred_element_type=F32)
        h = (hg * _sigmoid(hg)) * hu
        return jnp.dot(h.astype(BF16), wd, preferred_element_type=F32)

    def weights():
        return wg_ref[...].astype(BF16), wu_ref[...].astype(BF16), wd_ref[...].astype(BF16)

    @pl.when(j == 0)
    def _():
        wg, wu, wd = weights()
        for c in range(n_chunks):
            rows = pl.ds(c * rc, rc)
            xb = x_ref[rows, :].astype(BF16)
            xb_scr[rows, :] = xb
            o_ref[rows, :] = partial_ffn(xb, wg, wu, wd)

    @pl.when((j > 0) & (j < last))
    def _():
        wg, wu, wd = weights()
        for c in range(tm // FFN_MID_CHUNK):
            rows = pl.ds(c * FFN_MID_CHUNK, FFN_MID_CHUNK)
            o_ref[rows, :] += partial_ffn(xb_scr[rows, :], wg, wu, wd)

    @pl.when(j == last)
    def _():
        wg, wu, wd = weights()
        for c in range(n_chunks):
            rows = pl.ds(c * rc, rc)
            acc = o_ref[rows, :] + partial_ffn(xb_scr[rows, :], wg, wu, wd)
            o_ref[rows, :] = _layer_norm_rows(ALPHA * x_ref[rows, :] + 0.5 * acc, g_ref[...], b_ref[...])


def _ffn_ln(x, wg, wu, wd, g, b, l, tm):
    n = x.shape[0]
    const = lambda shape: pl.BlockSpec(shape, lambda i, j: (0, 0))
    return pl.pallas_call(
        functools.partial(_ffn_ln_kernel, n_chunks=tm // FFN_ROW_CHUNK),
        grid=(n // tm, D_FF // FFN_COL_BLOCK),
        in_specs=[
            pl.BlockSpec((tm, D_MODEL), lambda i, j: (i, 0)),
            pl.BlockSpec((None, D_MODEL, FFN_COL_BLOCK), lambda i, j: (l, 0, j)),
            pl.BlockSpec((None, D_MODEL, FFN_COL_BLOCK), lambda i, j: (l, 0, j)),
            pl.BlockSpec((None, FFN_COL_BLOCK, D_MODEL), lambda i, j: (l, j, 0)),
            const((1, D_MODEL)), const((1, D_MODEL)),
        ],
        out_specs=pl.BlockSpec((tm, D_MODEL), lambda i, j: (i, 0)),
        out_shape=jax.ShapeDtypeStruct((n, D_MODEL), F32),
        scratch_shapes=[pltpu.VMEM((tm, D_MODEL), BF16)],
        compiler_params=pltpu.CompilerParams(
            dimension_semantics=("parallel", "arbitrary"), vmem_limit_bytes=VMEM_LIMIT),
        name="ffn_ln",
    )(x, wg, wu, wd, g, b)


def _in_proj_kernel(x_ref, w_ref, o_ref):
    o_ref[...] = jnp.dot(x_ref[...].astype(BF16), w_ref[...], preferred_element_type=F32)


def _in_proj(x, w, tm):
    n = x.shape[0]
    return pl.pallas_call(
        _in_proj_kernel,
        grid=(n // tm,),
        in_specs=[
            pl.BlockSpec((tm, D_MODEL), lambda i: (i, 0)),
            pl.BlockSpec((D_MODEL, P_COLS), lambda i: (0, 0)),
        ],
        out_specs=pl.BlockSpec((tm, P_COLS), lambda i: (i, 0)),
        out_shape=jax.ShapeDtypeStruct((n, P_COLS), F32),
        compiler_params=pltpu.CompilerParams(
            dimension_semantics=("parallel",), vmem_limit_bytes=VMEM_LIMIT),
        name="in_proj",
    )(x, w)


def _out_proj_ln_kernel(x_ref, ya_ref, yb_ref, wa_ref, wb_ref, g_ref, b_ref, o_ref):
    mix = (jnp.dot(ya_ref[...].astype(BF16), wa_ref[...], preferred_element_type=F32)
           + jnp.dot(yb_ref[...].astype(BF16), wb_ref[...], preferred_element_type=F32))
    o_ref[...] = _layer_norm_rows(ALPHA * x_ref[...] + mix, g_ref[...], b_ref[...])


def _out_proj_ln(x, ya, yb, wa, wb, g, b, tm):
    n = x.shape[0]
    return pl.pallas_call(
        _out_proj_ln_kernel,
        grid=(n // tm,),
        in_specs=[
            pl.BlockSpec((tm, D_MODEL), lambda i: (i, 0)),
            pl.BlockSpec((tm, D_A), lambda i: (i, 0)),
            pl.BlockSpec((tm, D_B), lambda i: (i, 0)),
            pl.BlockSpec((D_A, D_MODEL), lambda i: (0, 0)),
            pl.BlockSpec((D_B, D_MODEL), lambda i: (0, 0)),
            pl.BlockSpec((1, D_MODEL), lambda i: (0, 0)),
            pl.BlockSpec((1, D_MODEL), lambda i: (0, 0)),
        ],
        out_specs=pl.BlockSpec((tm, D_MODEL), lambda i: (i, 0)),
        out_shape=jax.ShapeDtypeStruct((n, D_MODEL), F32),
        compiler_params=pltpu.CompilerParams(
            dimension_semantics=("parallel",), vmem_limit_bytes=VMEM_LIMIT),
        name="out_proj_ln",
    )(x, ya, yb, wa, wb, g, b)


def _rwkv_kernel(*refs, nb, tb, n_alias):
    (p_ref, sp_ref, s0_ref, mu_ref, w0_ref, w2a_ref, a0_ref, g2_ref, kk_ref, ka_ref, rk_ref,
     gng_ref, gnb_ref, seg_ref) = refs[:14]
    (ya_ref, so_ref, st_ref,
     s_scr, carry_scr, nk_scr, p2_scr, q1_scr, rp_scr, v_scr, vsw_scr, ya1_scr, b_scr, yc1_scr, k_scr, w12_scr,
     cbr_scr, ckr_scr, bonus_scr, g_scr, sa_scr, q_scr, yo_scr) = refs[14 + n_alias:]
    j = pl.program_id(1)
    n_hp = H_A // 2

    @pl.when(j == 0)
    def _():
        for b in range(nb):
            for hp in range(n_hp):
                s_scr[b, hp] = jnp.concatenate([s0_ref[b, 2 * hp], s0_ref[b, 2 * hp + 1]], axis=-1)
        carry_scr[...] = sp_ref[...]
        sa_scr[...] = jnp.zeros_like(sa_scr)
        q_scr[...] = jnp.zeros_like(q_scr)
        yo_scr[...] = jnp.zeros_like(yo_scr)

    seg = seg_ref[...]
    rows = nb * tb
    lane128 = lax.broadcasted_iota(jnp.int32, (rows, 128), 1)
    rowid = lax.broadcasted_iota(jnp.int32, (tb, RWKV_COLS), 0)
    rowid_a = lax.broadcasted_iota(jnp.int32, (tb, D_A), 0)

    pms, rps, r_last = [], [], []
    for b in range(nb):
        p = p_ref[b]
        prev = jnp.where(rowid == 0, carry_scr[b], pltpu.roll(p, 1, axis=0))
        carry_scr[b] = p[tb - 1:tb, :]
        pm_b = p + (prev - p) * mu_ref[...]
        r_b = pm_b[:, 0:D_A]
        pms.append(pm_b)
        rps.append(jnp.where(rowid_a == 0, 0.0, pltpu.roll(r_b, 1, axis=0)))
        r_last.append(r_b[tb - 1:tb, :])
    pm = jnp.concatenate(pms, axis=0)
    r = pm[:, 0:D_A]
    k = pm[:, D_A:2 * D_A]
    v = pm[:, 2 * D_A:3 * D_A]
    z = pm[:, 3 * D_A:3 * D_A + 128]
    xg = pm[:, 3 * D_A + 128:RWKV_COLS]
    zt = jnp.where(lane128 < R_W, jnp.tanh(z), z)
    lr = _bdot(zt, w2a_ref[...])
    g_scr[...] = _bdot(_sigmoid(xg), g2_ref[...]).reshape(nb, tb, D_A)
    kk = k * kk_ref[...]
    ss = _seg_sum(kk * kk, seg)
    w = jnp.exp(-math.exp(-0.5) * _sigmoid(w0_ref[...] + lr[:, :D_A]))
    a = _sigmoid(a0_ref[...] + lr[:, D_A:])
    kk = kk * lax.rsqrt(jnp.maximum(ss, 1e-24))
    kmod = k * (1.0 + (a - 1.0) * ka_ref[...])
    bonus_scr[...] = (_seg_sum(r * kmod * rk_ref[...], seg) * v).reshape(nb, tb, D_A)
    bb = kk * a
    nxt = lambda x: pltpu.roll(x, rows - 1, axis=0)
    kk_n, w_n, bb_n = nxt(kk), nxt(w), nxt(bb)
    cc = _seg_dot(jnp.concatenate([bb * kk_n, kmod * kk_n, bb * r, kmod * r], axis=0).astype(BF16), seg)
    cbk, ckk = cc[0:rows], cc[rows:2 * rows]
    cbr_scr[...] = cc[2 * rows:3 * rows].reshape(nb, tb, D_A)
    ckr_scr[...] = cc[3 * rows:4 * rows].reshape(nb, tb, D_A)
    def put(scr, arr):
        for hp in range(n_hp):
            scr[:, hp] = arr[:, hp * 128:(hp + 1) * 128].reshape(nb, tb, 128)

    put(nk_scr, -kk)
    put(p2_scr, -(w * kk_n))
    put(q1_scr, w * r)
    put(rp_scr, jnp.concatenate(rps, axis=0))
    put(ya1_scr, bb * w_n - cbk * bb_n)
    put(b_scr, bb)
    put(yc1_scr, kmod * w_n - ckk * bb_n)
    put(k_scr, kmod)
    put(w12_scr, w * w_n)
    put(v_scr, v)
    for hp in range(n_hp):
        vsw_scr[:, hp] = pltpu.roll(v[:, hp * 128:(hp + 1) * 128], 64, axis=1).reshape(nb, tb, 128)

    row8 = lax.broadcasted_iota(jnp.int32, (8, 128), 0)
    lane8 = lax.broadcasted_iota(jnp.int32, (8, 128), 1)
    pair8 = lax.shift_right_logical(row8, 1)
    half8 = ((((row8 & 1) == 0) & (lane8 < 64)) | (((row8 & 1) == 1) & (lane8 >= 64))).astype(F32)
    mk3 = ((row8 >= 6) & ((((row8 & 1) == 0) & (lane8 < 64)) | (((row8 & 1) == 1) & (lane8 >= 64)))).astype(F32)
    row64 = lax.broadcasted_iota(jnp.int32, (8, 64), 0)

    def pair_tile(q0, q1, q2, q3):
        return jnp.where(pair8 == 0, q0, jnp.where(pair8 == 1, q1, jnp.where(pair8 == 2, q2, q3))) * half8

    chains = [(b, hp) for b in range(nb) for hp in range(n_hp)]
    nt_dims = (((1,), (1,)), ((), ()))
    tn_dims = (((0,), (0,)), ((), ()))
    grp = min(16, tb)
    n_pair = grp // 2

    def natural_rows(rows_h0, rows_h1):
        pad = [jnp.zeros((8 - n_pair, N_A), F32)] if n_pair < 8 else []
        return jnp.concatenate([jnp.concatenate(rows_h0 + pad, axis=0),
                                jnp.concatenate(rows_h1 + pad, axis=0)], axis=1)

    def group(tg, carry):
        t0 = tg * grp

        def row(ref, c, t, n=8):
            return ref[c[0], c[1], pl.ds(t0 + t, n, stride=0), :]

        out = {c: [[] for _ in range(6)] for c in chains}
        for pi in range(n_pair):
            i = 2 * pi
            reds = {}
            for c in chains:
                a_mat = pair_tile(row(nk_scr, c, i), row(p2_scr, c, i), row(q1_scr, c, i), row(rp_scr, c, i))
                reds[c] = lax.dot_general(a_mat.astype(BF16), s_scr[c[0], c[1]].astype(BF16), nt_dims,
                                          preferred_element_type=F32)
            xs = {}
            for c in chains:
                x_mat = jnp.where(
                    row64 < 4, reds[c],
                    jnp.where(row64 == 4, row(v_scr, c, i)[:, :64],
                              jnp.where(row64 == 5, row(vsw_scr, c, i)[:, :64],
                                        jnp.where(row64 == 6, row(v_scr, c, i + 1)[:, :64],
                                                  row(vsw_scr, c, i + 1)[:, :64]))))
                xs[c] = x_mat.astype(BF16)
            for c in chains:
                y_mat = pair_tile(row(ya1_scr, c, i), row(b_scr, c, i + 1), row(yc1_scr, c, i), row(k_scr, c, i + 1))
                d_s = lax.dot_general(xs[c], y_mat.astype(BF16), tn_dims, preferred_element_type=F32)
                s_scr[c[0], c[1]] = s_scr[c[0], c[1]] * row(w12_scr, c, i, N_A) + d_s
                for slot, red_row in enumerate((0, 1, 4, 5, 6, 7)):
                    out[c][slot].append(reds[c][red_row:red_row + 1, :])
        even0 = pl.multiple_of(tg * grp, 8)
        for c in chains:
            o = out[c]
            sa_scr[c[0], c[1], pl.ds(even0, 8, stride=2), :] = natural_rows(o[0], o[1])
            q_scr[c[0], c[1], pl.ds(even0, 8, stride=2), :] = natural_rows(o[2], o[3])
            yo_scr[c[0], c[1], pl.ds(even0 + 8, 8, stride=2), :] = natural_rows(o[4], o[5])
        return carry

    lax.fori_loop(0, tb // grp, group, 0)

    for (b, hp) in chains:
        rl = r_last[b][:, hp * 128:(hp + 1) * 128]
        a_mat = rl * mk3
        red = lax.dot_general(a_mat.astype(BF16), s_scr[b, hp].astype(BF16), nt_dims, preferred_element_type=F32)
        y_last = jnp.concatenate([red[6:7, :], red[7:8, :]], axis=1)
        yo_scr[b, hp, pl.ds(tb + 8, 8), :] = jnp.broadcast_to(y_last, (8, 128))

    def slab(scr, first):
        return jnp.concatenate(
            [jnp.concatenate([scr[b, hp, pl.ds(first, tb), :] for hp in range(n_hp)], axis=1) for b in range(nb)],
            axis=0)

    y_even = (slab(q_scr, 0) + slab(sa_scr, 0) * cbr_scr[...].reshape(rows, D_A)
              + slab(v_scr, 0) * ckr_scr[...].reshape(rows, D_A))
    parity = lax.broadcasted_iota(jnp.int32, (rows, D_A), 0) & 1
    y = jnp.where(parity == 0, y_even, slab(yo_scr, 9))
    mu = _seg_sum(y, seg) * (1.0 / N_A)
    d = y - mu
    var = _seg_sum(d * d, seg) * (1.0 / N_A)
    yn = d * lax.rsqrt(var + GN_EPS) * gng_ref[...] + gnb_ref[...]
    ya_ref[...] = ((yn.reshape(nb, tb, D_A) + bonus_scr[...]) * g_scr[...]).astype(ya_ref.dtype)

    @pl.when(j == pl.num_programs(1) - 1)
    def _():
        own = st_ref if n_alias else st_ref.at[0]
        for b in range(nb):
            for hp in range(n_hp):
                s_pair = s_scr[b, hp]
                own[b, 2 * hp] = s_pair[:, :N_A]
                own[b, 2 * hp + 1] = s_pair[:, N_A:]
        if not n_alias:
            st_ref[1:] = jnp.zeros((DEPTH - 1,) + tuple(st_ref.shape[1:]), F32)
        so_ref[...] = carry_scr[...]


def _rwkv(p3, shift_prev, wkv_all, l, wkv_out_prev, wts, nb, tb, y_dtype):
    bsz, t, _ = p3.shape
    n_hp = H_A // 2
    full = lambda shape: pl.BlockSpec(shape, lambda i, j: (0,) * len(shape))
    blk = lambda: pltpu.VMEM((nb, tb, D_A), F32)
    n_alias = 0 if wkv_out_prev is None else 1
    kern = functools.partial(_rwkv_kernel, nb=nb, tb=tb, n_alias=n_alias)
    state_spec = pl.BlockSpec((None, nb, H_A, N_A, N_A), lambda i, j: (l, i, 0, 0, 0))
    in_specs = [
        pl.BlockSpec((nb, tb, RWKV_COLS), lambda i, j: (i, j, 0)),
        pl.BlockSpec((nb, 1, RWKV_COLS), lambda i, j: (i, 0, 0)),
        state_spec,
        full((1, RWKV_COLS)), full((1, D_A)), full((128, 2 * D_A)), full((1, D_A)), full((R_G, D_A)),
        full((1, D_A)), full((1, D_A)), full((1, D_A)), full((1, D_A)), full((1, D_A)), full((D_A // 2, D_A // 2)),
    ]
    args = [p3, shift_prev, wkv_all, *wts]
    aliases = {}
    out_state_spec = pl.BlockSpec((DEPTH, nb, H_A, N_A, N_A), lambda i, j: (0, i, 0, 0, 0))
    if n_alias:
        in_specs.append(pl.BlockSpec(memory_space=pl.ANY))
        args.append(wkv_out_prev)
        aliases = {len(args) - 1: 2}
        out_state_spec = state_spec
    return pl.pallas_call(
        kern,
        grid=(bsz // nb, t // tb),
        in_specs=in_specs,
        out_specs=[
            pl.BlockSpec((nb, tb, D_A), lambda i, j: (i, j, 0)),
            pl.BlockSpec((nb, 1, RWKV_COLS), lambda i, j: (i, 0, 0)),
            out_state_spec,
        ],
        out_shape=[
            jax.ShapeDtypeStruct((bsz, t, D_A), y_dtype),
            jax.ShapeDtypeStruct((bsz, 1, RWKV_COLS), F32),
            jax.ShapeDtypeStruct((DEPTH, bsz, H_A, N_A, N_A), F32),
        ],
        scratch_shapes=[
            pltpu.VMEM((nb, n_hp, N_A, 128), F32),
            pltpu.VMEM((nb, 1, RWKV_COLS), F32),
            *[pltpu.VMEM((nb, n_hp, tb, 2 * N_A), F32) for _ in range(11)],
            *[blk() for _ in range(4)],
            *[pltpu.VMEM((nb, n_hp, tb + 24, 2 * N_A), F32) for _ in range(3)],
        ],
        input_output_aliases=aliases,
        compiler_params=pltpu.CompilerParams(
            dimension_semantics=("parallel", "arbitrary"), vmem_limit_bytes=VMEM_LIMIT),
        name="rwkv7",
    )(*args)


def _mlstm_kernel(*refs, nb, tb, lc, n_alias):
    (pm_ref, pg_ref, cp_ref, c0_ref, n0_ref, m0_ref, cw_ref, cb_ref, gb_ref, mhg_ref) = refs[:10]
    (yb_ref, co_ref, ct_ref, nt_ref, mt_ref, x_scr, c_scr, n_scr, m_scr) = refs[10 + n_alias:]
    j = pl.program_id(1)
    hi = lax.Precision.HIGHEST
    nch = tb // lc
    lc_shift = lc.bit_length() - 1

    @pl.when(j == 0)
    def _():
        c_scr[...] = c0_ref[...]
        n_scr[...] = n0_ref[...]
        m_scr[...] = jnp.broadcast_to(m0_ref[...], m_scr.shape)
        x_scr[:, 5:8, :] = cp_ref[...]

    rr = lax.broadcasted_iota(jnp.int32, (lc, lc), 0)
    cc = lax.broadcasted_iota(jnp.int32, (lc, lc), 1)
    causal = rr >= cc
    rb = lax.broadcasted_iota(jnp.int32, (tb, tb), 0)
    cb = lax.broadcasted_iota(jnp.int32, (tb, tb), 1)
    tri_blk = ((rb >= cb) & (lax.shift_right_logical(rb, lc_shift) == lax.shift_right_logical(cb, lc_shift))
               ).astype(F32)
    lane_g = lax.broadcasted_iota(jnp.int32, (tb, 128), 1)
    tn_dims = (((0,), (0,)), ((), ()))
    nt_dims = (((1,), (1,)), ((), ()))

    units = [(b, c, h) for b in range(nb) for c in range(nch) for h in range(H_B)]
    q_u, k_u, v_u, o_u = {}, {}, {}, {}
    li_col, li_row, b_col, b_row, b_l = {}, {}, {}, {}, {}

    for b in range(nb):
        x = pm_ref[b, :, 0:2 * D_B]
        x_scr[b, pl.ds(8, tb), :] = x
        conv = cb_ref[...] + x * cw_ref[3:4, :]
        for s in range(1, CONV_W):
            conv = conv + x_scr[b, pl.ds(8 - s, tb), :] * cw_ref[3 - s:4 - s, :]
        x_scr[b, 5:8, :] = x_scr[b, pl.ds(8 + tb - 3, 3), :]
        sc = conv * _sigmoid(conv)
        q_all = sc[:, :D_B]
        k_all = sc[:, D_B:] * (DK_B ** -0.5)
        v_all = pm_ref[b, :, 2 * D_B:3 * D_B]
        o_all = pm_ref[b, :, 3 * D_B:4 * D_B]
        gp = pg_ref[b, :, 0:128] + gb_ref[...]
        gates = jnp.where(lane_g < H_B, gp, jnp.minimum(gp, 0.0) - jnp.log1p(jnp.exp(-jnp.abs(gp))))
        csum_col = jnp.dot(tri_blk, gates, precision=hi, preferred_element_type=F32)
        gt = gates.T
        csum_row = lax.dot_general(gt, tri_blk, nt_dims, precision=hi, preferred_element_type=F32)
        for c in range(nch):
            rs = slice(c * lc, (c + 1) * lc)
            for h in range(H_B):
                u = (b, c, h)
                hs = slice(h * DK_B, (h + 1) * DK_B)
                q_u[u], k_u[u], v_u[u], o_u[u] = q_all[rs, hs], k_all[rs, hs], v_all[rs, hs], o_all[rs, hs]
                li_col[u] = gates[rs, h:h + 1]
                li_row[u] = gt[h:h + 1, rs]
                b_col[u] = csum_col[rs, H_B + h:H_B + h + 1]
                b_row[u] = csum_row[H_B + h:H_B + h + 1, rs]
                b_l[u] = csum_col[(c + 1) * lc - 1:(c + 1) * lc, H_B + h:H_B + h + 1]

    rep = lambda col: jnp.broadcast_to(col, (lc, DK_B))
    bc = {u: rep(b_col[u]) for u in units}
    lic = {u: rep(li_col[u]) for u in units}

    last_max = {u: jnp.max(b_l[u] - b_row[u] + li_row[u], axis=-1, keepdims=True) for u in units}
    m_prev, m_new = {}, {}
    for b in range(nb):
        for h in range(H_B):
            m_p = m_scr[b, h][:, 0:1]
            for c in range(nch):
                u = (b, c, h)
                m_prev[u] = m_p
                m_p = jnp.maximum(b_l[u] + m_p, last_max[u])
                m_new[u] = m_p
            m_scr[b, h] = jnp.broadcast_to(m_p, (1, 128))

    dmat = {u: jnp.where(causal, bc[u][:, :lc] - b_row[u] + li_row[u], -jnp.inf) for u in units}
    row_max = {u: rep(jnp.max(dmat[u], axis=-1, keepdims=True)) for u in units}
    m_t, g_inter, e_mat = {}, {}, {}
    for u in units:
        inter = bc[u] + m_prev[u]
        m_t[u] = jnp.maximum(inter, row_max[u])
        g_inter[u] = jnp.exp(inter - m_t[u])
        e_mat[u] = jnp.exp(dmat[u] - m_t[u][:, :lc])

    qk = {u: lax.dot_general(q_u[u].astype(BF16), k_u[u].astype(BF16), nt_dims, preferred_element_type=F32)
          for u in units}
    s_mat = {u: qk[u] * e_mat[u] for u in units}
    sv = {u: _bdot(s_mat[u], v_u[u]) for u in units}
    s_sum = {u: rep(jnp.sum(s_mat[u], axis=-1, keepdims=True)) for u in units}
    kw = {u: k_u[u] * jnp.exp(b_l[u] - bc[u] + lic[u] - m_new[u]) for u in units}
    kwv = {u: lax.dot_general(kw[u].astype(BF16), v_u[u].astype(BF16), tn_dims, preferred_element_type=F32)
           for u in units}

    q_c, qn_prod = {}, {}
    c_cur = {(b, h): c_scr[b, h] for b in range(nb) for h in range(H_B)}
    n_cur = {(b, h): n_scr[b, h] for b in range(nb) for h in range(H_B)}
    for c in range(nch):
        for b in range(nb):
            for h in range(H_B):
                u = (b, c, h)
                q_c[u] = _bdot(q_u[u], c_cur[(b, h)])
                qn_prod[u] = q_u[u] * n_cur[(b, h)]
                dec = jnp.exp(b_l[u] + m_prev[u] - m_new[u])
                c_cur[(b, h)] = dec * c_cur[(b, h)] + kwv[u]
                n_cur[(b, h)] = dec * n_cur[(b, h)] + jnp.sum(kw[u], axis=0, keepdims=True)
    for b in range(nb):
        for h in range(H_B):
            c_scr[b, h] = c_cur[(b, h)]
            n_scr[b, h] = n_cur[(b, h)]
    q_n = {u: rep(jnp.sum(qn_prod[u], axis=-1, keepdims=True)) for u in units}

    hh = {}
    for u in units:
        num = g_inter[u] * q_c[u] + sv[u]
        den = g_inter[u] * q_n[u] + s_sum[u]
        hh[u] = num / jnp.maximum(jnp.abs(den), jnp.exp(-m_t[u]))
    mu = {u: rep(jnp.sum(hh[u], axis=-1, keepdims=True)) * (1.0 / DK_B) for u in units}
    dev = {u: hh[u] - mu[u] for u in units}
    var = {u: rep(jnp.sum(dev[u] * dev[u], axis=-1, keepdims=True)) * (1.0 / DK_B) for u in units}
    for u in units:
        b, c, h = u
        hs = slice(h * DK_B, (h + 1) * DK_B)
        hn = dev[u] * lax.rsqrt(var[u] + MH_EPS) * mhg_ref[:, hs]
        yb_ref[b, c * lc:(c + 1) * lc, hs] = (_sigmoid(o_u[u]) * hn).astype(yb_ref.dtype)

    @pl.when(j == pl.num_programs(1) - 1)
    def _():
        if n_alias:
            ct_ref[...] = c_scr[...]
        else:
            ct_ref[0] = c_scr[...]
            ct_ref[1:] = jnp.zeros((DEPTH - 1,) + tuple(ct_ref.shape[1:]), F32)
        nt_ref[...] = n_scr[...]
        mt_ref[...] = m_scr[:, :, :, 0:1]
        co_ref[...] = x_scr[:, 5:8, :]


def _mlstm(p3, conv_prev, c_all, l, c_out_prev, n0, m0, wts, nb, tb, lc, y_dtype):
    bsz, t, _ = p3.shape
    full = lambda shape: pl.BlockSpec(shape, lambda i, j: (0,) * len(shape))
    n_alias = 0 if c_out_prev is None else 1
    kern = functools.partial(_mlstm_kernel, nb=nb, tb=tb, lc=lc, n_alias=n_alias)
    gate_blk = RWKV_COLS // GATE_COLS
    c_spec = pl.BlockSpec((None, nb, H_B, DK_B, DK_B), lambda i, j: (l, i, 0, 0, 0))
    in_specs = [
        pl.BlockSpec((nb, tb, ML_COLS), lambda i, j: (i, j, 1)),
        pl.BlockSpec((nb, tb, GATE_COLS), lambda i, j: (i, j, gate_blk)),
        pl.BlockSpec((nb, CONV_W - 1, 2 * D_B), lambda i, j: (i, 0, 0)),
        c_spec,
        pl.BlockSpec((nb, H_B, 1, DK_B), lambda i, j: (i, 0, 0, 0)),
        pl.BlockSpec((nb, H_B, 1, 1), lambda i, j: (i, 0, 0, 0)),
        full((CONV_W, 2 * D_B)), full((1, 2 * D_B)), full((1, 128)), full((1, D_B)),
    ]
    args = [p3, p3, conv_prev, c_all, n0, m0, *wts]
    aliases = {}
    out_c_spec = pl.BlockSpec((DEPTH, nb, H_B, DK_B, DK_B), lambda i, j: (0, i, 0, 0, 0))
    if n_alias:
        in_specs.append(pl.BlockSpec(memory_space=pl.ANY))
        args.append(c_out_prev)
        aliases = {len(args) - 1: 2}
        out_c_spec = c_spec
    return pl.pallas_call(
        kern,
        grid=(bsz // nb, t // tb),
        in_specs=in_specs,
        out_specs=[
            pl.BlockSpec((nb, tb, D_B), lambda i, j: (i, j, 0)),
            pl.BlockSpec((nb, CONV_W - 1, 2 * D_B), lambda i, j: (i, 0, 0)),
            out_c_spec,
            pl.BlockSpec((nb, H_B, 1, DK_B), lambda i, j: (i, 0, 0, 0)),
            pl.BlockSpec((nb, H_B, 1, 1), lambda i, j: (i, 0, 0, 0)),
        ],
        out_shape=[
            jax.ShapeDtypeStruct((bsz, t, D_B), y_dtype),
            jax.ShapeDtypeStruct((bsz, CONV_W - 1, 2 * D_B), F32),
            jax.ShapeDtypeStruct((DEPTH, bsz, H_B, DK_B, DK_B), F32),
            jax.ShapeDtypeStruct((bsz, H_B, 1, DK_B), F32),
            jax.ShapeDtypeStruct((bsz, H_B, 1, 1), F32),
        ],
        scratch_shapes=[
            pltpu.VMEM((nb, tb + 8, 2 * D_B), F32),
            pltpu.VMEM((nb, H_B, DK_B, DK_B), F32),
            pltpu.VMEM((nb, H_B, 1, DK_B), F32),
            pltpu.VMEM((nb, H_B, 1, 128), F32),
        ],
        input_output_aliases=aliases,
        compiler_params=pltpu.CompilerParams(
            dimension_semantics=("parallel", "arbitrary"), vmem_limit_bytes=VMEM_LIMIT),
        name="mlstm",
    )(*args)


def _layer_weights(l, w):
    bf = lambda a: a.astype(BF16)
    row = lambda a: a.reshape(1, -1)
    w_in = w['w_in'][l]
    w_gates = jnp.pad(w_in[:, RWKV_COLS + ML_COLS:], ((0, 0), (0, GATE_COLS - 2 * H_B)))
    w_cat = jnp.concatenate([w_in[:, :RWKV_COLS], w_gates, w_in[:, RWKV_COLS:RWKV_COLS + ML_COLS]], axis=1)
    zero = jnp.zeros((R_W, D_A), F32)
    w2a = jnp.concatenate([jnp.concatenate([w['w2'][l], zero], axis=1),
                           jnp.concatenate([zero, w['a2'][l]], axis=1)], axis=0)
    ids = jnp.arange(D_A // 2) // N_A
    seg = (ids[:, None] == ids[None, :]).astype(BF16)
    gate_bias = jnp.pad(jnp.concatenate([w['i_bias'][l], w['f_bias'][l]]), (0, 128 - 2 * H_B)).reshape(1, 128)
    return dict(
        ln1=(row(w['ln1_g'][l]), row(w['ln1_b'][l])),
        ln3=(row(w['ln3_g'][l]), row(w['ln3_b'][l])),
        w_cat=bf(w_cat),
        rwkv=(row(w['mu_shift'][l]), row(w['w0'][l]), bf(w2a), row(w['a0'][l]), bf(w['g2'][l]), row(w['k_k'][l]),
              row(w['k_a'][l]), row(w['r_k'][l]), row(w['gn_g'][l]), row(w['gn_b'][l]), seg),
        mlstm=(w['conv_w'][l], row(w['conv_b'][l]), gate_bias, row(w['mh_g'][l])),
        out=(bf(w['w_out'][l][:D_A]), bf(w['w_out'][l][D_A:]), row(w['ln2_g'][l]), row(w['ln2_b'][l])),
    )


def _trunk(x, states, lw, ffn_w, nb_r, tb_r, nb_m, tb_m):
    bsz, t, _ = x.shape
    n = bsz * t
    tm_ffn, tm_out, tm_proj = min(FFN_TILE_ROWS, n), min(OUT_PROJ_TILE_ROWS, n), min(PROJ_TILE_ROWS, n)
    y_dtype = BF16 if t % 16 == 0 else F32
    lc = math.gcd(t, CHUNK)
    xf = x.reshape(n, D_MODEL)
    st_shift, st_wkv, st_conv, st_c, st_n, st_m = states
    new = [[] for _ in range(4)]
    wkv_out, c_out = None, None
    for l in range(DEPTH):
        wl = lw[l]
        x1 = _ffn_ln(xf, *ffn_w[0], *wl['ln1'], l=l, tm=tm_ffn)
        p3 = _in_proj(x1, wl['w_cat'], tm=tm_proj).reshape(bsz, t, P_COLS)
        ya, shift, wkv_out = _rwkv(p3, st_shift[l].reshape(bsz, 1, RWKV_COLS), st_wkv, l, wkv_out,
                                   wl['rwkv'], nb_r, tb_r, y_dtype)
        yb, conv, c_out, n_t, m_t = _mlstm(p3, st_conv[l], st_c, l, c_out,
                                           st_n[l].reshape(bsz, H_B, 1, DK_B), st_m[l].reshape(bsz, H_B, 1, 1),
                                           wl['mlstm'], nb_m, tb_m, lc, y_dtype)
        x2 = _out_proj_ln(x1, ya.reshape(n, D_A), yb.reshape(n, D_B), *wl['out'], tm=tm_out)
        xf = _ffn_ln(x2, *ffn_w[1], *wl['ln3'], l=l, tm=tm_ffn)
        for idx, s in enumerate((shift.reshape(bsz, RWKV_COLS), conv, n_t.reshape(bsz, H_B, DK_B),
                                 m_t.reshape(bsz, H_B))):
            new[idx].append(s)
    shift_o, conv_o, n_o, m_o = [jnp.stack(s) for s in new]
    return xf.reshape(bsz, t, D_MODEL), [shift_o, wkv_out, conv_o, c_out, n_o, m_o]


def kernel(x_prompt, x_sample, state_shift, state_wkv, state_conv, state_C, state_n, state_m,
           ffn1_wg, ffn1_wu, ffn1_wd, ln1_g, ln1_b, w_in, mu_shift, w0, w2, a0, a2, g2, k_k, k_a, r_k,
           gn_g, gn_b, conv_w, conv_b, i_bias, f_bias, mh_g, w_out, ln2_g, ln2_b,
           ffn2_wg, ffn2_wu, ffn2_wd, ln3_g, ln3_b):
    w = dict(ffn1_wg=ffn1_wg, ffn1_wu=ffn1_wu, ffn1_wd=ffn1_wd, ln1_g=ln1_g, ln1_b=ln1_b, w_in=w_in,
             mu_shift=mu_shift, w0=w0, w2=w2, a0=a0, a2=a2, g2=g2, k_k=k_k, k_a=k_a, r_k=r_k,
             gn_g=gn_g, gn_b=gn_b, conv_w=conv_w, conv_b=conv_b, i_bias=i_bias, f_bias=f_bias,
             mh_g=mh_g, w_out=w_out, ln2_g=ln2_g, ln2_b=ln2_b, ffn2_wg=ffn2_wg, ffn2_wu=ffn2_wu,
             ffn2_wd=ffn2_wd, ln3_g=ln3_g, ln3_b=ln3_b)
    lw = [_layer_weights(l, w) for l in range(DEPTH)]
    ffn_w = ((ffn1_wg, ffn1_wu, ffn1_wd), (ffn2_wg, ffn2_wu, ffn2_wd))
    bp = x_prompt.shape[0]
    dt = x_prompt.dtype
    init = [jnp.zeros((DEPTH, bp, RWKV_COLS), dt),
            jnp.zeros((DEPTH, bp, H_A, N_A, N_A), dt),
            jnp.zeros((DEPTH, bp, CONV_W - 1, 2 * D_B), dt),
            jnp.zeros((DEPTH, bp, H_B, DK_B, DK_B), dt),
            jnp.zeros((DEPTH, bp, H_B, DK_B), dt),
            jnp.zeros((DEPTH, bp, H_B), dt)]
    y_p, ps = _trunk(x_prompt, init, lw, ffn_w, nb_r=8, tb_r=64, nb_m=1, tb_m=256)
    y_s, ss = _trunk(x_sample, [state_shift, state_wkv, state_conv, state_C, state_n, state_m], lw, ffn_w,
                     nb_r=16, tb_r=8, nb_m=16, tb_m=8)
    return (y_p, y_s, *ps, *ss)
```

```python
import functools
import math

import jax
import jax.numpy as jnp
from jax import lax
from jax.experimental import pallas as pl
from jax.experimental.pallas import tpu as pltpu

D_MODEL = 1024
DEPTH = 2
D_A = 512
N_A = 64
H_A = 8
D_B = 512
H_B = 4
DK_B = 128
R_W = 64
R_A = 64
R_G = 128
RWKV_COLS = 3 * D_A + R_W + R_A + R_G
CONV_W = 4
CHUNK = 64
D_FF = 2816
ALPHA = (2.0 * DEPTH) ** 0.25
LN_EPS = 1e-5
GN_EPS = 64e-5
MH_EPS = 1e-6

GATE_COLS = 256
ML_COLS = 4 * D_B
P_COLS = RWKV_COLS + GATE_COLS + ML_COLS
VMEM_LIMIT = 56 * 1024 * 1024
FFN_COL_BLOCK = 256
FFN_ROW_CHUNK = 512
FFN_LAST_CHUNK = 256
FFN_TILE_ROWS = 2048
OUT_PROJ_TILE_ROWS = 2048
PROJ_TILE_ROWS = 512
RWKV_ROWS_PER_TRIP = 32

F32 = jnp.float32
BF16 = jnp.bfloat16


def _bdot(a, b):
    return jnp.dot(a.astype(BF16), b.astype(BF16), preferred_element_type=F32)


def _seg_dot(x_bf16, seg_half):
    h = seg_half.shape[0]
    return jnp.concatenate([jnp.dot(x_bf16[:, :h], seg_half, preferred_element_type=F32),
                            jnp.dot(x_bf16[:, h:], seg_half, preferred_element_type=F32)], axis=1)


def _seg_sum(x, seg_half):
    hi = x.astype(BF16)
    lo = (x - hi.astype(F32)).astype(BF16)
    return _seg_dot(hi, seg_half) + _seg_dot(lo, seg_half)


def _layer_norm_rows(y, g, b):
    mu = jnp.mean(y, axis=-1, keepdims=True)
    d = y - mu
    var = jnp.mean(d * d, axis=-1, keepdims=True)
    return d * lax.rsqrt(var + LN_EPS) * g + b


def _sigmoid(x):
    return 1.0 / (1.0 + jnp.exp(-x))


def _ffn_ln_kernel(x_ref, wg_ref, wu_ref, wd_ref, g_ref, b_ref, o_ref, xb_scr):
    j = pl.program_id(1)
    last = pl.num_programs(1) - 1
    tm = x_ref.shape[0]

    def row_chunks(size):
        return [pl.ds(start, size) for start in range(0, tm, size)]

    def partial_ffn(xb, wg, wu, wd):
        hg = jnp.dot(xb, wg, preferred_element_type=F32)
        hu = jnp.dot(xb, wu, preferred_element_type=F32)
        h = (hg * _sigmoid(hg)) * hu
        return jnp.dot(h.astype(BF16), wd, preferred_element_type=F32)

    def weights():
        return wg_ref[...].astype(BF16), wu_ref[...].astype(BF16), wd_ref[...].astype(BF16)

    @pl.when(j == 0)
    def _():
        wg, wu, wd = weights()
        for rows in row_chunks(FFN_ROW_CHUNK):
            xb = x_ref[rows, :].astype(BF16)
            xb_scr[rows, :] = xb
            o_ref[rows, :] = partial_ffn(xb, wg, wu, wd)

    @pl.when((j > 0) & (j < last))
    def _():
        wg, wu, wd = weights()
        for rows in row_chunks(FFN_ROW_CHUNK):
            o_ref[rows, :] += partial_ffn(xb_scr[rows, :], wg, wu, wd)

    @pl.when(j == last)
    def _():
        wg, wu, wd = weights()
        for rows in row_chunks(FFN_LAST_CHUNK):
            acc = o_ref[rows, :] + partial_ffn(xb_scr[rows, :], wg, wu, wd)
            o_ref[rows, :] = _layer_norm_rows(ALPHA * x_ref[rows, :] + 0.5 * acc, g_ref[...], b_ref[...])


def _ffn_ln(x, wg, wu, wd, g, b, l, tm):
    n = x.shape[0]
    const = lambda shape: pl.BlockSpec(shape, lambda i, j: (0, 0))
    return pl.pallas_call(
        _ffn_ln_kernel,
        grid=(n // tm, D_FF // FFN_COL_BLOCK),
        in_specs=[
            pl.BlockSpec((tm, D_MODEL), lambda i, j: (i, 0)),
            pl.BlockSpec((None, D_MODEL, FFN_COL_BLOCK), lambda i, j: (l, 0, j)),
            pl.BlockSpec((None, D_MODEL, FFN_COL_BLOCK), lambda i, j: (l, 0, j)),
            pl.BlockSpec((None, FFN_COL_BLOCK, D_MODEL), lambda i, j: (l, j, 0)),
            const((1, D_MODEL)), const((1, D_MODEL)),
        ],
        out_specs=pl.BlockSpec((tm, D_MODEL), lambda i, j: (i, 0)),
        out_shape=jax.ShapeDtypeStruct((n, D_MODEL), F32),
        scratch_shapes=[pltpu.VMEM((tm, D_MODEL), BF16)],
        compiler_params=pltpu.CompilerParams(
            dimension_semantics=("parallel", "arbitrary"), vmem_limit_bytes=VMEM_LIMIT),
        name="ffn_ln",
    )(x, wg, wu, wd, g, b)


def _in_proj_kernel(x_ref, w_ref, o_ref):
    o_ref[...] = jnp.dot(x_ref[...].astype(BF16), w_ref[...], preferred_element_type=F32)


def _in_proj(x, w, tm):
    n = x.shape[0]
    return pl.pallas_call(
        _in_proj_kernel,
        grid=(n // tm,),
        in_specs=[
            pl.BlockSpec((tm, D_MODEL), lambda i: (i, 0)),
            pl.BlockSpec((D_MODEL, P_COLS), lambda i: (0, 0)),
        ],
        out_specs=pl.BlockSpec((tm, P_COLS), lambda i: (i, 0)),
        out_shape=jax.ShapeDtypeStruct((n, P_COLS), F32),
        compiler_params=pltpu.CompilerParams(
            dimension_semantics=("parallel",), vmem_limit_bytes=VMEM_LIMIT),
        name="in_proj",
    )(x, w)


def _out_proj_ln_kernel(x_ref, ya_ref, yb_ref, wa_ref, wb_ref, g_ref, b_ref, o_ref):
    mix = (jnp.dot(ya_ref[...].astype(BF16), wa_ref[...], preferred_element_type=F32)
           + jnp.dot(yb_ref[...].astype(BF16), wb_ref[...], preferred_element_type=F32))
    o_ref[...] = _layer_norm_rows(ALPHA * x_ref[...] + mix, g_ref[...], b_ref[...])


def _out_proj_ln(x, ya, yb, wa, wb, g, b, tm):
    n = x.shape[0]
    return pl.pallas_call(
        _out_proj_ln_kernel,
        grid=(n // tm,),
        in_specs=[
            pl.BlockSpec((tm, D_MODEL), lambda i: (i, 0)),
            pl.BlockSpec((tm, D_A), lambda i: (i, 0)),
            pl.BlockSpec((tm, D_B), lambda i: (i, 0)),
            pl.BlockSpec((D_A, D_MODEL), lambda i: (0, 0)),
            pl.BlockSpec((D_B, D_MODEL), lambda i: (0, 0)),
            pl.BlockSpec((1, D_MODEL), lambda i: (0, 0)),
            pl.BlockSpec((1, D_MODEL), lambda i: (0, 0)),
        ],
        out_specs=pl.BlockSpec((tm, D_MODEL), lambda i: (i, 0)),
        out_shape=jax.ShapeDtypeStruct((n, D_MODEL), F32),
        compiler_params=pltpu.CompilerParams(
            dimension_semantics=("parallel",), vmem_limit_bytes=VMEM_LIMIT),
        name="out_proj_ln",
    )(x, ya, yb, wa, wb, g, b)


def _rwkv_kernel(*refs, nb, tb, n_alias):
    (p_ref, sp_ref, s0_ref, mu_ref, w0_ref, w2a_ref, a0_ref, g2_ref, kk_ref, ka_ref, rk_ref,
     gng_ref, gnb_ref, seg_ref) = refs[:14]
    (ya_ref, so_ref, st_ref,
     s_scr, carry_scr, nk_scr, p2_scr, q1_scr, rp_scr, v_scr, vsw_scr, ya1_scr, b_scr, yc1_scr, k_scr, w12_scr,
     cbr_scr, ckr_scr, bonus_scr, g_scr, sa_scr, q_scr, yo_scr) = refs[14 + n_alias:]
    j = pl.program_id(1)
    n_hp = H_A // 2

    @pl.when(j == 0)
    def _():
        for b in range(nb):
            for hp in range(n_hp):
                s_scr[b, hp] = jnp.concatenate([s0_ref[b, 2 * hp], s0_ref[b, 2 * hp + 1]], axis=-1)
        carry_scr[...] = sp_ref[...]
        sa_scr[...] = jnp.zeros_like(sa_scr)
        q_scr[...] = jnp.zeros_like(q_scr)
        yo_scr[...] = jnp.zeros_like(yo_scr)

    seg = seg_ref[...]
    rows = nb * tb
    lane128 = lax.broadcasted_iota(jnp.int32, (rows, 128), 1)
    rowid = lax.broadcasted_iota(jnp.int32, (tb, RWKV_COLS), 0)
    rowid_a = lax.broadcasted_iota(jnp.int32, (tb, D_A), 0)

    pms, rps, r_last = [], [], []
    for b in range(nb):
        p = p_ref[b]
        prev = jnp.where(rowid == 0, carry_scr[b], pltpu.roll(p, 1, axis=0))
        carry_scr[b] = p[tb - 1:tb, :]
        pm_b = p + (prev - p) * mu_ref[...]
        r_b = pm_b[:, 0:D_A]
        pms.append(pm_b)
        rps.append(jnp.where(rowid_a == 0, 0.0, pltpu.roll(r_b, 1, axis=0)))
        r_last.append(r_b[tb - 1:tb, :])
    pm = jnp.concatenate(pms, axis=0)
    r = pm[:, 0:D_A]
    k = pm[:, D_A:2 * D_A]
    v = pm[:, 2 * D_A:3 * D_A]
    z = pm[:, 3 * D_A:3 * D_A + 128]
    xg = pm[:, 3 * D_A + 128:RWKV_COLS]
    zt = jnp.where(lane128 < R_W, jnp.tanh(z), z)
    lr = _bdot(zt, w2a_ref[...])
    g_scr[...] = _bdot(_sigmoid(xg), g2_ref[...]).reshape(nb, tb, D_A)
    kk = k * kk_ref[...]
    ss = _seg_sum(kk * kk, seg)
    w = jnp.exp(-math.exp(-0.5) * _sigmoid(w0_ref[...] + lr[:, :D_A]))
    a = _sigmoid(a0_ref[...] + lr[:, D_A:])
    kk = kk * lax.rsqrt(jnp.maximum(ss, 1e-24))
    kmod = k * (1.0 + (a - 1.0) * ka_ref[...])
    bonus_scr[...] = (_seg_sum(r * kmod * rk_ref[...], seg) * v).reshape(nb, tb, D_A)
    bb = kk * a
    nxt = lambda x: pltpu.roll(x, rows - 1, axis=0)
    kk_n, w_n, bb_n = nxt(kk), nxt(w), nxt(bb)
    cc = _seg_dot(jnp.concatenate([bb * kk_n, kmod * kk_n, bb * r, kmod * r], axis=0).astype(BF16), seg)
    cbk, ckk = cc[0:rows], cc[rows:2 * rows]
    cbr_scr[...] = cc[2 * rows:3 * rows].reshape(nb, tb, D_A)
    ckr_scr[...] = cc[3 * rows:4 * rows].reshape(nb, tb, D_A)
    def put(scr, arr):
        for hp in range(n_hp):
            scr[:, hp] = arr[:, hp * 128:(hp + 1) * 128].reshape(nb, tb, 128)

    put(nk_scr, -kk)
    put(p2_scr, -(w * kk_n))
    put(q1_scr, w * r)
    put(rp_scr, jnp.concatenate(rps, axis=0))
    put(ya1_scr, bb * w_n - cbk * bb_n)
    put(b_scr, bb)
    put(yc1_scr, kmod * w_n - ckk * bb_n)
    put(k_scr, kmod)
    put(w12_scr, w * w_n)
    put(v_scr, v)
    for hp in range(n_hp):
        vsw_scr[:, hp] = pltpu.roll(v[:, hp * 128:(hp + 1) * 128], 64, axis=1).reshape(nb, tb, 128)

    row8 = lax.broadcasted_iota(jnp.int32, (8, 128), 0)
    lane8 = lax.broadcasted_iota(jnp.int32, (8, 128), 1)
    pair8 = lax.shift_right_logical(row8, 1)
    half8 = ((((row8 & 1) == 0) & (lane8 < 64)) | (((row8 & 1) == 1) & (lane8 >= 64))).astype(F32)
    mk3 = ((row8 >= 6) & ((((row8 & 1) == 0) & (lane8 < 64)) | (((row8 & 1) == 1) & (lane8 >= 64)))).astype(F32)
    row64 = lax.broadcasted_iota(jnp.int32, (8, 64), 0)

    def pair_tile(q0, q1, q2, q3):
        return jnp.where(pair8 == 0, q0, jnp.where(pair8 == 1, q1, jnp.where(pair8 == 2, q2, q3))) * half8

    chains = [(b, hp) for b in range(nb) for hp in range(n_hp)]
    nt_dims = (((1,), (1,)), ((), ()))
    tn_dims = (((0,), (0,)), ((), ()))
    grp = min(RWKV_ROWS_PER_TRIP, tb)
    n_pair = grp // 2

    def natural_rows(rows_h0, rows_h1):
        pad = [jnp.zeros((8 - len(rows_h0), N_A), F32)] if len(rows_h0) < 8 else []
        return jnp.concatenate([jnp.concatenate(rows_h0 + pad, axis=0),
                                jnp.concatenate(rows_h1 + pad, axis=0)], axis=1)

    def group(tg, carry):
        t0 = tg * grp

        def row(ref, c, t, n=8):
            return ref[c[0], c[1], pl.ds(t0 + t, n, stride=0), :]

        out = {c: [[] for _ in range(6)] for c in chains}
        for pi in range(n_pair):
            i = 2 * pi
            reds = {}
            for c in chains:
                a_mat = pair_tile(row(nk_scr, c, i), row(p2_scr, c, i), row(q1_scr, c, i), row(rp_scr, c, i))
                reds[c] = lax.dot_general(a_mat.astype(BF16), s_scr[c[0], c[1]].astype(BF16), nt_dims,
                                          preferred_element_type=F32)
            xs = {}
            for c in chains:
                x_mat = jnp.where(
                    row64 < 4, reds[c],
                    jnp.where(row64 == 4, row(v_scr, c, i)[:, :64],
                              jnp.where(row64 == 5, row(vsw_scr, c, i)[:, :64],
                                        jnp.where(row64 == 6, row(v_scr, c, i + 1)[:, :64],
                                                  row(vsw_scr, c, i + 1)[:, :64]))))
                xs[c] = x_mat.astype(BF16)
            for c in chains:
                y_mat = pair_tile(row(ya1_scr, c, i), row(b_scr, c, i + 1), row(yc1_scr, c, i), row(k_scr, c, i + 1))
                d_s = lax.dot_general(xs[c], y_mat.astype(BF16), tn_dims, preferred_element_type=F32)
                s_scr[c[0], c[1]] = s_scr[c[0], c[1]] * row(w12_scr, c, i, N_A) + d_s
                for slot, red_row in enumerate((0, 1, 4, 5, 6, 7)):
                    out[c][slot].append(reds[c][red_row:red_row + 1, :])
        for k in range(0, n_pair, 8):
            even0 = pl.multiple_of(tg * grp + 2 * k, 8)
            for c in chains:
                o = [rows_list[k:k + 8] for rows_list in out[c]]
                sa_scr[c[0], c[1], pl.ds(even0, 8, stride=2), :] = natural_rows(o[0], o[1])
                q_scr[c[0], c[1], pl.ds(even0, 8, stride=2), :] = natural_rows(o[2], o[3])
                yo_scr[c[0], c[1], pl.ds(even0 + 8, 8, stride=2), :] = natural_rows(o[4], o[5])
        return carry

    lax.fori_loop(0, tb // grp, group, 0)

    for (b, hp) in chains:
        rl = r_last[b][:, hp * 128:(hp + 1) * 128]
        a_mat = rl * mk3
        red = lax.dot_general(a_mat.astype(BF16), s_scr[b, hp].astype(BF16), nt_dims, preferred_element_type=F32)
        y_last = jnp.concatenate([red[6:7, :], red[7:8, :]], axis=1)
        yo_scr[b, hp, pl.ds(tb + 8, 8), :] = jnp.broadcast_to(y_last, (8, 128))

    def slab(scr, first):
        return jnp.concatenate(
            [jnp.concatenate([scr[b, hp, pl.ds(first, tb), :] for hp in range(n_hp)], axis=1) for b in range(nb)],
            axis=0)

    y_even = (slab(q_scr, 0) + slab(sa_scr, 0) * cbr_scr[...].reshape(rows, D_A)
              + slab(v_scr, 0) * ckr_scr[...].reshape(rows, D_A))
    parity = lax.broadcasted_iota(jnp.int32, (rows, D_A), 0) & 1
    y = jnp.where(parity == 0, y_even, slab(yo_scr, 9))
    mu = _seg_sum(y, seg) * (1.0 / N_A)
    d = y - mu
    var = _seg_sum(d * d, seg) * (1.0 / N_A)
    yn = d * lax.rsqrt(var + GN_EPS) * gng_ref[...] + gnb_ref[...]
    ya_ref[...] = ((yn.reshape(nb, tb, D_A) + bonus_scr[...]) * g_scr[...]).astype(ya_ref.dtype)

    @pl.when(j == pl.num_programs(1) - 1)
    def _():
        own = st_ref if n_alias else st_ref.at[0]
        for b in range(nb):
            for hp in range(n_hp):
                s_pair = s_scr[b, hp]
                own[b, 2 * hp] = s_pair[:, :N_A]
                own[b, 2 * hp + 1] = s_pair[:, N_A:]
        if not n_alias:
            st_ref[1:] = jnp.zeros((DEPTH - 1,) + tuple(st_ref.shape[1:]), F32)
        so_ref[...] = carry_scr[...]


def _rwkv(p3, shift_prev, wkv_all, l, wkv_out_prev, wts, nb, tb, y_dtype):
    bsz, t, _ = p3.shape
    n_hp = H_A // 2
    full = lambda shape: pl.BlockSpec(shape, lambda i, j: (0,) * len(shape))
    blk = lambda: pltpu.VMEM((nb, tb, D_A), F32)
    n_alias = 0 if wkv_out_prev is None else 1
    kern = functools.partial(_rwkv_kernel, nb=nb, tb=tb, n_alias=n_alias)
    state_spec = pl.BlockSpec((None, nb, H_A, N_A, N_A), lambda i, j: (l, i, 0, 0, 0))
    in_specs = [
        pl.BlockSpec((nb, tb, RWKV_COLS), lambda i, j: (i, j, 0)),
        pl.BlockSpec((nb, 1, RWKV_COLS), lambda i, j: (i, 0, 0)),
        state_spec,
        full((1, RWKV_COLS)), full((1, D_A)), full((128, 2 * D_A)), full((1, D_A)), full((R_G, D_A)),
        full((1, D_A)), full((1, D_A)), full((1, D_A)), full((1, D_A)), full((1, D_A)), full((D_A // 2, D_A // 2)),
    ]
    args = [p3, shift_prev, wkv_all, *wts]
    aliases = {}
    out_state_spec = pl.BlockSpec((DEPTH, nb, H_A, N_A, N_A), lambda i, j: (0, i, 0, 0, 0))
    if n_alias:
        in_specs.append(pl.BlockSpec(memory_space=pl.ANY))
        args.append(wkv_out_prev)
        aliases = {len(args) - 1: 2}
        out_state_spec = state_spec
    return pl.pallas_call(
        kern,
        grid=(bsz // nb, t // tb),
        in_specs=in_specs,
        out_specs=[
            pl.BlockSpec((nb, tb, D_A), lambda i, j: (i, j, 0)),
            pl.BlockSpec((nb, 1, RWKV_COLS), lambda i, j: (i, 0, 0)),
            out_state_spec,
        ],
        out_shape=[
            jax.ShapeDtypeStruct((bsz, t, D_A), y_dtype),
            jax.ShapeDtypeStruct((bsz, 1, RWKV_COLS), F32),
            jax.ShapeDtypeStruct((DEPTH, bsz, H_A, N_A, N_A), F32),
        ],
        scratch_shapes=[
            pltpu.VMEM((nb, n_hp, N_A, 128), F32),
            pltpu.VMEM((nb, 1, RWKV_COLS), F32),
            *[pltpu.VMEM((nb, n_hp, tb, 2 * N_A), F32) for _ in range(11)],
            *[blk() for _ in range(4)],
            *[pltpu.VMEM((nb, n_hp, tb + 24, 2 * N_A), F32) for _ in range(3)],
        ],
        input_output_aliases=aliases,
        compiler_params=pltpu.CompilerParams(
            dimension_semantics=("parallel", "arbitrary"), vmem_limit_bytes=VMEM_LIMIT),
        name="rwkv7",
    )(*args)


def _mlstm_kernel(*refs, nb, tb, lc, n_alias):
    (pm_ref, pg_ref, cp_ref, c0_ref, n0_ref, m0_ref, cw_ref, cb_ref, gb_ref, mhg_ref) = refs[:10]
    (yb_ref, co_ref, ct_ref, nt_ref, mt_ref, x_scr, c_scr, n_scr, m_scr) = refs[10 + n_alias:]
    j = pl.program_id(1)
    hi = lax.Precision.HIGHEST
    nch = tb // lc
    lc_shift = lc.bit_length() - 1

    @pl.when(j == 0)
    def _():
        c_scr[...] = c0_ref[...]
        n_scr[...] = n0_ref[...]
        m_scr[...] = jnp.broadcast_to(m0_ref[...], m_scr.shape)
        x_scr[:, 5:8, :] = cp_ref[...]

    rr = lax.broadcasted_iota(jnp.int32, (lc, lc), 0)
    cc = lax.broadcasted_iota(jnp.int32, (lc, lc), 1)
    causal = rr >= cc
    rb = lax.broadcasted_iota(jnp.int32, (tb, tb), 0)
    cb = lax.broadcasted_iota(jnp.int32, (tb, tb), 1)
    tri_blk = ((rb >= cb) & (lax.shift_right_logical(rb, lc_shift) == lax.shift_right_logical(cb, lc_shift))
               ).astype(F32)
    lane_g = lax.broadcasted_iota(jnp.int32, (tb, 128), 1)
    tn_dims = (((0,), (0,)), ((), ()))
    nt_dims = (((1,), (1,)), ((), ()))

    units = [(b, c, h) for b in range(nb) for c in range(nch) for h in range(H_B)]
    q_u, k_u, v_u, o_u = {}, {}, {}, {}
    li_col, li_row, b_col, b_row, b_l = {}, {}, {}, {}, {}

    for b in range(nb):
        x = pm_ref[b, :, 0:2 * D_B]
        x_scr[b, pl.ds(8, tb), :] = x
        conv = cb_ref[...] + x * cw_ref[3:4, :]
        for s in range(1, CONV_W):
            conv = conv + x_scr[b, pl.ds(8 - s, tb), :] * cw_ref[3 - s:4 - s, :]
        x_scr[b, 5:8, :] = x_scr[b, pl.ds(8 + tb - 3, 3), :]
        sc = conv * _sigmoid(conv)
        q_all = sc[:, :D_B]
        k_all = sc[:, D_B:] * (DK_B ** -0.5)
        v_all = pm_ref[b, :, 2 * D_B:3 * D_B]
        o_all = pm_ref[b, :, 3 * D_B:4 * D_B]
        gp = pg_ref[b, :, 0:128] + gb_ref[...]
        gates = jnp.where(lane_g < H_B, gp, jnp.minimum(gp, 0.0) - jnp.log1p(jnp.exp(-jnp.abs(gp))))
        csum_col = jnp.dot(tri_blk, gates, precision=hi, preferred_element_type=F32)
        gt = gates.T
        csum_row = lax.dot_general(gt, tri_blk, nt_dims, precision=hi, preferred_element_type=F32)
        for c in range(nch):
            rs = slice(c * lc, (c + 1) * lc)
            for h in range(H_B):
                u = (b, c, h)
                hs = slice(h * DK_B, (h + 1) * DK_B)
                q_u[u], k_u[u], v_u[u], o_u[u] = q_all[rs, hs], k_all[rs, hs], v_all[rs, hs], o_all[rs, hs]
                li_col[u] = gates[rs, h:h + 1]
                li_row[u] = gt[h:h + 1, rs]
                b_col[u] = csum_col[rs, H_B + h:H_B + h + 1]
                b_row[u] = csum_row[H_B + h:H_B + h + 1, rs]
                b_l[u] = csum_col[(c + 1) * lc - 1:(c + 1) * lc, H_B + h:H_B + h + 1]

    rep = lambda col: jnp.broadcast_to(col, (lc, DK_B))
    bc = {u: rep(b_col[u]) for u in units}
    lic = {u: rep(li_col[u]) for u in units}

    last_max = {u: jnp.max(b_l[u] - b_row[u] + li_row[u], axis=-1, keepdims=True) for u in units}
    m_prev, m_new = {}, {}
    for b in range(nb):
        for h in range(H_B):
            m_p = m_scr[b, h][:, 0:1]
            for c in range(nch):
                u = (b, c, h)
                m_prev[u] = m_p
                m_p = jnp.maximum(b_l[u] + m_p, last_max[u])
                m_new[u] = m_p
            m_scr[b, h] = jnp.broadcast_to(m_p, (1, 128))

    dmat = {u: jnp.where(causal, bc[u][:, :lc] - b_row[u] + li_row[u], -jnp.inf) for u in units}
    row_max = {u: rep(jnp.max(dmat[u], axis=-1, keepdims=True)) for u in units}
    m_t, g_inter, e_mat = {}, {}, {}
    for u in units:
        inter = bc[u] + m_prev[u]
        m_t[u] = jnp.maximum(inter, row_max[u])
        g_inter[u] = jnp.exp(inter - m_t[u])
        e_mat[u] = jnp.exp(dmat[u] - m_t[u][:, :lc])

    qk = {u: lax.dot_general(q_u[u].astype(BF16), k_u[u].astype(BF16), nt_dims, preferred_element_type=F32)
          for u in units}
    s_mat = {u: qk[u] * e_mat[u] for u in units}
    sv = {u: _bdot(s_mat[u], v_u[u]) for u in units}
    s_sum = {u: rep(jnp.sum(s_mat[u], axis=-1, keepdims=True)) for u in units}
    kw = {u: k_u[u] * jnp.exp(b_l[u] - bc[u] + lic[u] - m_new[u]) for u in units}
    kwv = {u: lax.dot_general(kw[u].astype(BF16), v_u[u].astype(BF16), tn_dims, preferred_element_type=F32)
           for u in units}

    q_c, qn_prod = {}, {}
    c_cur = {(b, h): c_scr[b, h] for b in range(nb) for h in range(H_B)}
    n_cur = {(b, h): n_scr[b, h] for b in range(nb) for h in range(H_B)}
    for c in range(nch):
        for b in range(nb):
            for h in range(H_B):
                u = (b, c, h)
                q_c[u] = _bdot(q_u[u], c_cur[(b, h)])
                qn_prod[u] = q_u[u] * n_cur[(b, h)]
                dec = jnp.exp(b_l[u] + m_prev[u] - m_new[u])
                c_cur[(b, h)] = dec * c_cur[(b, h)] + kwv[u]
                n_cur[(b, h)] = dec * n_cur[(b, h)] + jnp.sum(kw[u], axis=0, keepdims=True)
    for b in range(nb):
        for h in range(H_B):
            c_scr[b, h] = c_cur[(b, h)]
            n_scr[b, h] = n_cur[(b, h)]
    q_n = {u: rep(jnp.sum(qn_prod[u], axis=-1, keepdims=True)) for u in units}

    hh = {}
    for u in units:
        num = g_inter[u] * q_c[u] + sv[u]
        den = g_inter[u] * q_n[u] + s_sum[u]
        hh[u] = num / jnp.maximum(jnp.abs(den), jnp.exp(-m_t[u]))
    mu = {u: rep(jnp.sum(hh[u], axis=-1, keepdims=True)) * (1.0 / DK_B) for u in units}
    dev = {u: hh[u] - mu[u] for u in units}
    var = {u: rep(jnp.sum(dev[u] * dev[u], axis=-1, keepdims=True)) * (1.0 / DK_B) for u in units}
    for u in units:
        b, c, h = u
        hs = slice(h * DK_B, (h + 1) * DK_B)
        hn = dev[u] * lax.rsqrt(var[u] + MH_EPS) * mhg_ref[:, hs]
        yb_ref[b, c * lc:(c + 1) * lc, hs] = (_sigmoid(o_u[u]) * hn).astype(yb_ref.dtype)

    @pl.when(j == pl.num_programs(1) - 1)
    def _():
        if n_alias:
            ct_ref[...] = c_scr[...]
        else:
            ct_ref[0] = c_scr[...]
            ct_ref[1:] = jnp.zeros((DEPTH - 1,) + tuple(ct_ref.shape[1:]), F32)
        nt_ref[...] = n_scr[...]
        mt_ref[...] = m_scr[:, :, :, 0:1]
        co_ref[...] = x_scr[:, 5:8, :]


def _mlstm(p3, conv_prev, c_all, l, c_out_prev, n0, m0, wts, nb, tb, lc, y_dtype):
    bsz, t, _ = p3.shape
    full = lambda shape: pl.BlockSpec(shape, lambda i, j: (0,) * len(shape))
    n_alias = 0 if c_out_prev is None else 1
    kern = functools.partial(_mlstm_kernel, nb=nb, tb=tb, lc=lc, n_alias=n_alias)
    gate_blk = RWKV_COLS // GATE_COLS
    c_spec = pl.BlockSpec((None, nb, H_B, DK_B, DK_B), lambda i, j: (l, i, 0, 0, 0))
    in_specs = [
        pl.BlockSpec((nb, tb, ML_COLS), lambda i, j: (i, j, 1)),
        pl.BlockSpec((nb, tb, GATE_COLS), lambda i, j: (i, j, gate_blk)),
        pl.BlockSpec((nb, CONV_W - 1, 2 * D_B), lambda i, j: (i, 0, 0)),
        c_spec,
        pl.BlockSpec((nb, H_B, 1, DK_B), lambda i, j: (i, 0, 0, 0)),
        pl.BlockSpec((nb, H_B, 1, 1), lambda i, j: (i, 0, 0, 0)),
        full((CONV_W, 2 * D_B)), full((1, 2 * D_B)), full((1, 128)), full((1, D_B)),
    ]
    args = [p3, p3, conv_prev, c_all, n0, m0, *wts]
    aliases = {}
    out_c_spec = pl.BlockSpec((DEPTH, nb, H_B, DK_B, DK_B), lambda i, j: (0, i, 0, 0, 0))
    if n_alias:
        in_specs.append(pl.BlockSpec(memory_space=pl.ANY))
        args.append(c_out_prev)
        aliases = {len(args) - 1: 2}
        out_c_spec = c_spec
    return pl.pallas_call(
        kern,
        grid=(bsz // nb, t // tb),
        in_specs=in_specs,
        out_specs=[
            pl.BlockSpec((nb, tb, D_B), lambda i, j: (i, j, 0)),
            pl.BlockSpec((nb, CONV_W - 1, 2 * D_B), lambda i, j: (i, 0, 0)),
            out_c_spec,
            pl.BlockSpec((nb, H_B, 1, DK_B), lambda i, j: (i, 0, 0, 0)),
            pl.BlockSpec((nb, H_B, 1, 1), lambda i, j: (i, 0, 0, 0)),
        ],
        out_shape=[
            jax.ShapeDtypeStruct((bsz, t, D_B), y_dtype),
            jax.ShapeDtypeStruct((bsz, CONV_W - 1, 2 * D_B), F32),
            jax.ShapeDtypeStruct((DEPTH, bsz, H_B, DK_B, DK_B), F32),
            jax.ShapeDtypeStruct((bsz, H_B, 1, DK_B), F32),
            jax.ShapeDtypeStruct((bsz, H_B, 1, 1), F32),
        ],
        scratch_shapes=[
            pltpu.VMEM((nb, tb + 8, 2 * D_B), F32),
            pltpu.VMEM((nb, H_B, DK_B, DK_B), F32),
            pltpu.VMEM((nb, H_B, 1, DK_B), F32),
            pltpu.VMEM((nb, H_B, 1, 128), F32),
        ],
        input_output_aliases=aliases,
        compiler_params=pltpu.CompilerParams(
            dimension_semantics=("parallel", "arbitrary"), vmem_limit_bytes=VMEM_LIMIT),
        name="mlstm",
    )(*args)


def _layer_weights(l, w):
    bf = lambda a: a.astype(BF16)
    row = lambda a: a.reshape(1, -1)
    w_in = w['w_in'][l]
    w_gates = jnp.pad(w_in[:, RWKV_COLS + ML_COLS:], ((0, 0), (0, GATE_COLS - 2 * H_B)))
    w_cat = jnp.concatenate([w_in[:, :RWKV_COLS], w_gates, w_in[:, RWKV_COLS:RWKV_COLS + ML_COLS]], axis=1)
    zero = jnp.zeros((R_W, D_A), F32)
    w2a = jnp.concatenate([jnp.concatenate([w['w2'][l], zero], axis=1),
                           jnp.concatenate([zero, w['a2'][l]], axis=1)], axis=0)
    ids = jnp.arange(D_A // 2) // N_A
    seg = (ids[:, None] == ids[None, :]).astype(BF16)
    gate_bias = jnp.pad(jnp.concatenate([w['i_bias'][l], w['f_bias'][l]]), (0, 128 - 2 * H_B)).reshape(1, 128)
    return dict(
        ln1=(row(w['ln1_g'][l]), row(w['ln1_b'][l])),
        ln3=(row(w['ln3_g'][l]), row(w['ln3_b'][l])),
        w_cat=bf(w_cat),
        rwkv=(row(w['mu_shift'][l]), row(w['w0'][l]), bf(w2a), row(w['a0'][l]), bf(w['g2'][l]), row(w['k_k'][l]),
              row(w['k_a'][l]), row(w['r_k'][l]), row(w['gn_g'][l]), row(w['gn_b'][l]), seg),
        mlstm=(w['conv_w'][l], row(w['conv_b'][l]), gate_bias, row(w['mh_g'][l])),
        out=(bf(w['w_out'][l][:D_A]), bf(w['w_out'][l][D_A:]), row(w['ln2_g'][l]), row(w['ln2_b'][l])),
    )


def _trunk(x, states, lw, ffn_w, nb_r, tb_r, nb_m, tb_m):
    bsz, t, _ = x.shape
    n = bsz * t
    tm_ffn, tm_out, tm_proj = min(FFN_TILE_ROWS, n), min(OUT_PROJ_TILE_ROWS, n), min(PROJ_TILE_ROWS, n)
    y_dtype = BF16 if t % 16 == 0 else F32
    lc = math.gcd(t, CHUNK)
    xf = x.reshape(n, D_MODEL)
    st_shift, st_wkv, st_conv, st_c, st_n, st_m = states
    new = [[] for _ in range(4)]
    wkv_out, c_out = None, None
    for l in range(DEPTH):
        wl = lw[l]
        x1 = _ffn_ln(xf, *ffn_w[0], *wl['ln1'], l=l, tm=tm_ffn)
        p3 = _in_proj(x1, wl['w_cat'], tm=tm_proj).reshape(bsz, t, P_COLS)
        ya, shift, wkv_out = _rwkv(p3, st_shift[l].reshape(bsz, 1, RWKV_COLS), st_wkv, l, wkv_out,
                                   wl['rwkv'], nb_r, tb_r, y_dtype)
        yb, conv, c_out, n_t, m_t = _mlstm(p3, st_conv[l], st_c, l, c_out,
                                           st_n[l].reshape(bsz, H_B, 1, DK_B), st_m[l].reshape(bsz, H_B, 1, 1),
                                           wl['mlstm'], nb_m, tb_m, lc, y_dtype)
        x2 = _out_proj_ln(x1, ya.reshape(n, D_A), yb.reshape(n, D_B), *wl['out'], tm=tm_out)
        xf = _ffn_ln(x2, *ffn_w[1], *wl['ln3'], l=l, tm=tm_ffn)
        for idx, s in enumerate((shift.reshape(bsz, RWKV_COLS), conv, n_t.reshape(bsz, H_B, DK_B),
                                 m_t.reshape(bsz, H_B))):
            new[idx].append(s)
    shift_o, conv_o, n_o, m_o = [jnp.stack(s) for s in new]
    return xf.reshape(bsz, t, D_MODEL), [shift_o, wkv_out, conv_o, c_out, n_o, m_o]


def kernel(x_prompt, x_sample, state_shift, state_wkv, state_conv, state_C, state_n, state_m,
           ffn1_wg, ffn1_wu, ffn1_wd, ln1_g, ln1_b, w_in, mu_shift, w0, w2, a0, a2, g2, k_k, k_a, r_k,
           gn_g, gn_b, conv_w, conv_b, i_bias, f_bias, mh_g, w_out, ln2_g, ln2_b,
           ffn2_wg, ffn2_wu, ffn2_wd, ln3_g, ln3_b):
    w = dict(ffn1_wg=ffn1_wg, ffn1_wu=ffn1_wu, ffn1_wd=ffn1_wd, ln1_g=ln1_g, ln1_b=ln1_b, w_in=w_in,
             mu_shift=mu_shift, w0=w0, w2=w2, a0=a0, a2=a2, g2=g2, k_k=k_k, k_a=k_a, r_k=r_k,
             gn_g=gn_g, gn_b=gn_b, conv_w=conv_w, conv_b=conv_b, i_bias=i_bias, f_bias=f_bias,
             mh_g=mh_g, w_out=w_out, ln2_g=ln2_g, ln2_b=ln2_b, ffn2_wg=ffn2_wg, ffn2_wu=ffn2_wu,
             ffn2_wd=ffn2_wd, ln3_g=ln3_g, ln3_b=ln3_b)
    lw = [_layer_weights(l, w) for l in range(DEPTH)]
    ffn_w = ((ffn1_wg, ffn1_wu, ffn1_wd), (ffn2_wg, ffn2_wu, ffn2_wd))
    bp = x_prompt.shape[0]
    dt = x_prompt.dtype
    init = [jnp.zeros((DEPTH, bp, RWKV_COLS), dt),
            jnp.zeros((DEPTH, bp, H_A, N_A, N_A), dt),
            jnp.zeros((DEPTH, bp, CONV_W - 1, 2 * D_B), dt),
            jnp.zeros((DEPTH, bp, H_B, DK_B, DK_B), dt),
            jnp.zeros((DEPTH, bp, H_B, DK_B), dt),
            jnp.zeros((DEPTH, bp, H_B), dt)]
    y_p, ps = _trunk(x_prompt, init, lw, ffn_w, nb_r=8, tb_r=64, nb_m=1, tb_m=256)
    y_s, ss = _trunk(x_sample, [state_shift, state_wkv, state_conv, state_C, state_n, state_m], lw, ffn_w,
                     nb_r=16, tb_r=8, nb_m=16, tb_m=8)
    return (y_p, y_s, *ps, *ss)
```

```python
import functools
import math

import jax
import jax.numpy as jnp
from jax import lax
from jax.experimental import pallas as pl
from jax.experimental.pallas import tpu as pltpu

D_MODEL = 1024
DEPTH = 2
D_A = 512
N_A = 64
H_A = 8
D_B = 512
H_B = 4
DK_B = 128
R_W = 64
R_A = 64
R_G = 128
RWKV_COLS = 3 * D_A + R_W + R_A + R_G
CONV_W = 4
CHUNK = 64
D_FF = 2816
ALPHA = (2.0 * DEPTH) ** 0.25
LN_EPS = 1e-5
GN_EPS = 64e-5
MH_EPS = 1e-6

GATE_COLS = 256
ML_COLS = 4 * D_B
P_COLS = RWKV_COLS + GATE_COLS + ML_COLS
VMEM_LIMIT = 56 * 1024 * 1024
FFN_COL_BLOCK = 256
FFN_ROW_CHUNK = 512
FFN_LAST_CHUNK = 256
FFN_TILE_ROWS = 2048
OUT_PROJ_TILE_ROWS = 2048
PROJ_TILE_ROWS = 512
RWKV_ROWS_PER_TRIP = 32
RWKV_BATCH_ROWS = 8
RWKV_TIME_BLOCK = 64
MLSTM_TIME_BLOCK = 256
MIXER_SHORT_BATCH_ROWS = 16

F32 = jnp.float32
BF16 = jnp.bfloat16


def _bdot(a, b):
    return jnp.dot(a.astype(BF16), b.astype(BF16), preferred_element_type=F32)


def _seg_dot(x_bf16, seg_half):
    h = seg_half.shape[0]
    return jnp.concatenate([jnp.dot(x_bf16[:, :h], seg_half, preferred_element_type=F32),
                            jnp.dot(x_bf16[:, h:], seg_half, preferred_element_type=F32)], axis=1)


def _seg_sum(x, seg_half):
    hi = x.astype(BF16)
    lo = (x - hi.astype(F32)).astype(BF16)
    return _seg_dot(hi, seg_half) + _seg_dot(lo, seg_half)


def _layer_norm_rows(y, g, b):
    mu = jnp.mean(y, axis=-1, keepdims=True)
    d = y - mu
    var = jnp.mean(d * d, axis=-1, keepdims=True)
    return d * lax.rsqrt(var + LN_EPS) * g + b


def _sigmoid(x):
    return 1.0 / (1.0 + jnp.exp(-x))


def _ffn_ln_kernel(x_ref, wg_ref, wu_ref, wd_ref, g_ref, b_ref, o_ref, xb_scr):
    j = pl.program_id(1)
    last = pl.num_programs(1) - 1
    tm = x_ref.shape[0]

    def row_chunks(size):
        return [pl.ds(start, size) for start in range(0, tm, size)]

    def partial_ffn(xb, wg, wu, wd):
        hg = jnp.dot(xb, wg, preferred_element_type=F32)
        hu = jnp.dot(xb, wu, preferred_element_type=F32)
        h = (hg * _sigmoid(hg)) * hu
        return jnp.dot(h.astype(BF16), wd, preferred_element_type=F32)

    def weights():
        return wg_ref[...].astype(BF16), wu_ref[...].astype(BF16), wd_ref[...].astype(BF16)

    @pl.when(j == 0)
    def _():
        wg, wu, wd = weights()
        for rows in row_chunks(FFN_ROW_CHUNK):
            xb = x_ref[rows, :].astype(BF16)
            xb_scr[rows, :] = xb
            o_ref[rows, :] = partial_ffn(xb, wg, wu, wd)

    @pl.when((j > 0) & (j < last))
    def _():
        wg, wu, wd = weights()
        for rows in row_chunks(FFN_ROW_CHUNK):
            o_ref[rows, :] += partial_ffn(xb_scr[rows, :], wg, wu, wd)

    @pl.when(j == last)
    def _():
        wg, wu, wd = weights()
        for rows in row_chunks(FFN_LAST_CHUNK):
            acc = o_ref[rows, :] + partial_ffn(xb_scr[rows, :], wg, wu, wd)
            o_ref[rows, :] = _layer_norm_rows(ALPHA * x_ref[rows, :] + 0.5 * acc, g_ref[...], b_ref[...])


def _ffn_ln(x, wg, wu, wd, g, b, l, tm):
    n = x.shape[0]
    const = lambda shape: pl.BlockSpec(shape, lambda i, j: (0, 0))
    return pl.pallas_call(
        _ffn_ln_kernel,
        grid=(n // tm, D_FF // FFN_COL_BLOCK),
        in_specs=[
            pl.BlockSpec((tm, D_MODEL), lambda i, j: (i, 0)),
            pl.BlockSpec((None, D_MODEL, FFN_COL_BLOCK), lambda i, j: (l, 0, j)),
            pl.BlockSpec((None, D_MODEL, FFN_COL_BLOCK), lambda i, j: (l, 0, j)),
            pl.BlockSpec((None, FFN_COL_BLOCK, D_MODEL), lambda i, j: (l, j, 0)),
            const((1, D_MODEL)), const((1, D_MODEL)),
        ],
        out_specs=pl.BlockSpec((tm, D_MODEL), lambda i, j: (i, 0)),
        out_shape=jax.ShapeDtypeStruct((n, D_MODEL), F32),
        scratch_shapes=[pltpu.VMEM((tm, D_MODEL), BF16)],
        compiler_params=pltpu.CompilerParams(
            dimension_semantics=("parallel", "arbitrary"), vmem_limit_bytes=VMEM_LIMIT),
        name="ffn_ln",
    )(x, wg, wu, wd, g, b)


def _in_proj_kernel(x_ref, w_ref, o_ref):
    o_ref[...] = jnp.dot(x_ref[...].astype(BF16), w_ref[...], preferred_element_type=F32)


def _in_proj(x, w, tm):
    n = x.shape[0]
    return pl.pallas_call(
        _in_proj_kernel,
        grid=(n // tm,),
        in_specs=[
            pl.BlockSpec((tm, D_MODEL), lambda i: (i, 0)),
            pl.BlockSpec((D_MODEL, P_COLS), lambda i: (0, 0)),
        ],
        out_specs=pl.BlockSpec((tm, P_COLS), lambda i: (i, 0)),
        out_shape=jax.ShapeDtypeStruct((n, P_COLS), F32),
        compiler_params=pltpu.CompilerParams(
            dimension_semantics=("parallel",), vmem_limit_bytes=VMEM_LIMIT),
        name="in_proj",
    )(x, w)


def _out_proj_ln_kernel(x_ref, ya_ref, yb_ref, wa_ref, wb_ref, g_ref, b_ref, o_ref):
    mix = (jnp.dot(ya_ref[...].astype(BF16), wa_ref[...], preferred_element_type=F32)
           + jnp.dot(yb_ref[...].astype(BF16), wb_ref[...], preferred_element_type=F32))
    o_ref[...] = _layer_norm_rows(ALPHA * x_ref[...] + mix, g_ref[...], b_ref[...])


def _out_proj_ln(x, ya, yb, wa, wb, g, b, tm):
    n = x.shape[0]
    return pl.pallas_call(
        _out_proj_ln_kernel,
        grid=(n // tm,),
        in_specs=[
            pl.BlockSpec((tm, D_MODEL), lambda i: (i, 0)),
            pl.BlockSpec((tm, D_A), lambda i: (i, 0)),
            pl.BlockSpec((tm, D_B), lambda i: (i, 0)),
            pl.BlockSpec((D_A, D_MODEL), lambda i: (0, 0)),
            pl.BlockSpec((D_B, D_MODEL), lambda i: (0, 0)),
            pl.BlockSpec((1, D_MODEL), lambda i: (0, 0)),
            pl.BlockSpec((1, D_MODEL), lambda i: (0, 0)),
        ],
        out_specs=pl.BlockSpec((tm, D_MODEL), lambda i: (i, 0)),
        out_shape=jax.ShapeDtypeStruct((n, D_MODEL), F32),
        compiler_params=pltpu.CompilerParams(
            dimension_semantics=("parallel",), vmem_limit_bytes=VMEM_LIMIT),
        name="out_proj_ln",
    )(x, ya, yb, wa, wb, g, b)


def _rwkv_kernel(*refs, nb, tb, n_alias):
    (p_ref, sp_ref, s0_ref, mu_ref, w0_ref, w2a_ref, a0_ref, g2_ref, kk_ref, ka_ref, rk_ref,
     gng_ref, gnb_ref, seg_ref) = refs[:14]
    (ya_ref, so_ref, st_ref,
     s_scr, carry_scr, nk_scr, p2_scr, q1_scr, rp_scr, v_scr, vsw_scr, ya1_scr, b_scr, yc1_scr, k_scr, w12_scr,
     cbr_scr, ckr_scr, bonus_scr, g_scr, sa_scr, q_scr, yo_scr) = refs[14 + n_alias:]
    j = pl.program_id(1)
    n_hp = H_A // 2

    @pl.when(j == 0)
    def _():
        for b in range(nb):
            for hp in range(n_hp):
                s_scr[b, hp] = jnp.concatenate([s0_ref[b, 2 * hp], s0_ref[b, 2 * hp + 1]], axis=-1)
        carry_scr[...] = sp_ref[...]
        sa_scr[...] = jnp.zeros_like(sa_scr)
        q_scr[...] = jnp.zeros_like(q_scr)
        yo_scr[...] = jnp.zeros_like(yo_scr)

    seg = seg_ref[...]
    rows = nb * tb
    lane128 = lax.broadcasted_iota(jnp.int32, (rows, 128), 1)
    rowid = lax.broadcasted_iota(jnp.int32, (tb, RWKV_COLS), 0)
    rowid_a = lax.broadcasted_iota(jnp.int32, (tb, D_A), 0)

    pms, rps, r_last = [], [], []
    for b in range(nb):
        p = p_ref[b]
        prev = jnp.where(rowid == 0, carry_scr[b], pltpu.roll(p, 1, axis=0))
        carry_scr[b] = p[tb - 1:tb, :]
        pm_b = p + (prev - p) * mu_ref[...]
        r_b = pm_b[:, 0:D_A]
        pms.append(pm_b)
        rps.append(jnp.where(rowid_a == 0, 0.0, pltpu.roll(r_b, 1, axis=0)))
        r_last.append(r_b[tb - 1:tb, :])
    pm = jnp.concatenate(pms, axis=0)
    r = pm[:, 0:D_A]
    k = pm[:, D_A:2 * D_A]
    v = pm[:, 2 * D_A:3 * D_A]
    z = pm[:, 3 * D_A:3 * D_A + 128]
    xg = pm[:, 3 * D_A + 128:RWKV_COLS]
    zt = jnp.where(lane128 < R_W, jnp.tanh(z), z)
    lr = _bdot(zt, w2a_ref[...])
    g_scr[...] = _bdot(_sigmoid(xg), g2_ref[...]).reshape(nb, tb, D_A)
    kk = k * kk_ref[...]
    ss = _seg_sum(kk * kk, seg)
    w = jnp.exp(-math.exp(-0.5) * _sigmoid(w0_ref[...] + lr[:, :D_A]))
    a = _sigmoid(a0_ref[...] + lr[:, D_A:])
    kk = kk * lax.rsqrt(jnp.maximum(ss, 1e-24))
    kmod = k * (1.0 + (a - 1.0) * ka_ref[...])
    bonus_scr[...] = (_seg_sum(r * kmod * rk_ref[...], seg) * v).reshape(nb, tb, D_A)
    bb = kk * a
    nxt = lambda x: pltpu.roll(x, rows - 1, axis=0)
    kk_n, w_n, bb_n = nxt(kk), nxt(w), nxt(bb)
    cc = _seg_dot(jnp.concatenate([bb * kk_n, kmod * kk_n, bb * r, kmod * r], axis=0).astype(BF16), seg)
    cbk, ckk = cc[0:rows], cc[rows:2 * rows]
    cbr_scr[...] = cc[2 * rows:3 * rows].reshape(nb, tb, D_A)
    ckr_scr[...] = cc[3 * rows:4 * rows].reshape(nb, tb, D_A)
    def put(scr, arr):
        for hp in range(n_hp):
            scr[:, hp] = arr[:, hp * 128:(hp + 1) * 128].reshape(nb, tb, 128)

    put(nk_scr, -kk)
    put(p2_scr, -(w * kk_n))
    put(q1_scr, w * r)
    put(rp_scr, jnp.concatenate(rps, axis=0))
    put(ya1_scr, bb * w_n - cbk * bb_n)
    put(b_scr, bb)
    put(yc1_scr, kmod * w_n - ckk * bb_n)
    put(k_scr, kmod)
    put(w12_scr, w * w_n)
    put(v_scr, v)
    for hp in range(n_hp):
        vsw_scr[:, hp] = pltpu.roll(v[:, hp * 128:(hp + 1) * 128], 64, axis=1).reshape(nb, tb, 128)

    row8 = lax.broadcasted_iota(jnp.int32, (8, 128), 0)
    lane8 = lax.broadcasted_iota(jnp.int32, (8, 128), 1)
    pair8 = lax.shift_right_logical(row8, 1)
    half8 = ((((row8 & 1) == 0) & (lane8 < 64)) | (((row8 & 1) == 1) & (lane8 >= 64))).astype(F32)
    mk3 = ((row8 >= 6) & ((((row8 & 1) == 0) & (lane8 < 64)) | (((row8 & 1) == 1) & (lane8 >= 64)))).astype(F32)
    row64 = lax.broadcasted_iota(jnp.int32, (8, 64), 0)

    def pair_tile(q0, q1, q2, q3):
        return jnp.where(pair8 == 0, q0, jnp.where(pair8 == 1, q1, jnp.where(pair8 == 2, q2, q3))) * half8

    chains = [(b, hp) for b in range(nb) for hp in range(n_hp)]
    nt_dims = (((1,), (1,)), ((), ()))
    tn_dims = (((0,), (0,)), ((), ()))
    grp = min(RWKV_ROWS_PER_TRIP, tb)
    n_pair = grp // 2

    def natural_rows(rows_h0, rows_h1):
        pad = [jnp.zeros((8 - len(rows_h0), N_A), F32)] if len(rows_h0) < 8 else []
        return jnp.concatenate([jnp.concatenate(rows_h0 + pad, axis=0),
                                jnp.concatenate(rows_h1 + pad, axis=0)], axis=1)

    def group(tg, carry):
        t0 = tg * grp

        def row(ref, c, t, n=8):
            return ref[c[0], c[1], pl.ds(t0 + t, n, stride=0), :]

        out = {c: [[] for _ in range(6)] for c in chains}
        for pi in range(n_pair):
            i = 2 * pi
            reds = {}
            for c in chains:
                a_mat = pair_tile(row(nk_scr, c, i), row(p2_scr, c, i), row(q1_scr, c, i), row(rp_scr, c, i))
                reds[c] = lax.dot_general(a_mat.astype(BF16), s_scr[c[0], c[1]].astype(BF16), nt_dims,
                                          preferred_element_type=F32)
            xs = {}
            for c in chains:
                x_mat = jnp.where(
                    row64 < 4, reds[c],
                    jnp.where(row64 == 4, row(v_scr, c, i)[:, :64],
                              jnp.where(row64 == 5, row(vsw_scr, c, i)[:, :64],
                                        jnp.where(row64 == 6, row(v_scr, c, i + 1)[:, :64],
                                                  row(vsw_scr, c, i + 1)[:, :64]))))
                xs[c] = x_mat.astype(BF16)
            for c in chains:
                y_mat = pair_tile(row(ya1_scr, c, i), row(b_scr, c, i + 1), row(yc1_scr, c, i), row(k_scr, c, i + 1))
                d_s = lax.dot_general(xs[c], y_mat.astype(BF16), tn_dims, preferred_element_type=F32)
                s_scr[c[0], c[1]] = s_scr[c[0], c[1]] * row(w12_scr, c, i, N_A) + d_s
                for slot, red_row in enumerate((0, 1, 4, 5, 6, 7)):
                    out[c][slot].append(reds[c][red_row:red_row + 1, :])
        for k in range(0, n_pair, 8):
            even0 = pl.multiple_of(tg * grp + 2 * k, 8)
            for c in chains:
                o = [rows_list[k:k + 8] for rows_list in out[c]]
                sa_scr[c[0], c[1], pl.ds(even0, 8, stride=2), :] = natural_rows(o[0], o[1])
                q_scr[c[0], c[1], pl.ds(even0, 8, stride=2), :] = natural_rows(o[2], o[3])
                yo_scr[c[0], c[1], pl.ds(even0 + 8, 8, stride=2), :] = natural_rows(o[4], o[5])
        return carry

    lax.fori_loop(0, tb // grp, group, 0)

    for (b, hp) in chains:
        rl = r_last[b][:, hp * 128:(hp + 1) * 128]
        a_mat = rl * mk3
        red = lax.dot_general(a_mat.astype(BF16), s_scr[b, hp].astype(BF16), nt_dims, preferred_element_type=F32)
        y_last = jnp.concatenate([red[6:7, :], red[7:8, :]], axis=1)
        yo_scr[b, hp, pl.ds(tb + 8, 8), :] = jnp.broadcast_to(y_last, (8, 128))

    def slab(scr, first):
        return jnp.concatenate(
            [jnp.concatenate([scr[b, hp, pl.ds(first, tb), :] for hp in range(n_hp)], axis=1) for b in range(nb)],
            axis=0)

    y_even = (slab(q_scr, 0) + slab(sa_scr, 0) * cbr_scr[...].reshape(rows, D_A)
              + slab(v_scr, 0) * ckr_scr[...].reshape(rows, D_A))
    parity = lax.broadcasted_iota(jnp.int32, (rows, D_A), 0) & 1
    y = jnp.where(parity == 0, y_even, slab(yo_scr, 9))
    mu = _seg_sum(y, seg) * (1.0 / N_A)
    d = y - mu
    var = _seg_sum(d * d, seg) * (1.0 / N_A)
    yn = d * lax.rsqrt(var + GN_EPS) * gng_ref[...] + gnb_ref[...]
    ya_ref[...] = ((yn.reshape(nb, tb, D_A) + bonus_scr[...]) * g_scr[...]).astype(ya_ref.dtype)

    @pl.when(j == pl.num_programs(1) - 1)
    def _():
        own = st_ref if n_alias else st_ref.at[0]
        for b in range(nb):
            for hp in range(n_hp):
                s_pair = s_scr[b, hp]
                own[b, 2 * hp] = s_pair[:, :N_A]
                own[b, 2 * hp + 1] = s_pair[:, N_A:]
        if not n_alias:
            st_ref[1:] = jnp.zeros((DEPTH - 1,) + tuple(st_ref.shape[1:]), F32)
        so_ref[...] = carry_scr[...]


def _rwkv(p3, shift_prev, wkv_all, l, wkv_out_prev, wts, nb, tb, y_dtype):
    bsz, t, _ = p3.shape
    n_hp = H_A // 2
    full = lambda shape: pl.BlockSpec(shape, lambda i, j: (0,) * len(shape))
    blk = lambda: pltpu.VMEM((nb, tb, D_A), F32)
    n_alias = 0 if wkv_out_prev is None else 1
    kern = functools.partial(_rwkv_kernel, nb=nb, tb=tb, n_alias=n_alias)
    state_spec = pl.BlockSpec((None, nb, H_A, N_A, N_A), lambda i, j: (l, i, 0, 0, 0))
    in_specs = [
        pl.BlockSpec((nb, tb, RWKV_COLS), lambda i, j: (i, j, 0)),
        pl.BlockSpec((nb, 1, RWKV_COLS), lambda i, j: (i, 0, 0)),
        state_spec,
        full((1, RWKV_COLS)), full((1, D_A)), full((128, 2 * D_A)), full((1, D_A)), full((R_G, D_A)),
        full((1, D_A)), full((1, D_A)), full((1, D_A)), full((1, D_A)), full((1, D_A)), full((D_A // 2, D_A // 2)),
    ]
    args = [p3, shift_prev, wkv_all, *wts]
    aliases = {}
    out_state_spec = pl.BlockSpec((DEPTH, nb, H_A, N_A, N_A), lambda i, j: (0, i, 0, 0, 0))
    if n_alias:
        in_specs.append(pl.BlockSpec(memory_space=pl.ANY))
        args.append(wkv_out_prev)
        aliases = {len(args) - 1: 2}
        out_state_spec = state_spec
    return pl.pallas_call(
        kern,
        grid=(bsz // nb, t // tb),
        in_specs=in_specs,
        out_specs=[
            pl.BlockSpec((nb, tb, D_A), lambda i, j: (i, j, 0)),
            pl.BlockSpec((nb, 1, RWKV_COLS), lambda i, j: (i, 0, 0)),
            out_state_spec,
        ],
        out_shape=[
            jax.ShapeDtypeStruct((bsz, t, D_A), y_dtype),
            jax.ShapeDtypeStruct((bsz, 1, RWKV_COLS), F32),
            jax.ShapeDtypeStruct((DEPTH, bsz, H_A, N_A, N_A), F32),
        ],
        scratch_shapes=[
            pltpu.VMEM((nb, n_hp, N_A, 128), F32),
            pltpu.VMEM((nb, 1, RWKV_COLS), F32),
            *[pltpu.VMEM((nb, n_hp, tb, 2 * N_A), F32) for _ in range(11)],
            *[blk() for _ in range(4)],
            *[pltpu.VMEM((nb, n_hp, tb + 24, 2 * N_A), F32) for _ in range(3)],
        ],
        input_output_aliases=aliases,
        compiler_params=pltpu.CompilerParams(
            dimension_semantics=("parallel", "arbitrary"), vmem_limit_bytes=VMEM_LIMIT),
        name="rwkv7",
    )(*args)


def _mlstm_kernel(*refs, nb, tb, lc, n_alias):
    (pm_ref, pg_ref, cp_ref, c0_ref, n0_ref, m0_ref, cw_ref, cb_ref, gb_ref, mhg_ref) = refs[:10]
    (yb_ref, co_ref, ct_ref, nt_ref, mt_ref, x_scr, c_scr, n_scr, m_scr) = refs[10 + n_alias:]
    j = pl.program_id(1)
    hi = lax.Precision.HIGHEST
    nch = tb // lc
    lc_shift = lc.bit_length() - 1

    @pl.when(j == 0)
    def _():
        c_scr[...] = c0_ref[...]
        n_scr[...] = n0_ref[...]
        m_scr[...] = jnp.broadcast_to(m0_ref[...], m_scr.shape)
        x_scr[:, 5:8, :] = cp_ref[...]

    rr = lax.broadcasted_iota(jnp.int32, (lc, lc), 0)
    cc = lax.broadcasted_iota(jnp.int32, (lc, lc), 1)
    causal = rr >= cc
    rb = lax.broadcasted_iota(jnp.int32, (tb, tb), 0)
    cb = lax.broadcasted_iota(jnp.int32, (tb, tb), 1)
    tri_blk = ((rb >= cb) & (lax.shift_right_logical(rb, lc_shift) == lax.shift_right_logical(cb, lc_shift))
               ).astype(F32)
    lane_g = lax.broadcasted_iota(jnp.int32, (tb, 128), 1)
    tn_dims = (((0,), (0,)), ((), ()))
    nt_dims = (((1,), (1,)), ((), ()))

    units = [(b, c, h) for b in range(nb) for c in range(nch) for h in range(H_B)]
    q_u, k_u, v_u, o_u = {}, {}, {}, {}
    li_col, li_row, b_col, b_row, b_l = {}, {}, {}, {}, {}

    for b in range(nb):
        x = pm_ref[b, :, 0:2 * D_B]
        x_scr[b, pl.ds(8, tb), :] = x
        conv = cb_ref[...] + x * cw_ref[3:4, :]
        for s in range(1, CONV_W):
            conv = conv + x_scr[b, pl.ds(8 - s, tb), :] * cw_ref[3 - s:4 - s, :]
        x_scr[b, 5:8, :] = x_scr[b, pl.ds(8 + tb - 3, 3), :]
        sc = conv * _sigmoid(conv)
        q_all = sc[:, :D_B]
        k_all = sc[:, D_B:] * (DK_B ** -0.5)
        v_all = pm_ref[b, :, 2 * D_B:3 * D_B]
        o_all = pm_ref[b, :, 3 * D_B:4 * D_B]
        gp = pg_ref[b, :, 0:128] + gb_ref[...]
        gates = jnp.where(lane_g < H_B, gp, jnp.minimum(gp, 0.0) - jnp.log1p(jnp.exp(-jnp.abs(gp))))
        csum_col = jnp.dot(tri_blk, gates, precision=hi, preferred_element_type=F32)
        gt = gates.T
        csum_row = lax.dot_general(gt, tri_blk, nt_dims, precision=hi, preferred_element_type=F32)
        for c in range(nch):
            rs = slice(c * lc, (c + 1) * lc)
            for h in range(H_B):
                u = (b, c, h)
                hs = slice(h * DK_B, (h + 1) * DK_B)
                q_u[u], k_u[u], v_u[u], o_u[u] = q_all[rs, hs], k_all[rs, hs], v_all[rs, hs], o_all[rs, hs]
                li_col[u] = gates[rs, h:h + 1]
                li_row[u] = gt[h:h + 1, rs]
                b_col[u] = csum_col[rs, H_B + h:H_B + h + 1]
                b_row[u] = csum_row[H_B + h:H_B + h + 1, rs]
                b_l[u] = csum_col[(c + 1) * lc - 1:(c + 1) * lc, H_B + h:H_B + h + 1]

    rep = lambda col: jnp.broadcast_to(col, (lc, DK_B))
    bc = {u: rep(b_col[u]) for u in units}
    lic = {u: rep(li_col[u]) for u in units}

    last_max = {u: jnp.max(b_l[u] - b_row[u] + li_row[u], axis=-1, keepdims=True) for u in units}
    m_prev, m_new = {}, {}
    for b in range(nb):
        for h in range(H_B):
            m_p = m_scr[b, h][:, 0:1]
            for c in range(nch):
                u = (b, c, h)
                m_prev[u] = m_p
                m_p = jnp.maximum(b_l[u] + m_p, last_max[u])
                m_new[u] = m_p
            m_scr[b, h] = jnp.broadcast_to(m_p, (1, 128))

    dmat = {u: jnp.where(causal, bc[u][:, :lc] - b_row[u] + li_row[u], -jnp.inf) for u in units}
    row_max = {u: rep(jnp.max(dmat[u], axis=-1, keepdims=True)) for u in units}
    m_t, g_inter, e_mat = {}, {}, {}
    for u in units:
        inter = bc[u] + m_prev[u]
        m_t[u] = jnp.maximum(inter, row_max[u])
        g_inter[u] = jnp.exp(inter - m_t[u])
        e_mat[u] = jnp.exp(dmat[u] - m_t[u][:, :lc])

    qk = {u: lax.dot_general(q_u[u].astype(BF16), k_u[u].astype(BF16), nt_dims, preferred_element_type=F32)
          for u in units}
    s_mat = {u: qk[u] * e_mat[u] for u in units}
    sv = {u: _bdot(s_mat[u], v_u[u]) for u in units}
    s_sum = {u: rep(jnp.sum(s_mat[u], axis=-1, keepdims=True)) for u in units}
    kw = {u: k_u[u] * jnp.exp(b_l[u] - bc[u] + lic[u] - m_new[u]) for u in units}
    kwv = {u: lax.dot_general(kw[u].astype(BF16), v_u[u].astype(BF16), tn_dims, preferred_element_type=F32)
           for u in units}

    q_c, qn_prod = {}, {}
    c_cur = {(b, h): c_scr[b, h] for b in range(nb) for h in range(H_B)}
    n_cur = {(b, h): n_scr[b, h] for b in range(nb) for h in range(H_B)}
    for c in range(nch):
        for b in range(nb):
            for h in range(H_B):
                u = (b, c, h)
                q_c[u] = _bdot(q_u[u], c_cur[(b, h)])
                qn_prod[u] = q_u[u] * n_cur[(b, h)]
                dec = jnp.exp(b_l[u] + m_prev[u] - m_new[u])
                c_cur[(b, h)] = dec * c_cur[(b, h)] + kwv[u]
                n_cur[(b, h)] = dec * n_cur[(b, h)] + jnp.sum(kw[u], axis=0, keepdims=True)
    for b in range(nb):
        for h in range(H_B):
            c_scr[b, h] = c_cur[(b, h)]
            n_scr[b, h] = n_cur[(b, h)]
    q_n = {u: rep(jnp.sum(qn_prod[u], axis=-1, keepdims=True)) for u in units}

    hh = {}
    for u in units:
        num = g_inter[u] * q_c[u] + sv[u]
        den = g_inter[u] * q_n[u] + s_sum[u]
        hh[u] = num / jnp.maximum(jnp.abs(den), jnp.exp(-m_t[u]))
    mu = {u: rep(jnp.sum(hh[u], axis=-1, keepdims=True)) * (1.0 / DK_B) for u in units}
    dev = {u: hh[u] - mu[u] for u in units}
    var = {u: rep(jnp.sum(dev[u] * dev[u], axis=-1, keepdims=True)) * (1.0 / DK_B) for u in units}
    for u in units:
        b, c, h = u
        hs = slice(h * DK_B, (h + 1) * DK_B)
        hn = dev[u] * lax.rsqrt(var[u] + MH_EPS) * mhg_ref[:, hs]
        yb_ref[b, c * lc:(c + 1) * lc, hs] = (_sigmoid(o_u[u]) * hn).astype(yb_ref.dtype)

    @pl.when(j == pl.num_programs(1) - 1)
    def _():
        if n_alias:
            ct_ref[...] = c_scr[...]
        else:
            ct_ref[0] = c_scr[...]
            ct_ref[1:] = jnp.zeros((DEPTH - 1,) + tuple(ct_ref.shape[1:]), F32)
        nt_ref[...] = n_scr[...]
        mt_ref[...] = m_scr[:, :, :, 0:1]
        co_ref[...] = x_scr[:, 5:8, :]


def _mlstm(p3, conv_prev, c_all, l, c_out_prev, n0, m0, wts, nb, tb, lc, y_dtype):
    bsz, t, _ = p3.shape
    full = lambda shape: pl.BlockSpec(shape, lambda i, j: (0,) * len(shape))
    n_alias = 0 if c_out_prev is None else 1
    kern = functools.partial(_mlstm_kernel, nb=nb, tb=tb, lc=lc, n_alias=n_alias)
    gate_blk = RWKV_COLS // GATE_COLS
    c_spec = pl.BlockSpec((None, nb, H_B, DK_B, DK_B), lambda i, j: (l, i, 0, 0, 0))
    in_specs = [
        pl.BlockSpec((nb, tb, ML_COLS), lambda i, j: (i, j, 1)),
        pl.BlockSpec((nb, tb, GATE_COLS), lambda i, j: (i, j, gate_blk)),
        pl.BlockSpec((nb, CONV_W - 1, 2 * D_B), lambda i, j: (i, 0, 0)),
        c_spec,
        pl.BlockSpec((nb, H_B, 1, DK_B), lambda i, j: (i, 0, 0, 0)),
        pl.BlockSpec((nb, H_B, 1, 1), lambda i, j: (i, 0, 0, 0)),
        full((CONV_W, 2 * D_B)), full((1, 2 * D_B)), full((1, 128)), full((1, D_B)),
    ]
    args = [p3, p3, conv_prev, c_all, n0, m0, *wts]
    aliases = {}
    out_c_spec = pl.BlockSpec((DEPTH, nb, H_B, DK_B, DK_B), lambda i, j: (0, i, 0, 0, 0))
    if n_alias:
        in_specs.append(pl.BlockSpec(memory_space=pl.ANY))
        args.append(c_out_prev)
        aliases = {len(args) - 1: 2}
        out_c_spec = c_spec
    return pl.pallas_call(
        kern,
        grid=(bsz // nb, t // tb),
        in_specs=in_specs,
        out_specs=[
            pl.BlockSpec((nb, tb, D_B), lambda i, j: (i, j, 0)),
            pl.BlockSpec((nb, CONV_W - 1, 2 * D_B), lambda i, j: (i, 0, 0)),
            out_c_spec,
            pl.BlockSpec((nb, H_B, 1, DK_B), lambda i, j: (i, 0, 0, 0)),
            pl.BlockSpec((nb, H_B, 1, 1), lambda i, j: (i, 0, 0, 0)),
        ],
        out_shape=[
            jax.ShapeDtypeStruct((bsz, t, D_B), y_dtype),
            jax.ShapeDtypeStruct((bsz, CONV_W - 1, 2 * D_B), F32),
            jax.ShapeDtypeStruct((DEPTH, bsz, H_B, DK_B, DK_B), F32),
            jax.ShapeDtypeStruct((bsz, H_B, 1, DK_B), F32),
            jax.ShapeDtypeStruct((bsz, H_B, 1, 1), F32),
        ],
        scratch_shapes=[
            pltpu.VMEM((nb, tb + 8, 2 * D_B), F32),
            pltpu.VMEM((nb, H_B, DK_B, DK_B), F32),
            pltpu.VMEM((nb, H_B, 1, DK_B), F32),
            pltpu.VMEM((nb, H_B, 1, 128), F32),
        ],
        input_output_aliases=aliases,
        compiler_params=pltpu.CompilerParams(
            dimension_semantics=("parallel", "arbitrary"), vmem_limit_bytes=VMEM_LIMIT),
        name="mlstm",
    )(*args)


def _layer_weights(l, w):
    bf = lambda a: a.astype(BF16)
    row = lambda a: a.reshape(1, -1)
    w_in = w['w_in'][l]
    w_gates = jnp.pad(w_in[:, RWKV_COLS + ML_COLS:], ((0, 0), (0, GATE_COLS - 2 * H_B)))
    w_cat = jnp.concatenate([w_in[:, :RWKV_COLS], w_gates, w_in[:, RWKV_COLS:RWKV_COLS + ML_COLS]], axis=1)
    zero = jnp.zeros((R_W, D_A), F32)
    w2a = jnp.concatenate([jnp.concatenate([w['w2'][l], zero], axis=1),
                           jnp.concatenate([zero, w['a2'][l]], axis=1)], axis=0)
    ids = jnp.arange(D_A // 2) // N_A
    seg = (ids[:, None] == ids[None, :]).astype(BF16)
    gate_bias = jnp.pad(jnp.concatenate([w['i_bias'][l], w['f_bias'][l]]), (0, 128 - 2 * H_B)).reshape(1, 128)
    return dict(
        ln1=(row(w['ln1_g'][l]), row(w['ln1_b'][l])),
        ln3=(row(w['ln3_g'][l]), row(w['ln3_b'][l])),
        w_cat=bf(w_cat),
        rwkv=(row(w['mu_shift'][l]), row(w['w0'][l]), bf(w2a), row(w['a0'][l]), bf(w['g2'][l]), row(w['k_k'][l]),
              row(w['k_a'][l]), row(w['r_k'][l]), row(w['gn_g'][l]), row(w['gn_b'][l]), seg),
        mlstm=(w['conv_w'][l], row(w['conv_b'][l]), gate_bias, row(w['mh_g'][l])),
        out=(bf(w['w_out'][l][:D_A]), bf(w['w_out'][l][D_A:]), row(w['ln2_g'][l]), row(w['ln2_b'][l])),
    )


def _mixer_blocks(bsz, t):
    if t > RWKV_TIME_BLOCK:
        return RWKV_BATCH_ROWS, RWKV_TIME_BLOCK, 1, min(MLSTM_TIME_BLOCK, t)
    rows = min(MIXER_SHORT_BATCH_ROWS, bsz)
    return rows, t, rows, t


def _trunk(x, states, lw, ffn_w, nb_r, tb_r, nb_m, tb_m):
    bsz, t, _ = x.shape
    n = bsz * t
    tm_ffn, tm_out, tm_proj = min(FFN_TILE_ROWS, n), min(OUT_PROJ_TILE_ROWS, n), min(PROJ_TILE_ROWS, n)
    y_dtype = BF16 if t % 16 == 0 else F32
    lc = math.gcd(t, CHUNK)
    xf = x.reshape(n, D_MODEL)
    st_shift, st_wkv, st_conv, st_c, st_n, st_m = states
    new = [[] for _ in range(4)]
    wkv_out, c_out = None, None
    for l in range(DEPTH):
        wl = lw[l]
        x1 = _ffn_ln(xf, *ffn_w[0], *wl['ln1'], l=l, tm=tm_ffn)
        p3 = _in_proj(x1, wl['w_cat'], tm=tm_proj).reshape(bsz, t, P_COLS)
        ya, shift, wkv_out = _rwkv(p3, st_shift[l].reshape(bsz, 1, RWKV_COLS), st_wkv, l, wkv_out,
                                   wl['rwkv'], nb_r, tb_r, y_dtype)
        yb, conv, c_out, n_t, m_t = _mlstm(p3, st_conv[l], st_c, l, c_out,
                                           st_n[l].reshape(bsz, H_B, 1, DK_B), st_m[l].reshape(bsz, H_B, 1, 1),
                                           wl['mlstm'], nb_m, tb_m, lc, y_dtype)
        x2 = _out_proj_ln(x1, ya.reshape(n, D_A), yb.reshape(n, D_B), *wl['out'], tm=tm_out)
        xf = _ffn_ln(x2, *ffn_w[1], *wl['ln3'], l=l, tm=tm_ffn)
        for idx, s in enumerate((shift.reshape(bsz, RWKV_COLS), conv, n_t.reshape(bsz, H_B, DK_B),
                                 m_t.reshape(bsz, H_B))):
            new[idx].append(s)
    shift_o, conv_o, n_o, m_o = [jnp.stack(s) for s in new]
    return xf.reshape(bsz, t, D_MODEL), [shift_o, wkv_out, conv_o, c_out, n_o, m_o]


def kernel(x_prompt, x_sample, state_shift, state_wkv, state_conv, state_C, state_n, state_m,
           ffn1_wg, ffn1_wu, ffn1_wd, ln1_g, ln1_b, w_in, mu_shift, w0, w2, a0, a2, g2, k_k, k_a, r_k,
           gn_g, gn_b, conv_w, conv_b, i_bias, f_bias, mh_g, w_out, ln2_g, ln2_b,
           ffn2_wg, ffn2_wu, ffn2_wd, ln3_g, ln3_b):
    w = dict(ffn1_wg=ffn1_wg, ffn1_wu=ffn1_wu, ffn1_wd=ffn1_wd, ln1_g=ln1_g, ln1_b=ln1_b, w_in=w_in,
             mu_shift=mu_shift, w0=w0, w2=w2, a0=a0, a2=a2, g2=g2, k_k=k_k, k_a=k_a, r_k=r_k,
             gn_g=gn_g, gn_b=gn_b, conv_w=conv_w, conv_b=conv_b, i_bias=i_bias, f_bias=f_bias,
             mh_g=mh_g, w_out=w_out, ln2_g=ln2_g, ln2_b=ln2_b, ffn2_wg=ffn2_wg, ffn2_wu=ffn2_wu,
             ffn2_wd=ffn2_wd, ln3_g=ln3_g, ln3_b=ln3_b)
    lw = [_layer_weights(l, w) for l in range(DEPTH)]
    ffn_w = ((ffn1_wg, ffn1_wu, ffn1_wd), (ffn2_wg, ffn2_wu, ffn2_wd))
    bp = x_prompt.shape[0]
    dt = x_prompt.dtype
    init = [jnp.zeros((DEPTH, bp, RWKV_COLS), dt),
            jnp.zeros((DEPTH, bp, H_A, N_A, N_A), dt),
            jnp.zeros((DEPTH, bp, CONV_W - 1, 2 * D_B), dt),
            jnp.zeros((DEPTH, bp, H_B, DK_B, DK_B), dt),
            jnp.zeros((DEPTH, bp, H_B, DK_B), dt),
            jnp.zeros((DEPTH, bp, H_B), dt)]
    y_p, ps = _trunk(x_prompt, init, lw, ffn_w, *_mixer_blocks(*x_prompt.shape[:2]))
    y_s, ss = _trunk(x_sample, [state_shift, state_wkv, state_conv, state_C, state_n, state_m], lw, ffn_w,
                     *_mixer_blocks(*x_sample.shape[:2]))
    return (y_p, y_s, *ps, *ss)
```

```python
import functools
import math

import jax
import jax.numpy as jnp
from jax import lax
from jax.experimental import pallas as pl
from jax.experimental.pallas import tpu as pltpu

D_MODEL = 1024
DEPTH = 2
D_A = 512
N_A = 64
H_A = 8
D_B = 512
H_B = 4
DK_B = 128
R_W = 64
R_A = 64
R_G = 128
RWKV_COLS = 3 * D_A + R_W + R_A + R_G
CONV_W = 4
CHUNK = 64
D_FF = 2816
ALPHA = (2.0 * DEPTH) ** 0.25
LN_EPS = 1e-5
GN_EPS = 64e-5
MH_EPS = 1e-6

GATE_COLS = 256
ML_COLS = 4 * D_B
P_COLS = RWKV_COLS + GATE_COLS + ML_COLS
VMEM_LIMIT = 56 * 1024 * 1024
FFN_COL_BLOCK = 256
FFN_ROW_CHUNK = 512
FFN_LAST_CHUNK = 256
FFN_TILE_ROWS = 2048
OUT_PROJ_TILE_ROWS = 2048
PROJ_TILE_ROWS = 512
RWKV_ROWS_PER_TRIP = 32
RWKV_BATCH_ROWS = 8
RWKV_TIME_BLOCK = 64
MLSTM_TIME_BLOCK = 256
MIXER_SHORT_BATCH_ROWS = 16

F32 = jnp.float32
BF16 = jnp.bfloat16


def _bdot(a, b):
    return jnp.dot(a.astype(BF16), b.astype(BF16), preferred_element_type=F32)


def _seg_dot(x_bf16, seg_half):
    h = seg_half.shape[0]
    return jnp.concatenate([jnp.dot(x_bf16[:, :h], seg_half, preferred_element_type=F32),
                            jnp.dot(x_bf16[:, h:], seg_half, preferred_element_type=F32)], axis=1)


def _seg_sum(x, seg_half):
    return _seg_dot(x.astype(BF16), seg_half)


def _layer_norm_rows(y, g, b):
    mu = jnp.mean(y, axis=-1, keepdims=True)
    d = y - mu
    var = jnp.mean(d * d, axis=-1, keepdims=True)
    return d * lax.rsqrt(var + LN_EPS) * g + b


def _sigmoid(x):
    return 1.0 / (1.0 + jnp.exp(-x))


def _ffn_ln_kernel(x_ref, wg_ref, wu_ref, wd_ref, g_ref, b_ref, o_ref, xb_scr):
    j = pl.program_id(1)
    last = pl.num_programs(1) - 1
    tm = x_ref.shape[0]

    def row_chunks(size):
        return [pl.ds(start, size) for start in range(0, tm, size)]

    def partial_ffn(xb, wg, wu, wd):
        hg = jnp.dot(xb, wg, preferred_element_type=F32)
        hu = jnp.dot(xb, wu, preferred_element_type=F32)
        h = (hg * _sigmoid(hg)) * hu
        return jnp.dot(h.astype(BF16), wd, preferred_element_type=F32)

    def weights():
        return wg_ref[...].astype(BF16), wu_ref[...].astype(BF16), wd_ref[...].astype(BF16)

    @pl.when(j == 0)
    def _():
        wg, wu, wd = weights()
        for rows in row_chunks(FFN_ROW_CHUNK):
            xb = x_ref[rows, :].astype(BF16)
            xb_scr[rows, :] = xb
            o_ref[rows, :] = partial_ffn(xb, wg, wu, wd)

    @pl.when((j > 0) & (j < last))
    def _():
        wg, wu, wd = weights()
        for rows in row_chunks(FFN_ROW_CHUNK):
            o_ref[rows, :] += partial_ffn(xb_scr[rows, :], wg, wu, wd)

    @pl.when(j == last)
    def _():
        wg, wu, wd = weights()
        for rows in row_chunks(FFN_LAST_CHUNK):
            acc = o_ref[rows, :] + partial_ffn(xb_scr[rows, :], wg, wu, wd)
            o_ref[rows, :] = _layer_norm_rows(ALPHA * x_ref[rows, :] + 0.5 * acc, g_ref[...], b_ref[...])


def _ffn_ln(x, wg, wu, wd, g, b, l, tm):
    n = x.shape[0]
    const = lambda shape: pl.BlockSpec(shape, lambda i, j: (0, 0))
    return pl.pallas_call(
        _ffn_ln_kernel,
        grid=(n // tm, D_FF // FFN_COL_BLOCK),
        in_specs=[
            pl.BlockSpec((tm, D_MODEL), lambda i, j: (i, 0)),
            pl.BlockSpec((None, D_MODEL, FFN_COL_BLOCK), lambda i, j: (l, 0, j)),
            pl.BlockSpec((None, D_MODEL, FFN_COL_BLOCK), lambda i, j: (l, 0, j)),
            pl.BlockSpec((None, FFN_COL_BLOCK, D_MODEL), lambda i, j: (l, j, 0)),
            const((1, D_MODEL)), const((1, D_MODEL)),
        ],
        out_specs=pl.BlockSpec((tm, D_MODEL), lambda i, j: (i, 0)),
        out_shape=jax.ShapeDtypeStruct((n, D_MODEL), F32),
        scratch_shapes=[pltpu.VMEM((tm, D_MODEL), BF16)],
        compiler_params=pltpu.CompilerParams(
            dimension_semantics=("parallel", "arbitrary"), vmem_limit_bytes=VMEM_LIMIT),
        name="ffn_ln",
    )(x, wg, wu, wd, g, b)


def _in_proj_kernel(x_ref, w_ref, o_ref):
    o_ref[...] = jnp.dot(x_ref[...].astype(BF16), w_ref[...], preferred_element_type=F32)


def _in_proj(x, w, tm):
    n = x.shape[0]
    return pl.pallas_call(
        _in_proj_kernel,
        grid=(n // tm,),
        in_specs=[
            pl.BlockSpec((tm, D_MODEL), lambda i: (i, 0)),
            pl.BlockSpec((D_MODEL, P_COLS), lambda i: (0, 0)),
        ],
        out_specs=pl.BlockSpec((tm, P_COLS), lambda i: (i, 0)),
        out_shape=jax.ShapeDtypeStruct((n, P_COLS), F32),
        compiler_params=pltpu.CompilerParams(
            dimension_semantics=("parallel",), vmem_limit_bytes=VMEM_LIMIT),
        name="in_proj",
    )(x, w)


def _out_proj_ln_kernel(x_ref, ya_ref, yb_ref, wa_ref, wb_ref, g_ref, b_ref, o_ref):
    mix = (jnp.dot(ya_ref[...].astype(BF16), wa_ref[...], preferred_element_type=F32)
           + jnp.dot(yb_ref[...].astype(BF16), wb_ref[...], preferred_element_type=F32))
    o_ref[...] = _layer_norm_rows(ALPHA * x_ref[...] + mix, g_ref[...], b_ref[...])


def _out_proj_ln(x, ya, yb, wa, wb, g, b, tm):
    n = x.shape[0]
    return pl.pallas_call(
        _out_proj_ln_kernel,
        grid=(n // tm,),
        in_specs=[
            pl.BlockSpec((tm, D_MODEL), lambda i: (i, 0)),
            pl.BlockSpec((tm, D_A), lambda i: (i, 0)),
            pl.BlockSpec((tm, D_B), lambda i: (i, 0)),
            pl.BlockSpec((D_A, D_MODEL), lambda i: (0, 0)),
            pl.BlockSpec((D_B, D_MODEL), lambda i: (0, 0)),
            pl.BlockSpec((1, D_MODEL), lambda i: (0, 0)),
            pl.BlockSpec((1, D_MODEL), lambda i: (0, 0)),
        ],
        out_specs=pl.BlockSpec((tm, D_MODEL), lambda i: (i, 0)),
        out_shape=jax.ShapeDtypeStruct((n, D_MODEL), F32),
        compiler_params=pltpu.CompilerParams(
            dimension_semantics=("parallel",), vmem_limit_bytes=VMEM_LIMIT),
        name="out_proj_ln",
    )(x, ya, yb, wa, wb, g, b)


def _rwkv_kernel(*refs, nb, tb, n_alias):
    (p_ref, sp_ref, s0_ref, mu_ref, w0_ref, w2a_ref, a0_ref, g2_ref, kk_ref, ka_ref, rk_ref,
     gng_ref, gnb_ref, seg_ref) = refs[:14]
    (ya_ref, so_ref, st_ref,
     s_scr, carry_scr, nk_scr, p2_scr, q1_scr, rp_scr, v_scr, vsw_scr, ya1_scr, b_scr, yc1_scr, k_scr, w12_scr,
     cbr_scr, ckr_scr, bonus_scr, g_scr, sa_scr, q_scr, yo_scr) = refs[14 + n_alias:]
    j = pl.program_id(1)
    n_hp = H_A // 2

    @pl.when(j == 0)
    def _():
        for b in range(nb):
            for hp in range(n_hp):
                s_scr[b, hp] = jnp.concatenate([s0_ref[b, 2 * hp], s0_ref[b, 2 * hp + 1]], axis=-1)
        carry_scr[...] = sp_ref[...]
        sa_scr[...] = jnp.zeros_like(sa_scr)
        q_scr[...] = jnp.zeros_like(q_scr)
        yo_scr[...] = jnp.zeros_like(yo_scr)

    seg = seg_ref[...]
    rows = nb * tb
    lane128 = lax.broadcasted_iota(jnp.int32, (rows, 128), 1)
    rowid = lax.broadcasted_iota(jnp.int32, (tb, RWKV_COLS), 0)
    rowid_a = lax.broadcasted_iota(jnp.int32, (tb, D_A), 0)

    pms, rps, r_last = [], [], []
    for b in range(nb):
        p = p_ref[b]
        prev = jnp.where(rowid == 0, carry_scr[b], pltpu.roll(p, 1, axis=0))
        carry_scr[b] = p[tb - 1:tb, :]
        pm_b = p + (prev - p) * mu_ref[...]
        r_b = pm_b[:, 0:D_A]
        pms.append(pm_b)
        rps.append(jnp.where(rowid_a == 0, 0.0, pltpu.roll(r_b, 1, axis=0)))
        r_last.append(r_b[tb - 1:tb, :])
    pm = jnp.concatenate(pms, axis=0)
    r = pm[:, 0:D_A]
    k = pm[:, D_A:2 * D_A]
    v = pm[:, 2 * D_A:3 * D_A]
    z = pm[:, 3 * D_A:3 * D_A + 128]
    xg = pm[:, 3 * D_A + 128:RWKV_COLS]
    zt = jnp.where(lane128 < R_W, jnp.tanh(z), z)
    lr = _bdot(zt, w2a_ref[...])
    g_scr[...] = _bdot(_sigmoid(xg), g2_ref[...]).reshape(nb, tb, D_A)
    kk = k * kk_ref[...]
    ss = _seg_sum(kk * kk, seg)
    w = jnp.exp(-math.exp(-0.5) * _sigmoid(w0_ref[...] + lr[:, :D_A]))
    a = _sigmoid(a0_ref[...] + lr[:, D_A:])
    kk = kk * lax.rsqrt(jnp.maximum(ss, 1e-24))
    kmod = k * (1.0 + (a - 1.0) * ka_ref[...])
    bb = kk * a
    nxt = lambda x: pltpu.roll(x, rows - 1, axis=0)
    kk_n, w_n, bb_n = nxt(kk), nxt(w), nxt(bb)
    kr = kmod * r
    cc = _seg_dot(jnp.concatenate([bb * kk_n, kmod * kk_n, bb * r, kr, kr * rk_ref[...]], axis=0).astype(BF16), seg)
    cbk, ckk = cc[0:rows], cc[rows:2 * rows]
    bonus_scr[...] = (cc[4 * rows:5 * rows] * v).reshape(nb, tb, D_A)
    cbr_scr[...] = cc[2 * rows:3 * rows].reshape(nb, tb, D_A)
    ckr_scr[...] = cc[3 * rows:4 * rows].reshape(nb, tb, D_A)
    def put(scr, arr):
        for hp in range(n_hp):
            scr[:, hp] = arr[:, hp * 128:(hp + 1) * 128].reshape(nb, tb, 128)

    put(nk_scr, -kk)
    put(p2_scr, -(w * kk_n))
    put(q1_scr, w * r)
    put(rp_scr, jnp.concatenate(rps, axis=0))
    put(ya1_scr, bb * w_n - cbk * bb_n)
    put(b_scr, bb)
    put(yc1_scr, kmod * w_n - ckk * bb_n)
    put(k_scr, kmod)
    put(w12_scr, w * w_n)
    put(v_scr, v)
    for hp in range(n_hp):
        vsw_scr[:, hp] = pltpu.roll(v[:, hp * 128:(hp + 1) * 128], 64, axis=1).reshape(nb, tb, 128)

    row8 = lax.broadcasted_iota(jnp.int32, (8, 128), 0)
    lane8 = lax.broadcasted_iota(jnp.int32, (8, 128), 1)
    pair8 = lax.shift_right_logical(row8, 1)
    half8 = ((((row8 & 1) == 0) & (lane8 < 64)) | (((row8 & 1) == 1) & (lane8 >= 64))).astype(F32)
    mk3 = ((row8 >= 6) & ((((row8 & 1) == 0) & (lane8 < 64)) | (((row8 & 1) == 1) & (lane8 >= 64)))).astype(F32)
    row64 = lax.broadcasted_iota(jnp.int32, (8, 64), 0)

    def pair_tile(q0, q1, q2, q3):
        return jnp.where(pair8 == 0, q0, jnp.where(pair8 == 1, q1, jnp.where(pair8 == 2, q2, q3))) * half8

    chains = [(b, hp) for b in range(nb) for hp in range(n_hp)]
    nt_dims = (((1,), (1,)), ((), ()))
    tn_dims = (((0,), (0,)), ((), ()))
    grp = min(RWKV_ROWS_PER_TRIP, tb)
    n_pair = grp // 2

    def natural_rows(rows_h0, rows_h1):
        pad = [jnp.zeros((8 - len(rows_h0), N_A), F32)] if len(rows_h0) < 8 else []
        return jnp.concatenate([jnp.concatenate(rows_h0 + pad, axis=0),
                                jnp.concatenate(rows_h1 + pad, axis=0)], axis=1)

    def group(tg, carry):
        t0 = tg * grp

        def row(ref, c, t, n=8):
            return ref[c[0], c[1], pl.ds(t0 + t, n, stride=0), :]

        out = {c: [[] for _ in range(6)] for c in chains}
        for pi in range(n_pair):
            i = 2 * pi
            reds = {}
            for c in chains:
                a_mat = pair_tile(row(nk_scr, c, i), row(p2_scr, c, i), row(q1_scr, c, i), row(rp_scr, c, i))
                reds[c] = lax.dot_general(a_mat.astype(BF16), s_scr[c[0], c[1]].astype(BF16), nt_dims,
                                          preferred_element_type=F32)
            xs = {}
            for c in chains:
                x_mat = jnp.where(
                    row64 < 4, reds[c],
                    jnp.where(row64 == 4, row(v_scr, c, i)[:, :64],
                              jnp.where(row64 == 5, row(vsw_scr, c, i)[:, :64],
                                        jnp.where(row64 == 6, row(v_scr, c, i + 1)[:, :64],
                                                  row(vsw_scr, c, i + 1)[:, :64]))))
                xs[c] = x_mat.astype(BF16)
            for c in chains:
                y_mat = pair_tile(row(ya1_scr, c, i), row(b_scr, c, i + 1), row(yc1_scr, c, i), row(k_scr, c, i + 1))
                d_s = lax.dot_general(xs[c], y_mat.astype(BF16), tn_dims, preferred_element_type=F32)
                s_scr[c[0], c[1]] = s_scr[c[0], c[1]] * row(w12_scr, c, i, N_A) + d_s
                for slot, red_row in enumerate((0, 1, 4, 5, 6, 7)):
                    out[c][slot].append(reds[c][red_row:red_row + 1, :])
        for k in range(0, n_pair, 8):
            even0 = pl.multiple_of(tg * grp + 2 * k, 8)
            for c in chains:
                o = [rows_list[k:k + 8] for rows_list in out[c]]
                sa_scr[c[0], c[1], pl.ds(even0, 8, stride=2), :] = natural_rows(o[0], o[1])
                q_scr[c[0], c[1], pl.ds(even0, 8, stride=2), :] = natural_rows(o[2], o[3])
                yo_scr[c[0], c[1], pl.ds(even0 + 8, 8, stride=2), :] = natural_rows(o[4], o[5])
        return carry

    lax.fori_loop(0, tb // grp, group, 0)

    for (b, hp) in chains:
        rl = r_last[b][:, hp * 128:(hp + 1) * 128]
        a_mat = rl * mk3
        red = lax.dot_general(a_mat.astype(BF16), s_scr[b, hp].astype(BF16), nt_dims, preferred_element_type=F32)
        y_last = jnp.concatenate([red[6:7, :], red[7:8, :]], axis=1)
        yo_scr[b, hp, pl.ds(tb + 8, 8), :] = jnp.broadcast_to(y_last, (8, 128))

    def slab(scr, first):
        return jnp.concatenate(
            [jnp.concatenate([scr[b, hp, pl.ds(first, tb), :] for hp in range(n_hp)], axis=1) for b in range(nb)],
            axis=0)

    y_even = (slab(q_scr, 0) + slab(sa_scr, 0) * cbr_scr[...].reshape(rows, D_A)
              + slab(v_scr, 0) * ckr_scr[...].reshape(rows, D_A))
    parity = lax.broadcasted_iota(jnp.int32, (rows, D_A), 0) & 1
    y = jnp.where(parity == 0, y_even, slab(yo_scr, 9))
    mu = _seg_sum(y, seg) * (1.0 / N_A)
    d = y - mu
    var = _seg_sum(d * d, seg) * (1.0 / N_A)
    yn = d * lax.rsqrt(var + GN_EPS) * gng_ref[...] + gnb_ref[...]
    ya_ref[...] = ((yn.reshape(nb, tb, D_A) + bonus_scr[...]) * g_scr[...]).astype(ya_ref.dtype)

    @pl.when(j == pl.num_programs(1) - 1)
    def _():
        own = st_ref if n_alias else st_ref.at[0]
        for b in range(nb):
            for hp in range(n_hp):
                s_pair = s_scr[b, hp]
                own[b, 2 * hp] = s_pair[:, :N_A]
                own[b, 2 * hp + 1] = s_pair[:, N_A:]
        if not n_alias:
            st_ref[1:] = jnp.zeros((DEPTH - 1,) + tuple(st_ref.shape[1:]), F32)
        so_ref[...] = carry_scr[...]


def _rwkv(p3, shift_prev, wkv_all, l, wkv_out_prev, wts, nb, tb, y_dtype):
    bsz, t, _ = p3.shape
    n_hp = H_A // 2
    full = lambda shape: pl.BlockSpec(shape, lambda i, j: (0,) * len(shape))
    blk = lambda: pltpu.VMEM((nb, tb, D_A), F32)
    n_alias = 0 if wkv_out_prev is None else 1
    kern = functools.partial(_rwkv_kernel, nb=nb, tb=tb, n_alias=n_alias)
    state_spec = pl.BlockSpec((None, nb, H_A, N_A, N_A), lambda i, j: (l, i, 0, 0, 0))
    in_specs = [
        pl.BlockSpec((nb, tb, RWKV_COLS), lambda i, j: (i, j, 0)),
        pl.BlockSpec((nb, 1, RWKV_COLS), lambda i, j: (i, 0, 0)),
        state_spec,
        full((1, RWKV_COLS)), full((1, D_A)), full((128, 2 * D_A)), full((1, D_A)), full((R_G, D_A)),
        full((1, D_A)), full((1, D_A)), full((1, D_A)), full((1, D_A)), full((1, D_A)), full((D_A // 2, D_A // 2)),
    ]
    args = [p3, shift_prev, wkv_all, *wts]
    aliases = {}
    out_state_spec = pl.BlockSpec((DEPTH, nb, H_A, N_A, N_A), lambda i, j: (0, i, 0, 0, 0))
    if n_alias:
        in_specs.append(pl.BlockSpec(memory_space=pl.ANY))
        args.append(wkv_out_prev)
        aliases = {len(args) - 1: 2}
        out_state_spec = state_spec
    return pl.pallas_call(
        kern,
        grid=(bsz // nb, t // tb),
        in_specs=in_specs,
        out_specs=[
            pl.BlockSpec((nb, tb, D_A), lambda i, j: (i, j, 0)),
            pl.BlockSpec((nb, 1, RWKV_COLS), lambda i, j: (i, 0, 0)),
            out_state_spec,
        ],
        out_shape=[
            jax.ShapeDtypeStruct((bsz, t, D_A), y_dtype),
            jax.ShapeDtypeStruct((bsz, 1, RWKV_COLS), F32),
            jax.ShapeDtypeStruct((DEPTH, bsz, H_A, N_A, N_A), F32),
        ],
        scratch_shapes=[
            pltpu.VMEM((nb, n_hp, N_A, 128), F32),
            pltpu.VMEM((nb, 1, RWKV_COLS), F32),
            *[pltpu.VMEM((nb, n_hp, tb, 2 * N_A), F32) for _ in range(11)],
            *[blk() for _ in range(4)],
            *[pltpu.VMEM((nb, n_hp, tb + 24, 2 * N_A), F32) for _ in range(3)],
        ],
        input_output_aliases=aliases,
        compiler_params=pltpu.CompilerParams(
            dimension_semantics=("parallel", "arbitrary"), vmem_limit_bytes=VMEM_LIMIT),
        name="rwkv7",
    )(*args)


def _mlstm_kernel(*refs, nb, tb, lc, n_alias):
    (pm_ref, pg_ref, cp_ref, c0_ref, n0_ref, m0_ref, cw_ref, cb_ref, gb_ref, mhg_ref) = refs[:10]
    (yb_ref, co_ref, ct_ref, nt_ref, mt_ref, x_scr, c_scr, n_scr, m_scr) = refs[10 + n_alias:]
    j = pl.program_id(1)
    hi = lax.Precision.HIGHEST
    nch = tb // lc
    lc_shift = lc.bit_length() - 1

    @pl.when(j == 0)
    def _():
        c_scr[...] = c0_ref[...]
        n_scr[...] = n0_ref[...]
        m_scr[...] = jnp.broadcast_to(m0_ref[...], m_scr.shape)
        x_scr[:, 5:8, :] = cp_ref[...]

    rr = lax.broadcasted_iota(jnp.int32, (lc, lc), 0)
    cc = lax.broadcasted_iota(jnp.int32, (lc, lc), 1)
    causal = rr >= cc
    rb = lax.broadcasted_iota(jnp.int32, (tb, tb), 0)
    cb = lax.broadcasted_iota(jnp.int32, (tb, tb), 1)
    tri_blk = ((rb >= cb) & (lax.shift_right_logical(rb, lc_shift) == lax.shift_right_logical(cb, lc_shift))
               ).astype(F32)
    lane_g = lax.broadcasted_iota(jnp.int32, (tb, 128), 1)
    tn_dims = (((0,), (0,)), ((), ()))
    nt_dims = (((1,), (1,)), ((), ()))

    units = [(b, c, h) for b in range(nb) for c in range(nch) for h in range(H_B)]
    q_u, k_u, v_u, o_u = {}, {}, {}, {}
    li_col, li_row, b_col, b_row, b_l = {}, {}, {}, {}, {}

    for b in range(nb):
        x = pm_ref[b, :, 0:2 * D_B]
        x_scr[b, pl.ds(8, tb), :] = x
        conv = cb_ref[...] + x * cw_ref[3:4, :]
        for s in range(1, CONV_W):
            conv = conv + x_scr[b, pl.ds(8 - s, tb), :] * cw_ref[3 - s:4 - s, :]
        x_scr[b, 5:8, :] = x_scr[b, pl.ds(8 + tb - 3, 3), :]
        sc = conv * _sigmoid(conv)
        q_all = sc[:, :D_B]
        k_all = sc[:, D_B:] * (DK_B ** -0.5)
        v_all = pm_ref[b, :, 2 * D_B:3 * D_B]
        o_all = pm_ref[b, :, 3 * D_B:4 * D_B]
        gp = pg_ref[b, :, 0:128] + gb_ref[...]
        gates = jnp.where(lane_g < H_B, gp, jnp.minimum(gp, 0.0) - jnp.log1p(jnp.exp(-jnp.abs(gp))))
        csum_col = jnp.dot(tri_blk, gates, precision=hi, preferred_element_type=F32)
        gt = gates.T
        csum_row = lax.dot_general(gt, tri_blk, nt_dims, precision=hi, preferred_element_type=F32)
        for c in range(nch):
            rs = slice(c * lc, (c + 1) * lc)
            for h in range(H_B):
                u = (b, c, h)
                hs = slice(h * DK_B, (h + 1) * DK_B)
                q_u[u], k_u[u], v_u[u], o_u[u] = q_all[rs, hs], k_all[rs, hs], v_all[rs, hs], o_all[rs, hs]
                li_col[u] = gates[rs, h:h + 1]
                li_row[u] = gt[h:h + 1, rs]
                b_col[u] = csum_col[rs, H_B + h:H_B + h + 1]
                b_row[u] = csum_row[H_B + h:H_B + h + 1, rs]
                b_l[u] = csum_col[(c + 1) * lc - 1:(c + 1) * lc, H_B + h:H_B + h + 1]

    rep = lambda col: jnp.broadcast_to(col, (lc, DK_B))
    bc = {u: rep(b_col[u]) for u in units}
    lic = {u: rep(li_col[u]) for u in units}

    last_max = {u: jnp.max(b_l[u] - b_row[u] + li_row[u], axis=-1, keepdims=True) for u in units}
    m_prev, m_new = {}, {}
    for b in range(nb):
        for h in range(H_B):
            m_p = m_scr[b, h][:, 0:1]
            for c in range(nch):
                u = (b, c, h)
                m_prev[u] = m_p
                m_p = jnp.maximum(b_l[u] + m_p, last_max[u])
                m_new[u] = m_p
            m_scr[b, h] = jnp.broadcast_to(m_p, (1, 128))

    dmat = {u: jnp.where(causal, bc[u][:, :lc] - b_row[u] + li_row[u], -jnp.inf) for u in units}
    row_max = {u: rep(jnp.max(dmat[u], axis=-1, keepdims=True)) for u in units}
    m_t, g_inter, e_mat = {}, {}, {}
    for u in units:
        inter = bc[u] + m_prev[u]
        m_t[u] = jnp.maximum(inter, row_max[u])
        g_inter[u] = jnp.exp(inter - m_t[u])
        e_mat[u] = jnp.exp(dmat[u] - m_t[u][:, :lc])

    qk = {u: lax.dot_general(q_u[u].astype(BF16), k_u[u].astype(BF16), nt_dims, preferred_element_type=F32)
          for u in units}
    s_mat = {u: qk[u] * e_mat[u] for u in units}
    sv = {u: _bdot(s_mat[u], v_u[u]) for u in units}
    s_sum = {u: rep(jnp.sum(s_mat[u], axis=-1, keepdims=True)) for u in units}
    kw = {u: k_u[u] * jnp.exp(b_l[u] - bc[u] + lic[u] - m_new[u]) for u in units}
    kwv = {u: lax.dot_general(kw[u].astype(BF16), v_u[u].astype(BF16), tn_dims, preferred_element_type=F32)
           for u in units}

    q_c, qn_prod = {}, {}
    c_cur = {(b, h): c_scr[b, h] for b in range(nb) for h in range(H_B)}
    n_cur = {(b, h): n_scr[b, h] for b in range(nb) for h in range(H_B)}
    for c in range(nch):
        for b in range(nb):
            for h in range(H_B):
                u = (b, c, h)
                q_c[u] = _bdot(q_u[u], c_cur[(b, h)])
                qn_prod[u] = q_u[u] * n_cur[(b, h)]
                dec = jnp.exp(b_l[u] + m_prev[u] - m_new[u])
                c_cur[(b, h)] = dec * c_cur[(b, h)] + kwv[u]
                n_cur[(b, h)] = dec * n_cur[(b, h)] + jnp.sum(kw[u], axis=0, keepdims=True)
    for b in range(nb):
        for h in range(H_B):
            c_scr[b, h] = c_cur[(b, h)]
            n_scr[b, h] = n_cur[(b, h)]
    q_n = {u: rep(jnp.sum(qn_prod[u], axis=-1, keepdims=True)) for u in units}

    hh = {}
    for u in units:
        num = g_inter[u] * q_c[u] + sv[u]
        den = g_inter[u] * q_n[u] + s_sum[u]
        hh[u] = num / jnp.maximum(jnp.abs(den), jnp.exp(-m_t[u]))
    mu = {u: rep(jnp.sum(hh[u], axis=-1, keepdims=True)) * (1.0 / DK_B) for u in units}
    dev = {u: hh[u] - mu[u] for u in units}
    var = {u: rep(jnp.sum(dev[u] * dev[u], axis=-1, keepdims=True)) * (1.0 / DK_B) for u in units}
    for u in units:
        b, c, h = u
        hs = slice(h * DK_B, (h + 1) * DK_B)
        hn = dev[u] * lax.rsqrt(var[u] + MH_EPS) * mhg_ref[:, hs]
        yb_ref[b, c * lc:(c + 1) * lc, hs] = (_sigmoid(o_u[u]) * hn).astype(yb_ref.dtype)

    @pl.when(j == pl.num_programs(1) - 1)
    def _():
        if n_alias:
            ct_ref[...] = c_scr[...]
        else:
            ct_ref[0] = c_scr[...]
            ct_ref[1:] = jnp.zeros((DEPTH - 1,) + tuple(ct_ref.shape[1:]), F32)
        nt_ref[...] = n_scr[...]
        mt_ref[...] = m_scr[:, :, :, 0:1]
        co_ref[...] = x_scr[:, 5:8, :]


def _mlstm(p3, conv_prev, c_all, l, c_out_prev, n0, m0, wts, nb, tb, lc, y_dtype):
    bsz, t, _ = p3.shape
    full = lambda shape: pl.BlockSpec(shape, lambda i, j: (0,) * len(shape))
    n_alias = 0 if c_out_prev is None else 1
    kern = functools.partial(_mlstm_kernel, nb=nb, tb=tb, lc=lc, n_alias=n_alias)
    gate_blk = RWKV_COLS // GATE_COLS
    c_spec = pl.BlockSpec((None, nb, H_B, DK_B, DK_B), lambda i, j: (l, i, 0, 0, 0))
    in_specs = [
        pl.BlockSpec((nb, tb, ML_COLS), lambda i, j: (i, j, 1)),
        pl.BlockSpec((nb, tb, GATE_COLS), lambda i, j: (i, j, gate_blk)),
        pl.BlockSpec((nb, CONV_W - 1, 2 * D_B), lambda i, j: (i, 0, 0)),
        c_spec,
        pl.BlockSpec((nb, H_B, 1, DK_B), lambda i, j: (i, 0, 0, 0)),
        pl.BlockSpec((nb, H_B, 1, 1), lambda i, j: (i, 0, 0, 0)),
        full((CONV_W, 2 * D_B)), full((1, 2 * D_B)), full((1, 128)), full((1, D_B)),
    ]
    args = [p3, p3, conv_prev, c_all, n0, m0, *wts]
    aliases = {}
    out_c_spec = pl.BlockSpec((DEPTH, nb, H_B, DK_B, DK_B), lambda i, j: (0, i, 0, 0, 0))
    if n_alias:
        in_specs.append(pl.BlockSpec(memory_space=pl.ANY))
        args.append(c_out_prev)
        aliases = {len(args) - 1: 2}
        out_c_spec = c_spec
    return pl.pallas_call(
        kern,
        grid=(bsz // nb, t // tb),
        in_specs=in_specs,
        out_specs=[
            pl.BlockSpec((nb, tb, D_B), lambda i, j: (i, j, 0)),
            pl.BlockSpec((nb, CONV_W - 1, 2 * D_B), lambda i, j: (i, 0, 0)),
            out_c_spec,
            pl.BlockSpec((nb, H_B, 1, DK_B), lambda i, j: (i, 0, 0, 0)),
            pl.BlockSpec((nb, H_B, 1, 1), lambda i, j: (i, 0, 0, 0)),
        ],
        out_shape=[
            jax.ShapeDtypeStruct((bsz, t, D_B), y_dtype),
            jax.ShapeDtypeStruct((bsz, CONV_W - 1, 2 * D_B), F32),
            jax.ShapeDtypeStruct((DEPTH, bsz, H_B, DK_B, DK_B), F32),
            jax.ShapeDtypeStruct((bsz, H_B, 1, DK_B), F32),
            jax.ShapeDtypeStruct((bsz, H_B, 1, 1), F32),
        ],
        scratch_shapes=[
            pltpu.VMEM((nb, tb + 8, 2 * D_B), F32),
            pltpu.VMEM((nb, H_B, DK_B, DK_B), F32),
            pltpu.VMEM((nb, H_B, 1, DK_B), F32),
            pltpu.VMEM((nb, H_B, 1, 128), F32),
        ],
        input_output_aliases=aliases,
        compiler_params=pltpu.CompilerParams(
            dimension_semantics=("parallel", "arbitrary"), vmem_limit_bytes=VMEM_LIMIT),
        name="mlstm",
    )(*args)


def _layer_weights(l, w):
    bf = lambda a: a.astype(BF16)
    row = lambda a: a.reshape(1, -1)
    w_in = w['w_in'][l]
    w_gates = jnp.pad(w_in[:, RWKV_COLS + ML_COLS:], ((0, 0), (0, GATE_COLS - 2 * H_B)))
    w_cat = jnp.concatenate([w_in[:, :RWKV_COLS], w_gates, w_in[:, RWKV_COLS:RWKV_COLS + ML_COLS]], axis=1)
    zero = jnp.zeros((R_W, D_A), F32)
    w2a = jnp.concatenate([jnp.concatenate([w['w2'][l], zero], axis=1),
                           jnp.concatenate([zero, w['a2'][l]], axis=1)], axis=0)
    ids = jnp.arange(D_A // 2) // N_A
    seg = (ids[:, None] == ids[None, :]).astype(BF16)
    gate_bias = jnp.pad(jnp.concatenate([w['i_bias'][l], w['f_bias'][l]]), (0, 128 - 2 * H_B)).reshape(1, 128)
    return dict(
        ln1=(row(w['ln1_g'][l]), row(w['ln1_b'][l])),
        ln3=(row(w['ln3_g'][l]), row(w['ln3_b'][l])),
        w_cat=bf(w_cat),
        rwkv=(row(w['mu_shift'][l]), row(w['w0'][l]), bf(w2a), row(w['a0'][l]), bf(w['g2'][l]), row(w['k_k'][l]),
              row(w['k_a'][l]), row(w['r_k'][l]), row(w['gn_g'][l]), row(w['gn_b'][l]), seg),
        mlstm=(w['conv_w'][l], row(w['conv_b'][l]), gate_bias, row(w['mh_g'][l])),
        out=(bf(w['w_out'][l][:D_A]), bf(w['w_out'][l][D_A:]), row(w['ln2_g'][l]), row(w['ln2_b'][l])),
    )


def _mixer_blocks(bsz, t):
    if t > RWKV_TIME_BLOCK:
        return RWKV_BATCH_ROWS, RWKV_TIME_BLOCK, 1, min(MLSTM_TIME_BLOCK, t)
    rows = min(MIXER_SHORT_BATCH_ROWS, bsz)
    return rows, t, rows, t


def _trunk(x, states, lw, ffn_w, nb_r, tb_r, nb_m, tb_m):
    bsz, t, _ = x.shape
    n = bsz * t
    tm_ffn, tm_out, tm_proj = min(FFN_TILE_ROWS, n), min(OUT_PROJ_TILE_ROWS, n), min(PROJ_TILE_ROWS, n)
    y_dtype = BF16 if t % 16 == 0 else F32
    lc = math.gcd(t, CHUNK)
    xf = x.reshape(n, D_MODEL)
    st_shift, st_wkv, st_conv, st_c, st_n, st_m = states
    new = [[] for _ in range(4)]
    wkv_out, c_out = None, None
    for l in range(DEPTH):
        wl = lw[l]
        x1 = _ffn_ln(xf, *ffn_w[0], *wl['ln1'], l=l, tm=tm_ffn)
        p3 = _in_proj(x1, wl['w_cat'], tm=tm_proj).reshape(bsz, t, P_COLS)
        ya, shift, wkv_out = _rwkv(p3, st_shift[l].reshape(bsz, 1, RWKV_COLS), st_wkv, l, wkv_out,
                                   wl['rwkv'], nb_r, tb_r, y_dtype)
        yb, conv, c_out, n_t, m_t = _mlstm(p3, st_conv[l], st_c, l, c_out,
                                           st_n[l].reshape(bsz, H_B, 1, DK_B), st_m[l].reshape(bsz, H_B, 1, 1),
                                           wl['mlstm'], nb_m, tb_m, lc, y_dtype)
        x2 = _out_proj_ln(x1, ya.reshape(n, D_A), yb.reshape(n, D_B), *wl['out'], tm=tm_out)
        xf = _ffn_ln(x2, *ffn_w[1], *wl['ln3'], l=l, tm=tm_ffn)
        for idx, s in enumerate((shift.reshape(bsz, RWKV_COLS), conv, n_t.reshape(bsz, H_B, DK_B),
                                 m_t.reshape(bsz, H_B))):
            new[idx].append(s)
    shift_o, conv_o, n_o, m_o = [jnp.stack(s) for s in new]
    return xf.reshape(bsz, t, D_MODEL), [shift_o, wkv_out, conv_o, c_out, n_o, m_o]


def kernel(x_prompt, x_sample, state_shift, state_wkv, state_conv, state_C, state_n, state_m,
           ffn1_wg, ffn1_wu, ffn1_wd, ln1_g, ln1_b, w_in, mu_shift, w0, w2, a0, a2, g2, k_k, k_a, r_k,
           gn_g, gn_b, conv_w, conv_b, i_bias, f_bias, mh_g, w_out, ln2_g, ln2_b,
           ffn2_wg, ffn2_wu, ffn2_wd, ln3_g, ln3_b):
    w = dict(ffn1_wg=ffn1_wg, ffn1_wu=ffn1_wu, ffn1_wd=ffn1_wd, ln1_g=ln1_g, ln1_b=ln1_b, w_in=w_in,
             mu_shift=mu_shift, w0=w0, w2=w2, a0=a0, a2=a2, g2=g2, k_k=k_k, k_a=k_a, r_k=r_k,
             gn_g=gn_g, gn_b=gn_b, conv_w=conv_w, conv_b=conv_b, i_bias=i_bias, f_bias=f_bias,
             mh_g=mh_g, w_out=w_out, ln2_g=ln2_g, ln2_b=ln2_b, ffn2_wg=ffn2_wg, ffn2_wu=ffn2_wu,
             ffn2_wd=ffn2_wd, ln3_g=ln3_g, ln3_b=ln3_b)
    lw = [_layer_weights(l, w) for l in range(DEPTH)]
    ffn_w = ((ffn1_wg, ffn1_wu, ffn1_wd), (ffn2_wg, ffn2_wu, ffn2_wd))
    bp = x_prompt.shape[0]
    dt = x_prompt.dtype
    init = [jnp.zeros((DEPTH, bp, RWKV_COLS), dt),
            jnp.zeros((DEPTH, bp, H_A, N_A, N_A), dt),
            jnp.zeros((DEPTH, bp, CONV_W - 1, 2 * D_B), dt),
            jnp.zeros((DEPTH, bp, H_B, DK_B, DK_B), dt),
            jnp.zeros((DEPTH, bp, H_B, DK_B), dt),
            jnp.zeros((DEPTH, bp, H_B), dt)]
    y_p, ps = _trunk(x_prompt, init, lw, ffn_w, *_mixer_blocks(*x_prompt.shape[:2]))
    y_s, ss = _trunk(x_sample, [state_shift, state_wkv, state_conv, state_C, state_n, state_m], lw, ffn_w,
                     *_mixer_blocks(*x_sample.shape[:2]))
    return (y_p, y_s, *ps, *ss)
```

```python
import functools
import math

import jax
import jax.numpy as jnp
from jax import lax
from jax.experimental import pallas as pl
from jax.experimental.pallas import tpu as pltpu

D_MODEL = 1024
DEPTH = 2
D_A = 512
N_A = 64
H_A = 8
D_B = 512
H_B = 4
DK_B = 128
R_W = 64
R_A = 64
R_G = 128
RWKV_COLS = 3 * D_A + R_W + R_A + R_G
CONV_W = 4
CHUNK = 64
D_FF = 2816
ALPHA = (2.0 * DEPTH) ** 0.25
LN_EPS = 1e-5
GN_EPS = 64e-5
MH_EPS = 1e-6

GATE_COLS = 256
ML_COLS = 4 * D_B
P_COLS = RWKV_COLS + GATE_COLS + ML_COLS
VMEM_LIMIT = 56 * 1024 * 1024
FFN_COL_BLOCK = 256
FFN_ROW_CHUNK = 512
FFN_LAST_CHUNK = 256
FFN_TILE_ROWS = 2048
OUT_PROJ_TILE_ROWS = 2048
PROJ_TILE_ROWS = 512
RWKV_ROWS_PER_TRIP = 32
RWKV_BATCH_ROWS = 8
RWKV_TIME_BLOCK = 64
MLSTM_TIME_BLOCK = 256
MIXER_SHORT_BATCH_ROWS = 16

F32 = jnp.float32
BF16 = jnp.bfloat16


def _bdot(a, b):
    return jnp.dot(a.astype(BF16), b.astype(BF16), preferred_element_type=F32)


def _seg_dot(x_bf16, seg_half):
    h = seg_half.shape[0]
    return jnp.concatenate([jnp.dot(x_bf16[:, :h], seg_half, preferred_element_type=F32),
                            jnp.dot(x_bf16[:, h:], seg_half, preferred_element_type=F32)], axis=1)


def _seg_sum(x, seg_half):
    return _seg_dot(x.astype(BF16), seg_half)


def _layer_norm_rows(y, g, b):
    mu = jnp.mean(y, axis=-1, keepdims=True)
    d = y - mu
    var = jnp.mean(d * d, axis=-1, keepdims=True)
    return d * lax.rsqrt(var + LN_EPS) * g + b


def _sigmoid(x):
    return 1.0 / (1.0 + jnp.exp(-x))


def _ffn_ln_kernel(x_ref, wg_hbm, wu_hbm, wd_hbm, g_ref, b_ref, o_ref, xb_scr, step_scr, *, l):
    last = D_FF // FFN_COL_BLOCK - 1
    tm = x_ref.shape[0]
    w_in = pl.BlockSpec((D_MODEL, FFN_COL_BLOCK), lambda j: (0, j))
    w_down = pl.BlockSpec((FFN_COL_BLOCK, D_MODEL), lambda j: (j, 0))
    step_scr[0] = 0
    pltpu.emit_pipeline(
        functools.partial(_ffn_ln_step, x_ref, g_ref, b_ref, o_ref, xb_scr, step_scr, last, tm),
        grid=(last + 1,), in_specs=[w_in, w_in, w_down], out_specs=[],
    )(wg_hbm.at[l], wu_hbm.at[l], wd_hbm.at[l])


def _ffn_ln_step(x_ref, g_ref, b_ref, o_ref, xb_scr, step_scr, last, tm, wg_ref, wu_ref, wd_ref):
    j = step_scr[0]
    step_scr[0] = j + 1

    def row_chunks(size):
        return [pl.ds(start, size) for start in range(0, tm, size)]

    def partial_ffn(xb, wg, wu, wd):
        hg = jnp.dot(xb, wg, preferred_element_type=F32)
        hu = jnp.dot(xb, wu, preferred_element_type=F32)
        h = (hg * _sigmoid(hg)) * hu
        return jnp.dot(h.astype(BF16), wd, preferred_element_type=F32)

    def weights():
        return wg_ref[...].astype(BF16), wu_ref[...].astype(BF16), wd_ref[...].astype(BF16)

    @pl.when(j == 0)
    def _():
        wg, wu, wd = weights()
        for rows in row_chunks(FFN_ROW_CHUNK):
            xb = x_ref[rows, :].astype(BF16)
            xb_scr[rows, :] = xb
            o_ref[rows, :] = partial_ffn(xb, wg, wu, wd)

    @pl.when((j > 0) & (j < last))
    def _():
        wg, wu, wd = weights()
        for rows in row_chunks(FFN_ROW_CHUNK):
            o_ref[rows, :] += partial_ffn(xb_scr[rows, :], wg, wu, wd)

    @pl.when(j == last)
    def _():
        wg, wu, wd = weights()
        for rows in row_chunks(FFN_LAST_CHUNK):
            acc = o_ref[rows, :] + partial_ffn(xb_scr[rows, :], wg, wu, wd)
            o_ref[rows, :] = _layer_norm_rows(ALPHA * x_ref[rows, :] + 0.5 * acc, g_ref[...], b_ref[...])


def _ffn_ln(x, wg, wu, wd, g, b, l, tm):
    n = x.shape[0]
    const = lambda shape: pl.BlockSpec(shape, lambda i: (0, 0))
    hbm = pl.BlockSpec(memory_space=pl.ANY)
    return pl.pallas_call(
        functools.partial(_ffn_ln_kernel, l=l),
        grid=(n // tm,),
        in_specs=[
            pl.BlockSpec((tm, D_MODEL), lambda i: (i, 0)),
            hbm, hbm, hbm,
            const((1, D_MODEL)), const((1, D_MODEL)),
        ],
        out_specs=pl.BlockSpec((tm, D_MODEL), lambda i: (i, 0)),
        out_shape=jax.ShapeDtypeStruct((n, D_MODEL), F32),
        scratch_shapes=[pltpu.VMEM((tm, D_MODEL), BF16), pltpu.SMEM((1,), jnp.int32)],
        compiler_params=pltpu.CompilerParams(
            dimension_semantics=("parallel",), vmem_limit_bytes=VMEM_LIMIT),
        name="ffn_ln",
    )(x, wg, wu, wd, g, b)


def _in_proj_kernel(x_ref, w_ref, o_ref):
    o_ref[...] = jnp.dot(x_ref[...].astype(BF16), w_ref[...], preferred_element_type=F32)


def _in_proj(x, w, tm):
    n = x.shape[0]
    return pl.pallas_call(
        _in_proj_kernel,
        grid=(n // tm,),
        in_specs=[
            pl.BlockSpec((tm, D_MODEL), lambda i: (i, 0)),
            pl.BlockSpec((D_MODEL, P_COLS), lambda i: (0, 0)),
        ],
        out_specs=pl.BlockSpec((tm, P_COLS), lambda i: (i, 0)),
        out_shape=jax.ShapeDtypeStruct((n, P_COLS), F32),
        compiler_params=pltpu.CompilerParams(
            dimension_semantics=("parallel",), vmem_limit_bytes=VMEM_LIMIT),
        name="in_proj",
    )(x, w)


def _out_proj_ln_kernel(x_ref, ya_ref, yb_ref, wa_ref, wb_ref, g_ref, b_ref, o_ref):
    mix = (jnp.dot(ya_ref[...].astype(BF16), wa_ref[...], preferred_element_type=F32)
           + jnp.dot(yb_ref[...].astype(BF16), wb_ref[...], preferred_element_type=F32))
    o_ref[...] = _layer_norm_rows(ALPHA * x_ref[...] + mix, g_ref[...], b_ref[...])


def _out_proj_ln(x, ya, yb, wa, wb, g, b, tm):
    n = x.shape[0]
    return pl.pallas_call(
        _out_proj_ln_kernel,
        grid=(n // tm,),
        in_specs=[
            pl.BlockSpec((tm, D_MODEL), lambda i: (i, 0)),
            pl.BlockSpec((tm, D_A), lambda i: (i, 0)),
            pl.BlockSpec((tm, D_B), lambda i: (i, 0)),
            pl.BlockSpec((D_A, D_MODEL), lambda i: (0, 0)),
            pl.BlockSpec((D_B, D_MODEL), lambda i: (0, 0)),
            pl.BlockSpec((1, D_MODEL), lambda i: (0, 0)),
            pl.BlockSpec((1, D_MODEL), lambda i: (0, 0)),
        ],
        out_specs=pl.BlockSpec((tm, D_MODEL), lambda i: (i, 0)),
        out_shape=jax.ShapeDtypeStruct((n, D_MODEL), F32),
        compiler_params=pltpu.CompilerParams(
            dimension_semantics=("parallel",), vmem_limit_bytes=VMEM_LIMIT),
        name="out_proj_ln",
    )(x, ya, yb, wa, wb, g, b)


def _rwkv_kernel(*refs, nb, tb, n_alias):
    (p_ref, sp_ref, s0_ref, mu_ref, w0_ref, w2a_ref, a0_ref, g2_ref, kk_ref, ka_ref, rk_ref,
     gng_ref, gnb_ref, seg_ref) = refs[:14]
    (ya_ref, so_ref, st_ref,
     s_scr, carry_scr, nk_scr, p2_scr, q1_scr, rp_scr, v_scr, vsw_scr, ya1_scr, b_scr, yc1_scr, k_scr, w12_scr,
     cbr_scr, ckr_scr, bonus_scr, g_scr, sa_scr, q_scr, yo_scr) = refs[14 + n_alias:]
    j = pl.program_id(1)
    n_hp = H_A // 2

    @pl.when(j == 0)
    def _():
        for b in range(nb):
            for hp in range(n_hp):
                s_scr[b, hp] = jnp.concatenate([s0_ref[b, 2 * hp], s0_ref[b, 2 * hp + 1]], axis=-1)
        carry_scr[...] = sp_ref[...]
        sa_scr[...] = jnp.zeros_like(sa_scr)
        q_scr[...] = jnp.zeros_like(q_scr)
        yo_scr[...] = jnp.zeros_like(yo_scr)

    seg = seg_ref[...]
    rows = nb * tb
    lane128 = lax.broadcasted_iota(jnp.int32, (rows, 128), 1)
    rowid = lax.broadcasted_iota(jnp.int32, (tb, RWKV_COLS), 0)
    rowid_a = lax.broadcasted_iota(jnp.int32, (tb, D_A), 0)

    pms, rps, r_last = [], [], []
    for b in range(nb):
        p = p_ref[b]
        prev = jnp.where(rowid == 0, carry_scr[b], pltpu.roll(p, 1, axis=0))
        carry_scr[b] = p[tb - 1:tb, :]
        pm_b = p + (prev - p) * mu_ref[...]
        r_b = pm_b[:, 0:D_A]
        pms.append(pm_b)
        rps.append(jnp.where(rowid_a == 0, 0.0, pltpu.roll(r_b, 1, axis=0)))
        r_last.append(r_b[tb - 1:tb, :])
    pm = jnp.concatenate(pms, axis=0)
    r = pm[:, 0:D_A]
    k = pm[:, D_A:2 * D_A]
    v = pm[:, 2 * D_A:3 * D_A]
    z = pm[:, 3 * D_A:3 * D_A + 128]
    xg = pm[:, 3 * D_A + 128:RWKV_COLS]
    zt = jnp.where(lane128 < R_W, jnp.tanh(z), z)
    lr = _bdot(zt, w2a_ref[...])
    g_scr[...] = _bdot(_sigmoid(xg), g2_ref[...]).reshape(nb, tb, D_A)
    kk = k * kk_ref[...]
    ss = _seg_sum(kk * kk, seg)
    w = jnp.exp(-math.exp(-0.5) * _sigmoid(w0_ref[...] + lr[:, :D_A]))
    a = _sigmoid(a0_ref[...] + lr[:, D_A:])
    kk = kk * lax.rsqrt(jnp.maximum(ss, 1e-24))
    kmod = k * (1.0 + (a - 1.0) * ka_ref[...])
    bb = kk * a
    nxt = lambda x: pltpu.roll(x, rows - 1, axis=0)
    kk_n, w_n, bb_n = nxt(kk), nxt(w), nxt(bb)
    kr = kmod * r
    cc = _seg_dot(jnp.concatenate([bb * kk_n, kmod * kk_n, bb * r, kr, kr * rk_ref[...]], axis=0).astype(BF16), seg)
    cbk, ckk = cc[0:rows], cc[rows:2 * rows]
    bonus_scr[...] = (cc[4 * rows:5 * rows] * v).reshape(nb, tb, D_A)
    cbr_scr[...] = cc[2 * rows:3 * rows].reshape(nb, tb, D_A)
    ckr_scr[...] = cc[3 * rows:4 * rows].reshape(nb, tb, D_A)
    def put(scr, arr):
        for hp in range(n_hp):
            scr[:, hp] = arr[:, hp * 128:(hp + 1) * 128].reshape(nb, tb, 128)

    put(nk_scr, -kk)
    put(p2_scr, -(w * kk_n))
    put(q1_scr, w * r)
    put(rp_scr, jnp.concatenate(rps, axis=0))
    put(ya1_scr, bb * w_n - cbk * bb_n)
    put(b_scr, bb)
    put(yc1_scr, kmod * w_n - ckk * bb_n)
    put(k_scr, kmod)
    put(w12_scr, w * w_n)
    put(v_scr, v)
    for hp in range(n_hp):
        vsw_scr[:, hp] = pltpu.roll(v[:, hp * 128:(hp + 1) * 128], 64, axis=1).reshape(nb, tb, 128)

    row8 = lax.broadcasted_iota(jnp.int32, (8, 128), 0)
    lane8 = lax.broadcasted_iota(jnp.int32, (8, 128), 1)
    pair8 = lax.shift_right_logical(row8, 1)
    half8 = ((((row8 & 1) == 0) & (lane8 < 64)) | (((row8 & 1) == 1) & (lane8 >= 64))).astype(F32)
    mk3 = ((row8 >= 6) & ((((row8 & 1) == 0) & (lane8 < 64)) | (((row8 & 1) == 1) & (lane8 >= 64)))).astype(F32)
    row64 = lax.broadcasted_iota(jnp.int32, (8, 64), 0)

    def pair_tile(q0, q1, q2, q3):
        return jnp.where(pair8 == 0, q0, jnp.where(pair8 == 1, q1, jnp.where(pair8 == 2, q2, q3))) * half8

    chains = [(b, hp) for b in range(nb) for hp in range(n_hp)]
    nt_dims = (((1,), (1,)), ((), ()))
    tn_dims = (((0,), (0,)), ((), ()))
    grp = min(RWKV_ROWS_PER_TRIP, tb)
    n_pair = grp // 2

    def natural_rows(rows_h0, rows_h1):
        pad = [jnp.zeros((8 - len(rows_h0), N_A), F32)] if len(rows_h0) < 8 else []
        return jnp.concatenate([jnp.concatenate(rows_h0 + pad, axis=0),
                                jnp.concatenate(rows_h1 + pad, axis=0)], axis=1)

    def group(tg, carry):
        t0 = tg * grp

        def row(ref, c, t, n=8):
            return ref[c[0], c[1], pl.ds(t0 + t, n, stride=0), :]

        out = {c: [[] for _ in range(6)] for c in chains}
        for pi in range(n_pair):
            i = 2 * pi
            reds = {}
            for c in chains:
                a_mat = pair_tile(row(nk_scr, c, i), row(p2_scr, c, i), row(q1_scr, c, i), row(rp_scr, c, i))
                reds[c] = lax.dot_general(a_mat.astype(BF16), s_scr[c[0], c[1]].astype(BF16), nt_dims,
                                          preferred_element_type=F32)
            xs = {}
            for c in chains:
                x_mat = jnp.where(
                    row64 < 4, reds[c],
                    jnp.where(row64 == 4, row(v_scr, c, i)[:, :64],
                              jnp.where(row64 == 5, row(vsw_scr, c, i)[:, :64],
                                        jnp.where(row64 == 6, row(v_scr, c, i + 1)[:, :64],
                                                  row(vsw_scr, c, i + 1)[:, :64]))))
                xs[c] = x_mat.astype(BF16)
            for c in chains:
                y_mat = pair_tile(row(ya1_scr, c, i), row(b_scr, c, i + 1), row(yc1_scr, c, i), row(k_scr, c, i + 1))
                d_s = lax.dot_general(xs[c], y_mat.astype(BF16), tn_dims, preferred_element_type=F32)
                s_scr[c[0], c[1]] = s_scr[c[0], c[1]] * row(w12_scr, c, i, N_A) + d_s
                for slot, red_row in enumerate((0, 1, 4, 5, 6, 7)):
                    out[c][slot].append(reds[c][red_row:red_row + 1, :])
        for k in range(0, n_pair, 8):
            even0 = pl.multiple_of(tg * grp + 2 * k, 8)
            for c in chains:
                o = [rows_list[k:k + 8] for rows_list in out[c]]
                sa_scr[c[0], c[1], pl.ds(even0, 8, stride=2), :] = natural_rows(o[0], o[1])
                q_scr[c[0], c[1], pl.ds(even0, 8, stride=2), :] = natural_rows(o[2], o[3])
                yo_scr[c[0], c[1], pl.ds(even0 + 8, 8, stride=2), :] = natural_rows(o[4], o[5])
        return carry

    lax.fori_loop(0, tb // grp, group, 0)

    for (b, hp) in chains:
        rl = r_last[b][:, hp * 128:(hp + 1) * 128]
        a_mat = rl * mk3
        red = lax.dot_general(a_mat.astype(BF16), s_scr[b, hp].astype(BF16), nt_dims, preferred_element_type=F32)
        y_last = jnp.concatenate([red[6:7, :], red[7:8, :]], axis=1)
        yo_scr[b, hp, pl.ds(tb + 8, 8), :] = jnp.broadcast_to(y_last, (8, 128))

    def slab(scr, first):
        return jnp.concatenate(
            [jnp.concatenate([scr[b, hp, pl.ds(first, tb), :] for hp in range(n_hp)], axis=1) for b in range(nb)],
            axis=0)

    y_even = (slab(q_scr, 0) + slab(sa_scr, 0) * cbr_scr[...].reshape(rows, D_A)
              + slab(v_scr, 0) * ckr_scr[...].reshape(rows, D_A))
    parity = lax.broadcasted_iota(jnp.int32, (rows, D_A), 0) & 1
    y = jnp.where(parity == 0, y_even, slab(yo_scr, 9))
    mu = _seg_sum(y, seg) * (1.0 / N_A)
    d = y - mu
    var = _seg_sum(d * d, seg) * (1.0 / N_A)
    yn = d * lax.rsqrt(var + GN_EPS) * gng_ref[...] + gnb_ref[...]
    ya_ref[...] = ((yn.reshape(nb, tb, D_A) + bonus_scr[...]) * g_scr[...]).astype(ya_ref.dtype)

    @pl.when(j == pl.num_programs(1) - 1)
    def _():
        own = st_ref if n_alias else st_ref.at[0]
        for b in range(nb):
            for hp in range(n_hp):
                s_pair = s_scr[b, hp]
                own[b, 2 * hp] = s_pair[:, :N_A]
                own[b, 2 * hp + 1] = s_pair[:, N_A:]
        if not n_alias:
            st_ref[1:] = jnp.zeros((DEPTH - 1,) + tuple(st_ref.shape[1:]), F32)
        so_ref[...] = carry_scr[...]


def _rwkv(p3, shift_prev, wkv_all, l, wkv_out_prev, wts, nb, tb, y_dtype):
    bsz, t, _ = p3.shape
    n_hp = H_A // 2
    full = lambda shape: pl.BlockSpec(shape, lambda i, j: (0,) * len(shape))
    blk = lambda: pltpu.VMEM((nb, tb, D_A), F32)
    n_alias = 0 if wkv_out_prev is None else 1
    kern = functools.partial(_rwkv_kernel, nb=nb, tb=tb, n_alias=n_alias)
    state_spec = pl.BlockSpec((None, nb, H_A, N_A, N_A), lambda i, j: (l, i, 0, 0, 0))
    in_specs = [
        pl.BlockSpec((nb, tb, RWKV_COLS), lambda i, j: (i, j, 0)),
        pl.BlockSpec((nb, 1, RWKV_COLS), lambda i, j: (i, 0, 0)),
        state_spec,
        full((1, RWKV_COLS)), full((1, D_A)), full((128, 2 * D_A)), full((1, D_A)), full((R_G, D_A)),
        full((1, D_A)), full((1, D_A)), full((1, D_A)), full((1, D_A)), full((1, D_A)), full((D_A // 2, D_A // 2)),
    ]
    args = [p3, shift_prev, wkv_all, *wts]
    aliases = {}
    out_state_spec = pl.BlockSpec((DEPTH, nb, H_A, N_A, N_A), lambda i, j: (0, i, 0, 0, 0))
    if n_alias:
        in_specs.append(pl.BlockSpec(memory_space=pl.ANY))
        args.append(wkv_out_prev)
        aliases = {len(args) - 1: 2}
        out_state_spec = state_spec
    return pl.pallas_call(
        kern,
        grid=(bsz // nb, t // tb),
        in_specs=in_specs,
        out_specs=[
            pl.BlockSpec((nb, tb, D_A), lambda i, j: (i, j, 0)),
            pl.BlockSpec((nb, 1, RWKV_COLS), lambda i, j: (i, 0, 0)),
            out_state_spec,
        ],
        out_shape=[
            jax.ShapeDtypeStruct((bsz, t, D_A), y_dtype),
            jax.ShapeDtypeStruct((bsz, 1, RWKV_COLS), F32),
            jax.ShapeDtypeStruct((DEPTH, bsz, H_A, N_A, N_A), F32),
        ],
        scratch_shapes=[
            pltpu.VMEM((nb, n_hp, N_A, 128), F32),
            pltpu.VMEM((nb, 1, RWKV_COLS), F32),
            *[pltpu.VMEM((nb, n_hp, tb, 2 * N_A), F32) for _ in range(11)],
            *[blk() for _ in range(4)],
            *[pltpu.VMEM((nb, n_hp, tb + 24, 2 * N_A), F32) for _ in range(3)],
        ],
        input_output_aliases=aliases,
        compiler_params=pltpu.CompilerParams(
            dimension_semantics=("parallel", "arbitrary"), vmem_limit_bytes=VMEM_LIMIT),
        name="rwkv7",
    )(*args)


def _mlstm_kernel(*refs, nb, tb, lc, n_alias):
    (pm_ref, pg_ref, cp_ref, c0_ref, n0_ref, m0_ref, cw_ref, cb_ref, gb_ref, mhg_ref) = refs[:10]
    (yb_ref, co_ref, ct_ref, nt_ref, mt_ref, x_scr, c_scr, n_scr, m_scr) = refs[10 + n_alias:]
    j = pl.program_id(1)
    hi = lax.Precision.HIGHEST
    nch = tb // lc
    lc_shift = lc.bit_length() - 1

    @pl.when(j == 0)
    def _():
        c_scr[...] = c0_ref[...]
        n_scr[...] = n0_ref[...]
        m_scr[...] = jnp.broadcast_to(m0_ref[...], m_scr.shape)
        x_scr[:, 5:8, :] = cp_ref[...]

    rr = lax.broadcasted_iota(jnp.int32, (lc, lc), 0)
    cc = lax.broadcasted_iota(jnp.int32, (lc, lc), 1)
    causal = rr >= cc
    rb = lax.broadcasted_iota(jnp.int32, (tb, tb), 0)
    cb = lax.broadcasted_iota(jnp.int32, (tb, tb), 1)
    tri_blk = ((rb >= cb) & (lax.shift_right_logical(rb, lc_shift) == lax.shift_right_logical(cb, lc_shift))
               ).astype(F32)
    lane_g = lax.broadcasted_iota(jnp.int32, (tb, 128), 1)
    tn_dims = (((0,), (0,)), ((), ()))
    nt_dims = (((1,), (1,)), ((), ()))

    units = [(b, c, h) for b in range(nb) for c in range(nch) for h in range(H_B)]
    q_u, k_u, v_u, o_u = {}, {}, {}, {}
    li_col, li_row, b_col, b_row, b_l = {}, {}, {}, {}, {}

    for b in range(nb):
        x = pm_ref[b, :, 0:2 * D_B]
        x_scr[b, pl.ds(8, tb), :] = x
        conv = cb_ref[...] + x * cw_ref[3:4, :]
        for s in range(1, CONV_W):
            conv = conv + x_scr[b, pl.ds(8 - s, tb), :] * cw_ref[3 - s:4 - s, :]
        x_scr[b, 5:8, :] = x_scr[b, pl.ds(8 + tb - 3, 3), :]
        sc = conv * _sigmoid(conv)
        q_all = sc[:, :D_B]
        k_all = sc[:, D_B:] * (DK_B ** -0.5)
        v_all = pm_ref[b, :, 2 * D_B:3 * D_B]
        o_all = pm_ref[b, :, 3 * D_B:4 * D_B]
        gp = pg_ref[b, :, 0:128] + gb_ref[...]
        gates = jnp.where(lane_g < H_B, gp, jnp.minimum(gp, 0.0) - jnp.log1p(jnp.exp(-jnp.abs(gp))))
        csum_col = jnp.dot(tri_blk, gates, precision=hi, preferred_element_type=F32)
        gt = gates.T
        csum_row = csum_col.T
        for c in range(nch):
            rs = slice(c * lc, (c + 1) * lc)
            for h in range(H_B):
                u = (b, c, h)
                hs = slice(h * DK_B, (h + 1) * DK_B)
                q_u[u], k_u[u], v_u[u], o_u[u] = q_all[rs, hs], k_all[rs, hs], v_all[rs, hs], o_all[rs, hs]
                li_col[u] = gates[rs, h:h + 1]
                li_row[u] = gt[h:h + 1, rs]
                b_col[u] = csum_col[rs, H_B + h:H_B + h + 1]
                b_row[u] = csum_row[H_B + h:H_B + h + 1, rs]
                b_l[u] = csum_col[(c + 1) * lc - 1:(c + 1) * lc, H_B + h:H_B + h + 1]

    rep = lambda col: jnp.broadcast_to(col, (lc, DK_B))
    bc = {u: rep(b_col[u]) for u in units}
    lic = {u: rep(li_col[u]) for u in units}

    last_max = {u: jnp.max(b_l[u] - b_row[u] + li_row[u], axis=-1, keepdims=True) for u in units}
    m_prev, m_new = {}, {}
    for b in range(nb):
        for h in range(H_B):
            m_p = m_scr[b, h][:, 0:1]
            for c in range(nch):
                u = (b, c, h)
                m_prev[u] = m_p
                m_p = jnp.maximum(b_l[u] + m_p, last_max[u])
                m_new[u] = m_p
            m_scr[b, h] = jnp.broadcast_to(m_p, (1, 128))

    dmat = {u: jnp.where(causal, bc[u][:, :lc] - b_row[u] + li_row[u], -jnp.inf) for u in units}
    row_max = {u: rep(jnp.max(dmat[u], axis=-1, keepdims=True)) for u in units}
    m_t, g_inter, e_mat = {}, {}, {}
    for u in units:
        inter = bc[u] + m_prev[u]
        m_t[u] = jnp.maximum(inter, row_max[u])
        g_inter[u] = jnp.exp(inter - m_t[u])
        e_mat[u] = jnp.exp(dmat[u] - m_t[u][:, :lc])

    qk = {u: lax.dot_general(q_u[u].astype(BF16), k_u[u].astype(BF16), nt_dims, preferred_element_type=F32)
          for u in units}
    s_mat = {u: qk[u] * e_mat[u] for u in units}
    sv = {u: _bdot(s_mat[u], v_u[u]) for u in units}
    s_sum = {u: rep(jnp.sum(s_mat[u], axis=-1, keepdims=True)) for u in units}
    kw = {u: k_u[u] * jnp.exp(b_l[u] - bc[u] + lic[u] - m_new[u]) for u in units}
    kwv = {u: lax.dot_general(kw[u].astype(BF16), v_u[u].astype(BF16), tn_dims, preferred_element_type=F32)
           for u in units}

    q_c, qn_prod = {}, {}
    c_cur = {(b, h): c_scr[b, h] for b in range(nb) for h in range(H_B)}
    n_cur = {(b, h): n_scr[b, h] for b in range(nb) for h in range(H_B)}
    for c in range(nch):
        for b in range(nb):
            for h in range(H_B):
                u = (b, c, h)
                q_c[u] = _bdot(q_u[u], c_cur[(b, h)])
                qn_prod[u] = q_u[u] * n_cur[(b, h)]
                dec = jnp.exp(b_l[u] + m_prev[u] - m_new[u])
                c_cur[(b, h)] = dec * c_cur[(b, h)] + kwv[u]
                n_cur[(b, h)] = dec * n_cur[(b, h)] + jnp.sum(kw[u], axis=0, keepdims=True)
    for b in range(nb):
        for h in range(H_B):
            c_scr[b, h] = c_cur[(b, h)]
            n_scr[b, h] = n_cur[(b, h)]
    q_n = {u: rep(jnp.sum(qn_prod[u], axis=-1, keepdims=True)) for u in units}

    hh = {}
    for u in units:
        num = g_inter[u] * q_c[u] + sv[u]
        den = g_inter[u] * q_n[u] + s_sum[u]
        hh[u] = num / jnp.maximum(jnp.abs(den), jnp.exp(-m_t[u]))
    mu = {u: rep(jnp.sum(hh[u], axis=-1, keepdims=True)) * (1.0 / DK_B) for u in units}
    dev = {u: hh[u] - mu[u] for u in units}
    var = {u: rep(jnp.sum(dev[u] * dev[u], axis=-1, keepdims=True)) * (1.0 / DK_B) for u in units}
    for u in units:
        b, c, h = u
        hs = slice(h * DK_B, (h + 1) * DK_B)
        hn = dev[u] * lax.rsqrt(var[u] + MH_EPS) * mhg_ref[:, hs]
        yb_ref[b, c * lc:(c + 1) * lc, hs] = (_sigmoid(o_u[u]) * hn).astype(yb_ref.dtype)

    @pl.when(j == pl.num_programs(1) - 1)
    def _():
        if n_alias:
            ct_ref[...] = c_scr[...]
        else:
            ct_ref[0] = c_scr[...]
            ct_ref[1:] = jnp.zeros((DEPTH - 1,) + tuple(ct_ref.shape[1:]), F32)
        nt_ref[...] = n_scr[...]
        mt_ref[...] = m_scr[:, :, :, 0:1]
        co_ref[...] = x_scr[:, 5:8, :]


def _mlstm(p3, conv_prev, c_all, l, c_out_prev, n0, m0, wts, nb, tb, lc, y_dtype):
    bsz, t, _ = p3.shape
    full = lambda shape: pl.BlockSpec(shape, lambda i, j: (0,) * len(shape))
    n_alias = 0 if c_out_prev is None else 1
    kern = functools.partial(_mlstm_kernel, nb=nb, tb=tb, lc=lc, n_alias=n_alias)
    gate_blk = RWKV_COLS // GATE_COLS
    c_spec = pl.BlockSpec((None, nb, H_B, DK_B, DK_B), lambda i, j: (l, i, 0, 0, 0))
    in_specs = [
        pl.BlockSpec((nb, tb, ML_COLS), lambda i, j: (i, j, 1)),
        pl.BlockSpec((nb, tb, GATE_COLS), lambda i, j: (i, j, gate_blk)),
        pl.BlockSpec((nb, CONV_W - 1, 2 * D_B), lambda i, j: (i, 0, 0)),
        c_spec,
        pl.BlockSpec((nb, H_B, 1, DK_B), lambda i, j: (i, 0, 0, 0)),
        pl.BlockSpec((nb, H_B, 1, 1), lambda i, j: (i, 0, 0, 0)),
        full((CONV_W, 2 * D_B)), full((1, 2 * D_B)), full((1, 128)), full((1, D_B)),
    ]
    args = [p3, p3, conv_prev, c_all, n0, m0, *wts]
    aliases = {}
    out_c_spec = pl.BlockSpec((DEPTH, nb, H_B, DK_B, DK_B), lambda i, j: (0, i, 0, 0, 0))
    if n_alias:
        in_specs.append(pl.BlockSpec(memory_space=pl.ANY))
        args.append(c_out_prev)
        aliases = {len(args) - 1: 2}
        out_c_spec = c_spec
    return pl.pallas_call(
        kern,
        grid=(bsz // nb, t // tb),
        in_specs=in_specs,
        out_specs=[
            pl.BlockSpec((nb, tb, D_B), lambda i, j: (i, j, 0)),
            pl.BlockSpec((nb, CONV_W - 1, 2 * D_B), lambda i, j: (i, 0, 0)),
            out_c_spec,
            pl.BlockSpec((nb, H_B, 1, DK_B), lambda i, j: (i, 0, 0, 0)),
            pl.BlockSpec((nb, H_B, 1, 1), lambda i, j: (i, 0, 0, 0)),
        ],
        out_shape=[
            jax.ShapeDtypeStruct((bsz, t, D_B), y_dtype),
            jax.ShapeDtypeStruct((bsz, CONV_W - 1, 2 * D_B), F32),
            jax.ShapeDtypeStruct((DEPTH, bsz, H_B, DK_B, DK_B), F32),
            jax.ShapeDtypeStruct((bsz, H_B, 1, DK_B), F32),
            jax.ShapeDtypeStruct((bsz, H_B, 1, 1), F32),
        ],
        scratch_shapes=[
            pltpu.VMEM((nb, tb + 8, 2 * D_B), F32),
            pltpu.VMEM((nb, H_B, DK_B, DK_B), F32),
            pltpu.VMEM((nb, H_B, 1, DK_B), F32),
            pltpu.VMEM((nb, H_B, 1, 128), F32),
        ],
        input_output_aliases=aliases,
        compiler_params=pltpu.CompilerParams(
            dimension_semantics=("parallel", "arbitrary"), vmem_limit_bytes=VMEM_LIMIT),
        name="mlstm",
    )(*args)


def _layer_weights(l, w):
    bf = lambda a: a.astype(BF16)
    row = lambda a: a.reshape(1, -1)
    w_in = w['w_in'][l]
    w_gates = jnp.pad(w_in[:, RWKV_COLS + ML_COLS:], ((0, 0), (0, GATE_COLS - 2 * H_B)))
    w_cat = jnp.concatenate([w_in[:, :RWKV_COLS], w_gates, w_in[:, RWKV_COLS:RWKV_COLS + ML_COLS]], axis=1)
    zero = jnp.zeros((R_W, D_A), F32)
    w2a = jnp.concatenate([jnp.concatenate([w['w2'][l], zero], axis=1),
                           jnp.concatenate([zero, w['a2'][l]], axis=1)], axis=0)
    ids = jnp.arange(D_A // 2) // N_A
    seg = (ids[:, None] == ids[None, :]).astype(BF16)
    gate_bias = jnp.pad(jnp.concatenate([w['i_bias'][l], w['f_bias'][l]]), (0, 128 - 2 * H_B)).reshape(1, 128)
    return dict(
        ln1=(row(w['ln1_g'][l]), row(w['ln1_b'][l])),
        ln3=(row(w['ln3_g'][l]), row(w['ln3_b'][l])),
        w_cat=bf(w_cat),
        rwkv=(row(w['mu_shift'][l]), row(w['w0'][l]), bf(w2a), row(w['a0'][l]), bf(w['g2'][l]), row(w['k_k'][l]),
              row(w['k_a'][l]), row(w['r_k'][l]), row(w['gn_g'][l]), row(w['gn_b'][l]), seg),
        mlstm=(w['conv_w'][l], row(w['conv_b'][l]), gate_bias, row(w['mh_g'][l])),
        out=(bf(w['w_out'][l][:D_A]), bf(w['w_out'][l][D_A:]), row(w['ln2_g'][l]), row(w['ln2_b'][l])),
    )


def _mixer_blocks(bsz, t):
    if t > RWKV_TIME_BLOCK:
        return RWKV_BATCH_ROWS, RWKV_TIME_BLOCK, 1, min(MLSTM_TIME_BLOCK, t)
    rows = min(MIXER_SHORT_BATCH_ROWS, bsz)
    return rows, t, rows, t


def _trunk(x, states, lw, ffn_w, nb_r, tb_r, nb_m, tb_m):
    bsz, t, _ = x.shape
    n = bsz * t
    tm_ffn, tm_out, tm_proj = min(FFN_TILE_ROWS, n), min(OUT_PROJ_TILE_ROWS, n), min(PROJ_TILE_ROWS, n)
    y_dtype = BF16 if t % 16 == 0 else F32
    lc = math.gcd(t, CHUNK)
    xf = x.reshape(n, D_MODEL)
    st_shift, st_wkv, st_conv, st_c, st_n, st_m = states
    new = [[] for _ in range(4)]
    wkv_out, c_out = None, None
    for l in range(DEPTH):
        wl = lw[l]
        x1 = _ffn_ln(xf, *ffn_w[0], *wl['ln1'], l=l, tm=tm_ffn)
        p3 = _in_proj(x1, wl['w_cat'], tm=tm_proj).reshape(bsz, t, P_COLS)
        ya, shift, wkv_out = _rwkv(p3, st_shift[l].reshape(bsz, 1, RWKV_COLS), st_wkv, l, wkv_out,
                                   wl['rwkv'], nb_r, tb_r, y_dtype)
        yb, conv, c_out, n_t, m_t = _mlstm(p3, st_conv[l], st_c, l, c_out,
                                           st_n[l].reshape(bsz, H_B, 1, DK_B), st_m[l].reshape(bsz, H_B, 1, 1),
                                           wl['mlstm'], nb_m, tb_m, lc, y_dtype)
        x2 = _out_proj_ln(x1, ya.reshape(n, D_A), yb.reshape(n, D_B), *wl['out'], tm=tm_out)
        xf = _ffn_ln(x2, *ffn_w[1], *wl['ln3'], l=l, tm=tm_ffn)
        for idx, s in enumerate((shift.reshape(bsz, RWKV_COLS), conv, n_t.reshape(bsz, H_B, DK_B),
                                 m_t.reshape(bsz, H_B))):
            new[idx].append(s)
    shift_o, conv_o, n_o, m_o = [jnp.stack(s) for s in new]
    return xf.reshape(bsz, t, D_MODEL), [shift_o, wkv_out, conv_o, c_out, n_o, m_o]


def kernel(x_prompt, x_sample, state_shift, state_wkv, state_conv, state_C, state_n, state_m,
           ffn1_wg, ffn1_wu, ffn1_wd, ln1_g, ln1_b, w_in, mu_shift, w0, w2, a0, a2, g2, k_k, k_a, r_k,
           gn_g, gn_b, conv_w, conv_b, i_bias, f_bias, mh_g, w_out, ln2_g, ln2_b,
           ffn2_wg, ffn2_wu, ffn2_wd, ln3_g, ln3_b):
    w = dict(ffn1_wg=ffn1_wg, ffn1_wu=ffn1_wu, ffn1_wd=ffn1_wd, ln1_g=ln1_g, ln1_b=ln1_b, w_in=w_in,
             mu_shift=mu_shift, w0=w0, w2=w2, a0=a0, a2=a2, g2=g2, k_k=k_k, k_a=k_a, r_k=r_k,
             gn_g=gn_g, gn_b=gn_b, conv_w=conv_w, conv_b=conv_b, i_bias=i_bias, f_bias=f_bias,
             mh_g=mh_g, w_out=w_out, ln2_g=ln2_g, ln2_b=ln2_b, ffn2_wg=ffn2_wg, ffn2_wu=ffn2_wu,
             ffn2_wd=ffn2_wd, ln3_g=ln3_g, ln3_b=ln3_b)
    lw = [_layer_weights(l, w) for l in range(DEPTH)]
    ffn_w = ((ffn1_wg, ffn1_wu, ffn1_wd), (ffn2_wg, ffn2_wu, ffn2_wd))
    bp = x_prompt.shape[0]
    dt = x_prompt.dtype
    init = [jnp.zeros((DEPTH, bp, RWKV_COLS), dt),
            jnp.zeros((DEPTH, bp, H_A, N_A, N_A), dt),
            jnp.zeros((DEPTH, bp, CONV_W - 1, 2 * D_B), dt),
            jnp.zeros((DEPTH, bp, H_B, DK_B, DK_B), dt),
            jnp.zeros((DEPTH, bp, H_B, DK_B), dt),
            jnp.zeros((DEPTH, bp, H_B), dt)]
    y_p, ps = _trunk(x_prompt, init, lw, ffn_w, *_mixer_blocks(*x_prompt.shape[:2]))
    y_s, ss = _trunk(x_sample, [state_shift, state_wkv, state_conv, state_C, state_n, state_m], lw, ffn_w,
                     *_mixer_blocks(*x_sample.shape[:2]))
    return (y_p, y_s, *ps, *ss)
```
